```python
import numpy as np
import jax, jax.numpy as jnp
from jax import lax

D_MODEL = 2048
BATCH = 1
SEQ = 8192
DEPTH = 1

HG_HEADS = 8
HG_DK = 128
HG_DV = 128
HG_FDIM = HG_HEADS * HG_DK
HG_WIDTH = HG_HEADS * HG_DV
HG_CHUNK = 64
NSA_HEADS = 16
NSA_KV = 4
NSA_HG = NSA_HEADS // NSA_KV
HEAD_DIM = 128
NSA_WIDTH = NSA_HEADS * HEAD_DIM
NSA_KVW = NSA_KV * HEAD_DIM
CMP_LEN = 32
CMP_STRIDE = 16
CMP_HIDDEN = 512
SEL_LEN = 64
SEL_TOPK = 16
WINDOW = 512
Q_BLOCK = 128
ROPE_THETA = 500000.0
ROT_DIM = HEAD_DIM // 4
EPS = 1e-6
IN_SPLITS = (HG_FDIM, HG_FDIM, HG_WIDTH, HG_WIDTH,
             NSA_WIDTH,
             NSA_KVW, NSA_KVW, NSA_KVW, NSA_KVW, NSA_KVW, NSA_KVW,
             3 * NSA_HEADS, NSA_WIDTH,
             2 * D_MODEL)
IN_COLS = sum(IN_SPLITS)

kernel_name = "hgrn2_nsa_gated_hybrid"


def rmsnorm(x, w):
    xf = x.astype(jnp.float32)
    return xf * lax.rsqrt(jnp.mean(xf * xf, axis=-1, keepdims=True) + EPS) * w.astype(jnp.float32)


def partial_rope(x, pos):
    half = ROT_DIM // 2
    inv = ROPE_THETA ** (-jnp.arange(0, ROT_DIM, 2, dtype=jnp.float32) / ROT_DIM)
    ang = pos[:, None] * inv[None, :]
    cos, sin = jnp.cos(ang)[:, None, :], jnp.sin(ang)[:, None, :]
    xf = x.astype(jnp.float32)
    x1, x2, rest = xf[..., :half], xf[..., half:ROT_DIM], xf[..., ROT_DIM:]
    return jnp.concatenate([x1 * cos - x2 * sin, x2 * cos + x1 * sin, rest], axis=-1)


def masked_softmax(s, m):
    s = jnp.where(m, s.astype(jnp.float32), -1e30)
    return jnp.where(m, jax.nn.softmax(s, axis=-1), 0.0)


def hgrn2_chunkwise(q, f_pre, i, lb):
    B, S, H, dk = q.shape
    dv = i.shape[-1]
    C = HG_CHUNK
    N = S // C
    q = q.astype(jnp.float32)
    f = lb + (1.0 - lb) * jax.nn.sigmoid(f_pre.astype(jnp.float32))
    log_f = jnp.log(f)
    k = 1.0 - f

    def to_chunks(a):
        return a.reshape(B, N, C, H, a.shape[-1]).transpose(1, 0, 3, 2, 4)

    causal = jnp.tril(jnp.ones((C, C), dtype=bool))[:, :, None]

    def step(state, xs):
        qc, kc, vc, gc = xs
        b = jnp.cumsum(gc, axis=2)
        inter = jnp.einsum('bhtd,bhdv->bhtv', qc * jnp.exp(b), state)
        diff = b[:, :, :, None, :] - b[:, :, None, :, :]
        decay = jnp.exp(jnp.where(causal, diff, -jnp.inf))
        scores = jnp.einsum('bhtd,bhsd,bhtsd->bhts', qc, kc, decay)
        intra = jnp.einsum('bhts,bhsv->bhtv', scores, vc)
        b_last = b[:, :, -1:, :]
        state = jnp.exp(b_last[:, :, 0, :])[..., None] * state + jnp.einsum(
            'bhsd,bhsv->bhdv', kc * jnp.exp(b_last - b), vc)
        return state, inter + intra

    state0 = jnp.zeros((B, H, dk, dv), jnp.float32)
    _, o = lax.scan(step, state0, (to_chunks(q), to_chunks(k),
                                   to_chunks(i.astype(jnp.float32)), to_chunks(log_f)))
    return o.transpose(1, 0, 3, 2, 4).reshape(B, S, H, dv)


def compress_blocks(k_raw, pos_emb, w1, b1, w2):
    B, S, G, dh = k_raw.shape
    Nc = (S - CMP_LEN) // CMP_STRIDE + 1
    idx = np.arange(Nc)[:, None] * CMP_STRIDE + np.arange(CMP_LEN)[None, :]
    blocks = k_raw[:, idx] + pos_emb[None, None, :, None, :]
    flat = blocks.transpose(0, 1, 3, 2, 4).reshape(B, Nc, G, CMP_LEN * dh)
    return jax.nn.gelu(flat @ w1 + b1) @ w2


def nsa_attention(q, kc, vc, k_slc, v_slc, k_win, v_win, gates):
    B, S, G, Hg, dh = q.shape
    Nc = kc.shape[1]
    Nsel = S // SEL_LEN
    topk = min(SEL_TOPK, Nsel)
    nblk = S // Q_BLOCK
    scale = HEAD_DIM ** -0.5
    cmp_end = jnp.arange(Nc) * CMP_STRIDE + CMP_LEN - 1
    ci = np.arange(Nc)[:, None] * CMP_STRIDE
    sj = np.arange(Nsel)[None, :] * SEL_LEN
    overlap = jnp.asarray(((ci < sj + SEL_LEN) & (ci + CMP_LEN > sj)).astype(np.float32))
    kb = k_slc.reshape(B, Nsel, SEL_LEN, G, dh).transpose(0, 3, 1, 2, 4)
    vb = v_slc.reshape(B, Nsel, SEL_LEN, G, dh).transpose(0, 3, 1, 2, 4)
    kw = jnp.pad(k_win, ((0, 0), (WINDOW, 0), (0, 0), (0, 0)))
    vw = jnp.pad(v_win, ((0, 0), (WINDOW, 0), (0, 0), (0, 0)))
    bi = jnp.arange(B)[:, None, None, None]
    gi = jnp.arange(G)[None, :, None, None]
    jsel = jnp.arange(Nsel)

    def block(n):
        q0 = n * Q_BLOCK
        t = q0 + jnp.arange(Q_BLOCK)
        qb = lax.dynamic_slice_in_dim(q, q0, Q_BLOCK, axis=1).astype(jnp.float32) * scale
        gb = jax.nn.sigmoid(lax.dynamic_slice_in_dim(gates, q0, Q_BLOCK, axis=1).astype(jnp.float32))
        s = jnp.einsum('bqghd,bcgd->bghqc', qb, kc)
        p_cmp = masked_softmax(s, cmp_end[None, :] <= t[:, None])
        o_cmp = jnp.einsum('bghqc,bcgd->bqghd', p_cmp, vc)
        imp = jnp.einsum('bghqc,cj->bgqj', p_cmp, overlap)
        jt = t // SEL_LEN
        valid = jsel[None, :] * SEL_LEN <= t[:, None]
        forced = (jsel[None, :] == 0) | (jsel[None, :] == jt[:, None]) | (jsel[None, :] == jt[:, None] - 1)
        score = jnp.where(valid, jnp.where(forced, jnp.inf, imp), -1.0)
        top_val, top_idx = lax.top_k(score, topk)
        ks = kb[bi, gi, top_idx]
        vs = vb[bi, gi, top_idx]
        s = jnp.einsum('bqghd,bgqnkd->bghqnk', qb, ks).reshape(B, G, Hg, Q_BLOCK, topk * SEL_LEN)
        key_pos = top_idx[..., None] * SEL_LEN + jnp.arange(SEL_LEN)
        m = (key_pos <= t[None, None, :, None, None]) & (top_val >= 0)[..., None]
        p = masked_softmax(s, m.reshape(B, G, 1, Q_BLOCK, topk * SEL_LEN))
        o_slc = jnp.einsum('bghqnk,bgqnkd->bqghd', p.reshape(B, G, Hg, Q_BLOCK, topk, SEL_LEN), vs)
        kwb = lax.dynamic_slice_in_dim(kw, q0, Q_BLOCK + WINDOW, axis=1)
        vwb = lax.dynamic_slice_in_dim(vw, q0, Q_BLOCK + WINDOW, axis=1)
        kpos = q0 - WINDOW + jnp.arange(Q_BLOCK + WINDOW)
        m = (kpos[None, :] <= t[:, None]) & (kpos[None, :] > t[:, None] - WINDOW) & (kpos[None, :] >= 0)
        p = masked_softmax(jnp.einsum('bqghd,bkgd->bghqk', qb, kwb), m)
        o_win = jnp.einsum('bghqk,bkgd->bqghd', p, vwb)
        return gb[..., 0:1] * o_cmp + gb[..., 1:2] * o_slc + gb[..., 2:3] * o_win

    out = lax.map(block, jnp.arange(nblk))
    return out.transpose(1, 0, 2, 3, 4, 5).reshape(B, S, G * Hg * dh)


def setup_inputs(seed: int = 0) -> dict:
    key = jax.random.key(seed)
    ks = jax.random.split(key, 18)
    nrm = lambda k, shape, s: jax.random.normal(k, shape, jnp.float32) * s
    L = DEPTH
    return {
        "x": nrm(ks[0], (BATCH, SEQ, D_MODEL), 1.0),
        "norm_w": 1.0 + nrm(ks[1], (L, D_MODEL), 0.02),
        "w_in": nrm(ks[2], (L, D_MODEL, IN_COLS), D_MODEL ** -0.5),
        "hg_lb_logits": nrm(ks[3], (L + 1, HG_FDIM), 0.5),
        "hg_norm_w": 1.0 + nrm(ks[4], (L, HG_DV), 0.02),
        "cmp_k_pos": nrm(ks[5], (L, CMP_LEN, HEAD_DIM), 0.1),
        "cmp_k_w1": nrm(ks[6], (L, CMP_LEN * HEAD_DIM, CMP_HIDDEN), (CMP_LEN * HEAD_DIM) ** -0.5),
        "cmp_k_b1": nrm(ks[7], (L, CMP_HIDDEN), 0.01),
        "cmp_k_w2": nrm(ks[8], (L, CMP_HIDDEN, HEAD_DIM), CMP_HIDDEN ** -0.5),
        "cmp_v_pos": nrm(ks[9], (L, CMP_LEN, HEAD_DIM), 0.1),
        "cmp_v_w1": nrm(ks[10], (L, CMP_LEN * HEAD_DIM, CMP_HIDDEN), (CMP_LEN * HEAD_DIM) ** -0.5),
        "cmp_v_b1": nrm(ks[11], (L, CMP_HIDDEN), 0.01),
        "cmp_v_w2": nrm(ks[12], (L, CMP_HIDDEN, HEAD_DIM), CMP_HIDDEN ** -0.5),
        "w_branch_hg": nrm(ks[13], (L, HG_WIDTH, D_MODEL), HG_WIDTH ** -0.5),
        "w_branch_nsa": nrm(ks[14], (L, NSA_WIDTH, D_MODEL), NSA_WIDTH ** -0.5),
        "w_out": nrm(ks[15], (L, D_MODEL, D_MODEL), D_MODEL ** -0.5),
        "final_norm_w": 1.0 + nrm(ks[16], (D_MODEL,), 0.02),
    }


def reference(x, norm_w, w_in, hg_lb_logits, hg_norm_w, cmp_k_pos, cmp_k_w1, cmp_k_b1, cmp_k_w2,
              cmp_v_pos, cmp_v_w1, cmp_v_b1, cmp_v_w2, w_branch_hg, w_branch_nsa, w_out, final_norm_w):
    B, S, _ = x.shape
    pos = jnp.arange(S, dtype=jnp.float32)
    lower_bounds = jnp.cumsum(jax.nn.softmax(hg_lb_logits.astype(jnp.float32), axis=0), axis=0)
    offsets = np.cumsum(IN_SPLITS)[:-1].tolist()
    h = x
    for layer in range(DEPTH):
        xn = rmsnorm(h, norm_w[layer]).astype(x.dtype)
        proj = xn @ w_in[layer]
        (hg_q, hg_f, hg_i, hg_z, nsa_q, k_cmp, v_cmp, k_slc, v_slc, k_win, v_win,
         nsa_g, nsa_z, merge_g) = jnp.split(proj, offsets, axis=-1)
        lb = lower_bounds[layer].reshape(HG_HEADS, HG_DK)
        o_hg = hgrn2_chunkwise(hg_q.reshape(B, S, HG_HEADS, HG_DK), hg_f.reshape(B, S, HG_HEADS, HG_DK),
                               hg_i.reshape(B, S, HG_HEADS, HG_DV), lb)
        o_hg = rmsnorm(o_hg, hg_norm_w[layer]) * jax.nn.silu(
            hg_z.astype(jnp.float32).reshape(B, S, HG_HEADS, HG_DV))
        y_hg = o_hg.reshape(B, S, HG_WIDTH).astype(x.dtype) @ w_branch_hg[layer]
        q = partial_rope(nsa_q.reshape(B, S, NSA_HEADS, HEAD_DIM), pos).reshape(B, S, NSA_KV, NSA_HG, HEAD_DIM)
        kv = lambda a: a.reshape(B, S, NSA_KV, HEAD_DIM)
        kc = compress_blocks(partial_rope(kv(k_cmp), pos), cmp_k_pos[layer], cmp_k_w1[layer],
                             cmp_k_b1[layer], cmp_k_w2[layer])
        vc = compress_blocks(kv(v_cmp).astype(jnp.float32), cmp_v_pos[layer], cmp_v_w1[layer],
                             cmp_v_b1[layer], cmp_v_w2[layer])
        o_nsa = nsa_attention(q, kc, vc, partial_rope(kv(k_slc), pos), kv(v_slc),
                              partial_rope(kv(k_win), pos), kv(v_win),
                              nsa_g.reshape(B, S, NSA_KV, NSA_HG, 3))
        o_nsa = o_nsa * jax.nn.silu(nsa_z.astype(jnp.float32))
        y_nsa = o_nsa.astype(x.dtype) @ w_branch_nsa[layer]
        g_hg, g_nsa = jnp.split(jax.nn.sigmoid(merge_g.astype(jnp.float32)), 2, axis=-1)
        merged = (g_hg * y_hg + g_nsa * y_nsa).astype(x.dtype)
        h = h + (merged @ w_out[layer]).astype(h.dtype)
    return rmsnorm(h, final_norm_w).astype(x.dtype)
```

```python
import functools

import numpy as np
import jax
import jax.numpy as jnp
from jax import lax
from jax.experimental import pallas as pl
from jax.experimental.pallas import tpu as pltpu

F32 = jnp.float32
BF16 = jnp.bfloat16

D_MODEL = 2048
HG_HEADS = 8
HG_DK = 128
HG_DV = 128
HG_FDIM = HG_HEADS * HG_DK
HG_WIDTH = HG_HEADS * HG_DV
NSA_HEADS = 16
NSA_KV = 4
NSA_HG = NSA_HEADS // NSA_KV
HEAD_DIM = 128
NSA_WIDTH = NSA_HEADS * HEAD_DIM
NSA_KVW = NSA_KV * HEAD_DIM
CMP_LEN = 32
CMP_STRIDE = 16
CMP_HIDDEN = 512
SEL_LEN = 64
SEL_TOPK = 16
WINDOW = 512
Q_BLOCK = 128
ROPE_THETA = 500000.0
ROT_DIM = HEAD_DIM // 4
ROT_HALF = ROT_DIM // 2
EPS = 1e-6

LANES = 128
NEG_BIG = -1e30
FORCED_SCORE = 1e30
VMEM_LIMIT = 56 * 1024 * 1024

OFF_HG = 0
OFF_Q = 4 * HG_FDIM
OFF_KV = OFF_Q + NSA_WIDTH
OFF_MG = OFF_KV + 6 * NSA_KVW
OFF_Z = OFF_MG + 2 * D_MODEL
OFF_G = OFF_Z + NSA_WIDTH
PROJ_COLS = OFF_G + NSA_KV * LANES
IN_GATE_OFF = OFF_MG
IN_Z_OFF = IN_GATE_OFF + 3 * NSA_HEADS
IN_MG_OFF = IN_Z_OFF + NSA_WIDTH


def _dot(a, b):
    return jnp.dot(a, b, preferred_element_type=F32)


def _dot_nt(a, b):
    return lax.dot_general(a, b, (((1,), (1,)), ((), ())), preferred_element_type=F32)


def _split_bf16(a):
    hi = a.astype(BF16)
    lo = (a - hi.astype(F32)).astype(BF16)
    return hi, lo


def _cparams(sem):
    return pltpu.CompilerParams(dimension_semantics=sem, vmem_limit_bytes=VMEM_LIMIT)


def _norm_proj_kernel(x_ref, nw_ref, w_ref, o_ref, xn_ref):
    @pl.when(pl.program_id(1) == 0)
    def _():
        x = x_ref[...]
        ms = jnp.mean(x * x, axis=-1, keepdims=True)
        xn_ref[...] = (x * lax.rsqrt(ms + EPS) * nw_ref[...]).astype(BF16)

    o_ref[...] = _dot(xn_ref[...], w_ref[...])


def _norm_proj(x2, norm_w, w_all, tm=1024, tn=512):
    S, D = x2.shape
    N = w_all.shape[1]
    tm = min(tm, S)
    return pl.pallas_call(
        _norm_proj_kernel,
        grid=(S // tm, N // tn),
        in_specs=[
            pl.BlockSpec((tm, D), lambda i, j: (i, 0)),
            pl.BlockSpec((1, D), lambda i, j: (0, 0)),
            pl.BlockSpec((D, tn), lambda i, j: (0, j)),
        ],
        out_specs=pl.BlockSpec((tm, tn), lambda i, j: (i, j)),
        out_shape=jax.ShapeDtypeStruct((S, N), F32),
        scratch_shapes=[pltpu.VMEM((tm, D), BF16)],
        compiler_params=_cparams(("arbitrary", "arbitrary")),
        name="norm_proj",
    )(x2, norm_w.reshape(1, D), w_all)


def _rope(x, cosf, sinf):
    lane = lax.broadcasted_iota(jnp.int32, x.shape, 1)
    rot = jnp.where(lane < ROT_HALF, pltpu.roll(x, LANES - ROT_HALF, 1), pltpu.roll(x, ROT_HALF, 1))
    return x * cosf + rot * sinf


def _rope_tables(S):
    pos = jnp.arange(S, dtype=F32)
    inv = ROPE_THETA ** (-jnp.arange(0, ROT_DIM, 2, dtype=F32) / ROT_DIM)
    ang = pos[:, None] * inv[None, :]
    cos, sin = jnp.cos(ang), jnp.sin(ang)
    rest = LANES - ROT_DIM
    cosf = jnp.concatenate([cos, cos, jnp.ones((S, rest), F32)], axis=1)
    sinf = jnp.concatenate([-sin, sin, jnp.zeros((S, rest), F32)], axis=1)
    return cosf, sinf


def _kv_prep_kernel(kv_ref, cos_ref, sin_ref, cmp_ref, kvb_ref):
    cosf = cos_ref[...]
    sinf = sin_ref[...]
    W = NSA_KVW
    for g in range(NSA_KV):
        kc = kv_ref[:, 0 * W + g * LANES:0 * W + (g + 1) * LANES]
        cmp_ref[0, g] = _rope(kc, cosf, sinf)
        cmp_ref[1, g] = kv_ref[:, 1 * W + g * LANES:1 * W + (g + 1) * LANES]
        ks = kv_ref[:, 2 * W + g * LANES:2 * W + (g + 1) * LANES]
        kvb_ref[:, 0 * W + g * LANES:0 * W + (g + 1) * LANES] = _rope(ks, cosf, sinf).astype(BF16)
        kvb_ref[:, 1 * W + g * LANES:1 * W + (g + 1) * LANES] = kv_ref[:, 3 * W + g * LANES:3 * W + (g + 1) * LANES].astype(BF16)
        kw = kv_ref[:, 4 * W + g * LANES:4 * W + (g + 1) * LANES]
        kvb_ref[:, 2 * W + g * LANES:2 * W + (g + 1) * LANES] = _rope(kw, cosf, sinf).astype(BF16)
        kvb_ref[:, 3 * W + g * LANES:3 * W + (g + 1) * LANES] = kv_ref[:, 5 * W + g * LANES:5 * W + (g + 1) * LANES].astype(BF16)


def _kv_prep(proj, cosf, sinf, tm=512):
    S = proj.shape[0]
    tm = min(tm, S)
    kvw = 6 * NSA_KVW
    return pl.pallas_call(
        _kv_prep_kernel,
        grid=(S // tm,),
        in_specs=[
            pl.BlockSpec((tm, kvw), lambda i: (i, OFF_KV // kvw)),
            pl.BlockSpec((tm, LANES), lambda i: (i, 0)),
            pl.BlockSpec((tm, LANES), lambda i: (i, 0)),
        ],
        out_specs=[
            pl.BlockSpec((2, NSA_KV, tm, LANES), lambda i: (0, 0, i, 0)),
            pl.BlockSpec((tm, 4 * NSA_KVW), lambda i: (i, 0)),
        ],
        out_shape=[
            jax.ShapeDtypeStruct((2, NSA_KV, S, LANES), F32),
            jax.ShapeDtypeStruct((S, 4 * NSA_KVW), BF16),
        ],
        compiler_params=_cparams(("arbitrary",)),
        name="kv_prep",
    )(proj, cosf, sinf)


def _compress_kernel(seg_ref, pe_ref, w1_ref, b1_ref, w2_ref, o_ref):
    half = (CMP_LEN // 2) * HEAD_DIM
    seg = seg_ref[0, 0]
    n_seg = seg.shape[0]
    pe = pe_ref[0]
    a = (seg + pe[:, :half]).astype(BF16)
    b = (seg + pe[:, half:]).astype(BF16)
    u = _dot(a, w1_ref[0, :half, :])
    v = _dot(b, w1_ref[0, half:, :])
    v_next = pltpu.roll(v, n_seg - 1, 0)
    pre = u + v_next + b1_ref[0]
    h = 0.5 * pre * (1.0 + jnp.tanh(np.sqrt(2.0 / np.pi).astype(np.float32) * (pre + 0.044715 * (pre * pre * pre))))
    o_ref[0, 0] = _dot(h.astype(BF16), w2_ref[0])


def _compress(segs, pe, w1, b1, w2):
    _, G, n_seg, segw = segs.shape
    return pl.pallas_call(
        _compress_kernel,
        grid=(2, G),
        in_specs=[
            pl.BlockSpec((1, 1, n_seg, segw), lambda a, g: (a, g, 0, 0)),
            pl.BlockSpec((1, 1, 2 * segw), lambda a, g: (a, 0, 0)),
            pl.BlockSpec((1, 2 * segw, CMP_HIDDEN), lambda a, g: (a, 0, 0)),
            pl.BlockSpec((1, 1, CMP_HIDDEN), lambda a, g: (a, 0, 0)),
            pl.BlockSpec((1, CMP_HIDDEN, HEAD_DIM), lambda a, g: (a, 0, 0)),
        ],
        out_specs=pl.BlockSpec((1, 1, n_seg, HEAD_DIM), lambda a, g: (a, g, 0, 0)),
        out_shape=jax.ShapeDtypeStruct((2, G, n_seg, HEAD_DIM), F32),
        compiler_params=_cparams(("arbitrary", "arbitrary")),
        name="compress",
    )(segs, pe, w1, b1, w2)


def _hgrn_consts(ch):
    nl = int(np.log2(ch))
    assert 1 << nl == ch
    t = np.arange(ch)[:, None]
    r = np.arange(ch)[None, :]
    mats = [r <= t]
    masks = [np.eye(ch, dtype=bool)]
    for l in range(nl):
        half = 1 << l
        blk = t // (2 * half)
        ref = blk * 2 * half + half - 1
        up = ((t >> l) & 1) == 1
        mats.append(np.where(up, (r > ref) & (r <= t), (r > t) & (r <= ref)))
        masks.append(up & (((r >> l) & 1) == 0) & (blk == r // (2 * half)))
    mats.append(r > t)
    mc = np.concatenate(mats, axis=0).astype(np.float32)
    lm = np.stack(masks, axis=0).astype(np.float32)
    return nl, mc, lm


def _hgrn_kernel(q_ref, f_ref, i_ref, z_ref, lbl_ref, nw_ref, mc_ref, lm_ref, o_ref, st_ref, e_ref, *, ch, nl):
    @pl.when(pl.program_id(0) == 0)
    def _():
        st_ref[...] = jnp.zeros_like(st_ref)

    lg = lbl_ref[...]
    ex = jnp.exp(lg - jnp.max(lg, axis=0, keepdims=True))
    lb = ex[0:1] / jnp.sum(ex, axis=0, keepdims=True)
    f = lb + (1.0 - lb) * jax.nn.sigmoid(f_ref[...])
    g_hi, g_lo = _split_bf16(jnp.log(f))
    mc = mc_ref[...]
    e_ref[...] = _dot(mc, g_hi) + _dot(mc, g_lo)
    row = lax.broadcasted_iota(jnp.int32, (ch, 1), 0)
    nw = nw_ref[...]

    for h in range(HG_HEADS):
        sl = slice(h * HG_DK, (h + 1) * HG_DK)
        q = q_ref[:, sl]
        k = 1.0 - f[:, sl]
        v = i_ref[:, sl]
        v_b = v.astype(BF16)
        b = e_ref[0:ch, sl]
        scores = _dot_nt(q.astype(BF16), k.astype(BF16)) * lm_ref[0]
        for l in range(nl):
            el = e_ref[(l + 1) * ch:(l + 2) * ch, sl]
            up = ((row >> l) & 1) == 1
            xl = (jnp.where(up, q, k) * jnp.exp(el)).astype(BF16)
            scores = scores + _dot_nt(xl, xl) * lm_ref[l + 1]
        intra = _dot(scores.astype(BF16), v_b)
        st = st_ref[h]
        inter = _dot_nt((q * jnp.exp(b)).astype(BF16), st.astype(BF16))
        ke = (k * jnp.exp(e_ref[(nl + 1) * ch:(nl + 2) * ch, sl])).astype(BF16)
        upd = lax.dot_general(v_b, ke, (((0,), (0,)), ((), ())), preferred_element_type=F32)
        st_ref[h] = st * jnp.exp(b[ch - 1:ch, :]) + upd
        o = inter + intra
        ms = jnp.mean(o * o, axis=-1, keepdims=True)
        z = z_ref[:, sl]
        o_ref[:, sl] = (o * lax.rsqrt(ms + EPS) * nw * (z * jax.nn.sigmoid(z))).astype(o_ref.dtype)


def _hgrn(proj, lb_logits, hg_norm_w, ch=64):
    S = proj.shape[0]
    nl, mc, lm = _hgrn_consts(ch)
    kern = functools.partial(_hgrn_kernel, ch=ch, nl=nl)
    nlb = lb_logits.shape[0]
    return pl.pallas_call(
        kern,
        grid=(S // ch,),
        in_specs=[
            pl.BlockSpec((ch, HG_FDIM), lambda c: (c, 0)),
            pl.BlockSpec((ch, HG_FDIM), lambda c: (c, 1)),
            pl.BlockSpec((ch, HG_WIDTH), lambda c: (c, 2)),
            pl.BlockSpec((ch, HG_WIDTH), lambda c: (c, 3)),
            pl.BlockSpec((nlb, HG_FDIM), lambda c: (0, 0)),
            pl.BlockSpec((1, HG_DV), lambda c: (0, 0)),
            pl.BlockSpec(mc.shape, lambda c: (0, 0)),
            pl.BlockSpec(lm.shape, lambda c: (0, 0, 0)),
        ],
        out_specs=pl.BlockSpec((ch, HG_WIDTH), lambda c: (c, 0)),
        out_shape=jax.ShapeDtypeStruct((S, HG_WIDTH), BF16),
        scratch_shapes=[
            pltpu.VMEM((HG_HEADS, HG_DV, HG_DK), F32),
            pltpu.VMEM(((nl + 2) * ch, HG_FDIM), F32),
        ],
        compiler_params=_cparams(("arbitrary",)),
        name="hgrn",
    )(proj, proj, proj, proj, lb_logits, hg_norm_w.reshape(1, HG_DV), jnp.asarray(mc, BF16), jnp.asarray(lm, F32))


def _nsa_kernel(q_ref, cos_ref, sin_ref, gate_ref, z_ref, kc_ref, vc_ref, ov_ref,
                ks_ref, vs_ref, kw_ref, vw_ref, o_ref, m_ref, l_ref, acc_ref, *, tk):
    n = pl.program_id(1)
    q0 = n * Q_BLOCK
    R = NSA_HG * Q_BLOCK
    scale = HEAD_DIM ** -0.5
    cosf = cos_ref[...]
    sinf = sin_ref[...]
    q4 = jnp.concatenate(
        [(_rope(q_ref[:, h * HEAD_DIM:(h + 1) * HEAD_DIM], cosf, sinf) * scale).astype(BF16) for h in range(NSA_HG)],
        axis=0)

    def tpos(shape):
        return q0 + (lax.broadcasted_iota(jnp.int32, shape, 0) & (Q_BLOCK - 1))

    ncp = kc_ref.shape[2]
    s = _dot_nt(q4, kc_ref[0, 0].astype(BF16))
    cend = lax.broadcasted_iota(jnp.int32, (R, ncp), 1) * CMP_STRIDE + (CMP_LEN - 1)
    ok = cend <= tpos((R, ncp))
    s = jnp.where(ok, s, NEG_BIG)
    e = jnp.where(ok, jnp.exp(s - jnp.max(s, axis=-1, keepdims=True)), 0.0)
    den = jnp.sum(e, axis=-1, keepdims=True)
    p = e / jnp.where(den > 0.0, den, 1.0)
    o_cmp = _dot(p.astype(BF16), vc_ref[0, 0].astype(BF16))
    p_sum = p[0:Q_BLOCK]
    for h in range(1, NSA_HG):
        p_sum = p_sum + p[h * Q_BLOCK:(h + 1) * Q_BLOCK]
    ov = ov_ref[...]
    ps_hi, ps_lo = _split_bf16(p_sum)
    imp = _dot(ps_hi, ov) + _dot(ps_lo, ov)

    imp_t = imp.T
    nsel_pad = imp_t.shape[0]
    jj = lax.broadcasted_iota(jnp.int32, (nsel_pad, Q_BLOCK), 0)
    tq = q0 + lax.broadcasted_iota(jnp.int32, (nsel_pad, Q_BLOCK), 1)
    jt = tq >> 6
    forced = (jj == 0) | (jj == jt) | (jj == jt - 1)
    jjf = jj.astype(F32)
    score = jnp.where(jj * SEL_LEN <= tq, jnp.where(forced, FORCED_SCORE, imp_t), -1.0)
    sel = jnp.zeros((nsel_pad, Q_BLOCK), F32)
    for _ in range(SEL_TOPK):
        mx = jnp.max(score, axis=0, keepdims=True)
        first = jnp.min(jnp.where(score == mx, jjf, float(nsel_pad)), axis=0, keepdims=True)
        hit = jjf == first
        sel = jnp.where(hit, jnp.where(mx >= 0.0, 1.0, 0.0), sel)
        score = jnp.where(hit, -2.0, score)
    sel_q = sel.T.astype(BF16)

    m_ref[...] = jnp.full(m_ref.shape, NEG_BIG, F32)
    l_ref[...] = jnp.zeros(l_ref.shape, F32)
    acc_ref[...] = jnp.zeros(acc_ref.shape, F32)
    tq_k = q0 + lax.broadcasted_iota(jnp.int32, (Q_BLOCK, tk), 0)
    blk_row = lax.broadcasted_iota(jnp.int32, (nsel_pad, tk), 0)
    blk_col = lax.broadcasted_iota(jnp.int32, (nsel_pad, tk), 1) >> 6
    key_col = lax.broadcasted_iota(jnp.int32, (Q_BLOCK, tk), 1)
    rep = tk // LANES

    def slc_body(kt, carry):
        k0 = pl.multiple_of(kt * tk, tk)
        kb = ks_ref[pl.ds(k0, tk), :]
        vb = vs_ref[pl.ds(k0, tk), :]
        sc = _dot_nt(q4, kb)
        expand = jnp.where(blk_row == blk_col + kt * (tk // SEL_LEN), 1.0, 0.0).astype(BF16)
        picked = _dot(sel_q, expand)
        okq = jnp.where(key_col + k0 <= tq_k, picked, 0.0) > 0.5
        okr = jnp.concatenate([okq] * NSA_HG, axis=0)
        sc = jnp.where(okr, sc, NEG_BIG)
        m_prev = m_ref[...]
        m_next = jnp.maximum(m_prev, jnp.max(sc, axis=-1, keepdims=True))
        pr = jnp.where(okr, jnp.exp(sc - jnp.concatenate([m_next] * rep, axis=1)), 0.0)
        alpha = jnp.exp(m_prev - m_next)
        l_ref[...] = alpha * l_ref[...] + jnp.sum(pr, axis=-1, keepdims=True)
        acc_ref[...] = alpha * acc_ref[...] + _dot(pr.astype(BF16), vb)
        m_ref[...] = m_next
        return carry

    n_tiles = (q0 + Q_BLOCK + tk - 1) // tk
    lax.fori_loop(0, n_tiles, slc_body, 0)
    l_fin = l_ref[...]
    o_slc = acc_ref[...] / jnp.where(l_fin > 0.0, l_fin, 1.0)

    wspan = WINDOW + Q_BLOCK
    w0 = pl.multiple_of(jnp.maximum(q0 - WINDOW, 0), Q_BLOCK)
    sw = _dot_nt(q4, kw_ref[pl.ds(w0, wspan), :])
    kpos = w0 + lax.broadcasted_iota(jnp.int32, (R, wspan), 1)
    tw = tpos((R, wspan))
    okw = jnp.where(kpos <= tw, jnp.where(kpos > tw - WINDOW, 1.0, 0.0), 0.0) > 0.5
    sw = jnp.where(okw, sw, NEG_BIG)
    ew = jnp.where(okw, jnp.exp(sw - jnp.max(sw, axis=-1, keepdims=True)), 0.0)
    dw = jnp.sum(ew, axis=-1, keepdims=True)
    pw = ew / jnp.where(dw > 0.0, dw, 1.0)
    o_win = _dot(pw.astype(BF16), vw_ref[pl.ds(w0, wspan), :])

    gsig = jax.nn.sigmoid(gate_ref[...])
    for h in range(NSA_HG):
        rs = slice(h * Q_BLOCK, (h + 1) * Q_BLOCK)
        cs = slice(h * HEAD_DIM, (h + 1) * HEAD_DIM)
        oh = (gsig[:, 3 * h:3 * h + 1] * o_cmp[rs]
              + gsig[:, 3 * h + 1:3 * h + 2] * o_slc[rs]
              + gsig[:, 3 * h + 2:3 * h + 3] * o_win[rs])
        z = z_ref[:, cs]
        o_ref[:, cs] = (oh * (z * jax.nn.sigmoid(z))).astype(o_ref.dtype)


def _overlap_matrix(ncp, nsel_pad, nsel):
    ci = np.arange(ncp)[:, None] * CMP_STRIDE
    sj = np.arange(nsel_pad)[None, :] * SEL_LEN
    ov = (ci < sj + SEL_LEN) & (ci + CMP_LEN > sj) & (np.arange(nsel_pad)[None, :] < nsel) & (np.arange(ncp)[:, None] < ncp - 1)
    return ov.astype(np.float32)


def _nsa(proj, cosf, sinf, kcvc, kvb, tk=512):
    S = proj.shape[0]
    ncp = kcvc.shape[2]
    nsel = S // SEL_LEN
    assert nsel <= LANES and S % tk == 0 and S >= WINDOW + Q_BLOCK
    ov = jnp.asarray(_overlap_matrix(ncp, LANES, nsel), BF16)
    qw = NSA_HG * HEAD_DIM
    R = NSA_HG * Q_BLOCK
    kern = functools.partial(_nsa_kernel, tk=tk)
    return pl.pallas_call(
        kern,
        grid=(NSA_KV, S // Q_BLOCK),
        in_specs=[
            pl.BlockSpec((Q_BLOCK, qw), lambda g, n: (n, OFF_Q // qw + g)),
            pl.BlockSpec((Q_BLOCK, LANES), lambda g, n: (n, 0)),
            pl.BlockSpec((Q_BLOCK, LANES), lambda g, n: (n, 0)),
            pl.BlockSpec((Q_BLOCK, LANES), lambda g, n: (n, OFF_G // LANES + g)),
            pl.BlockSpec((Q_BLOCK, qw), lambda g, n: (n, OFF_Z // qw + g)),
            pl.BlockSpec((1, 1, ncp, HEAD_DIM), lambda g, n: (0, g, 0, 0)),
            pl.BlockSpec((1, 1, ncp, HEAD_DIM), lambda g, n: (1, g, 0, 0)),
            pl.BlockSpec((ncp, LANES), lambda g, n: (0, 0)),
            pl.BlockSpec((S, HEAD_DIM), lambda g, n: (0, 0 * NSA_KV + g)),
            pl.BlockSpec((S, HEAD_DIM), lambda g, n: (0, 1 * NSA_KV + g)),
            pl.BlockSpec((S, HEAD_DIM), lambda g, n: (0, 2 * NSA_KV + g)),
            pl.BlockSpec((S, HEAD_DIM), lambda g, n: (0, 3 * NSA_KV + g)),
        ],
        out_specs=pl.BlockSpec((Q_BLOCK, qw), lambda g, n: (n, g)),
        out_shape=jax.ShapeDtypeStruct((S, NSA_WIDTH), BF16),
        scratch_shapes=[
            pltpu.VMEM((R, LANES), F32),
            pltpu.VMEM((R, LANES), F32),
            pltpu.VMEM((R, HEAD_DIM), F32),
        ],
        compiler_params=_cparams(("arbitrary", "arbitrary")),
        name="nsa",
    )(proj, cosf, sinf, proj, proj, kcvc, kcvc, ov, kvb, kvb, kvb, kvb)


def _merge_kernel(ohg_ref, onsa_ref, g0_ref, g1_ref, g2_ref, g3_ref, whg_ref, wnsa_ref, o_ref):
    y_hg = _dot(ohg_ref[...], whg_ref[...])
    y_nsa = _dot(onsa_ref[...], wnsa_ref[...])
    half = D_MODEL // 2
    gh = (g0_ref, g1_ref)
    gn = (g2_ref, g3_ref)
    for c in range(2):
        cs = slice(c * half, (c + 1) * half)
        m = jax.nn.sigmoid(gh[c][...]) * y_hg[:, cs] + jax.nn.sigmoid(gn[c][...]) * y_nsa[:, cs]
        o_ref[:, cs] = m.astype(o_ref.dtype)


def _merge(o_hg, o_nsa, proj, w_hg, w_nsa, tm=256):
    S = o_hg.shape[0]
    tm = min(tm, S)
    half = D_MODEL // 2
    gb = OFF_MG // half
    return pl.pallas_call(
        _merge_kernel,
        grid=(S // tm,),
        in_specs=[
            pl.BlockSpec((tm, HG_WIDTH), lambda i: (i, 0)),
            pl.BlockSpec((tm, NSA_WIDTH), lambda i: (i, 0)),
            pl.BlockSpec((tm, half), lambda i: (i, gb + 0)),
            pl.BlockSpec((tm, half), lambda i: (i, gb + 1)),
            pl.BlockSpec((tm, half), lambda i: (i, gb + 2)),
            pl.BlockSpec((tm, half), lambda i: (i, gb + 3)),
            pl.BlockSpec((HG_WIDTH, D_MODEL), lambda i: (0, 0)),
            pl.BlockSpec((NSA_WIDTH, D_MODEL), lambda i: (0, 0)),
        ],
        out_specs=pl.BlockSpec((tm, D_MODEL), lambda i: (i, 0)),
        out_shape=jax.ShapeDtypeStruct((S, D_MODEL), BF16),
        compiler_params=_cparams(("arbitrary",)),
        name="merge",
    )(o_hg, o_nsa, proj, proj, proj, proj, w_hg, w_nsa)


def _out_kernel(x_ref, m_ref, w_ref, nw_ref, o_ref):
    h = x_ref[...] + _dot(m_ref[...], w_ref[...])
    ms = jnp.mean(h * h, axis=-1, keepdims=True)
    o_ref[...] = h * lax.rsqrt(ms + EPS) * nw_ref[...]


def _out(x2, merged, w_out, final_w, tm=256):
    S, D = x2.shape
    tm = min(tm, S)
    return pl.pallas_call(
        _out_kernel,
        grid=(S // tm,),
        in_specs=[
            pl.BlockSpec((tm, D), lambda i: (i, 0)),
            pl.BlockSpec((tm, D), lambda i: (i, 0)),
            pl.BlockSpec((D, D), lambda i: (0, 0)),
            pl.BlockSpec((1, D), lambda i: (0, 0)),
        ],
        out_specs=pl.BlockSpec((tm, D), lambda i: (i, 0)),
        out_shape=jax.ShapeDtypeStruct((S, D), F32),
        compiler_params=_cparams(("arbitrary",)),
        name="out_proj",
    )(x2, merged, w_out, final_w.reshape(1, D))


def _layout_w_in(w):
    gates = w[:, IN_GATE_OFF:IN_Z_OFF].reshape(D_MODEL, NSA_KV, 3 * NSA_HG)
    gates = jnp.pad(gates, ((0, 0), (0, 0), (0, LANES - 3 * NSA_HG))).reshape(D_MODEL, NSA_KV * LANES)
    return jnp.concatenate(
        [w[:, :IN_GATE_OFF], w[:, IN_MG_OFF:IN_MG_OFF + 2 * D_MODEL], w[:, IN_Z_OFF:IN_MG_OFF], gates],
        axis=1).astype(BF16)


def _layer(x2, norm_w, w_in, lb_logits, hg_norm_w, cmp_k_pos, cmp_k_w1, cmp_k_b1, cmp_k_w2,
           cmp_v_pos, cmp_v_w1, cmp_v_b1, cmp_v_w2, w_branch_hg, w_branch_nsa, w_out, final_w):
    S = x2.shape[0]
    proj = _norm_proj(x2, norm_w, _layout_w_in(w_in))
    cosf, sinf = _rope_tables(S)
    cmp_in, kvb = _kv_prep(proj, cosf, sinf)
    segs = cmp_in.reshape(2, NSA_KV, S // CMP_STRIDE, CMP_STRIDE * HEAD_DIM)
    pe = jnp.stack([cmp_k_pos.reshape(1, -1), cmp_v_pos.reshape(1, -1)])
    w1 = jnp.stack([cmp_k_w1, cmp_v_w1]).astype(BF16)
    b1 = jnp.stack([cmp_k_b1.reshape(1, -1), cmp_v_b1.reshape(1, -1)])
    w2 = jnp.stack([cmp_k_w2, cmp_v_w2]).astype(BF16)
    kcvc = _compress(segs, pe, w1, b1, w2)
    o_hg = _hgrn(proj, lb_logits, hg_norm_w)
    o_nsa = _nsa(proj, cosf, sinf, kcvc, kvb)
    merged = _merge(o_hg, o_nsa, proj, w_branch_hg.astype(BF16), w_branch_nsa.astype(BF16))
    return _out(x2, merged, w_out.astype(BF16), final_w)


def kernel(x, norm_w, w_in, hg_lb_logits, hg_norm_w, cmp_k_pos, cmp_k_w1, cmp_k_b1, cmp_k_w2, cmp_v_pos, cmp_v_w1, cmp_v_b1, cmp_v_w2, w_branch_hg, w_branch_nsa, w_out, final_norm_w):
    B, S, D = x.shape
    assert B == 1 and D == D_MODEL and norm_w.shape[0] == 1
    y = _layer(x[0], norm_w[0], w_in[0], hg_lb_logits, hg_norm_w[0], cmp_k_pos[0], cmp_k_w1[0], cmp_k_b1[0],
               cmp_k_w2[0], cmp_v_pos[0], cmp_v_w1[0], cmp_v_b1[0], cmp_v_w2[0], w_branch_hg[0],
               w_branch_nsa[0], w_out[0], final_norm_w)
    return y[None]
```

```python
import functools

import numpy as np
import jax
import jax.numpy as jnp
from jax import lax
from jax.experimental import pallas as pl
from jax.experimental.pallas import tpu as pltpu

F32 = jnp.float32
BF16 = jnp.bfloat16

D_MODEL = 2048
HG_HEADS = 8
HG_DK = 128
HG_DV = 128
HG_FDIM = HG_HEADS * HG_DK
HG_WIDTH = HG_HEADS * HG_DV
NSA_HEADS = 16
NSA_KV = 4
NSA_HG = NSA_HEADS // NSA_KV
HEAD_DIM = 128
NSA_WIDTH = NSA_HEADS * HEAD_DIM
NSA_KVW = NSA_KV * HEAD_DIM
CMP_LEN = 32
CMP_STRIDE = 16
CMP_HIDDEN = 512
SEL_LEN = 64
SEL_TOPK = 16
WINDOW = 512
Q_BLOCK = 128
ROPE_THETA = 500000.0
ROT_DIM = HEAD_DIM // 4
ROT_HALF = ROT_DIM // 2
EPS = 1e-6
LOG2E = float(np.log2(np.e))

LANES = 128
NEG_BIG = -1e30
FORCED_SCORE = 1e30
VMEM_LIMIT = 56 * 1024 * 1024

OFF_HG = 0
OFF_Q = 4 * HG_FDIM
OFF_KV = OFF_Q + NSA_WIDTH
OFF_MG = OFF_KV + 6 * NSA_KVW
OFF_Z = OFF_MG + 2 * D_MODEL
OFF_G = OFF_Z + NSA_WIDTH
PROJ_COLS = OFF_G + NSA_KV * LANES
IN_GATE_OFF = OFF_MG
IN_Z_OFF = IN_GATE_OFF + 3 * NSA_HEADS
IN_MG_OFF = IN_Z_OFF + NSA_WIDTH


def _dot(a, b):
    return jnp.dot(a, b, preferred_element_type=F32)


def _dot_nt(a, b):
    return lax.dot_general(a, b, (((1,), (1,)), ((), ())), preferred_element_type=F32)


def _split_bf16(a):
    hi = a.astype(BF16)
    lo = (a - hi.astype(F32)).astype(BF16)
    return hi, lo


def _cparams(sem):
    return pltpu.CompilerParams(dimension_semantics=sem, vmem_limit_bytes=VMEM_LIMIT)


def _norm_proj_kernel(x_ref, nw_ref, w_ref, o_ref, xn_ref):
    @pl.when(pl.program_id(1) == 0)
    def _():
        x = x_ref[...]
        ms = jnp.mean(x * x, axis=-1, keepdims=True)
        xn_ref[...] = (x * lax.rsqrt(ms + EPS) * nw_ref[...]).astype(BF16)

    o_ref[...] = _dot(xn_ref[...], w_ref[...])


def _norm_proj(x2, norm_w, w_all, tm=1024, tn=512):
    S, D = x2.shape
    N = w_all.shape[1]
    tm = min(tm, S)
    return pl.pallas_call(
        _norm_proj_kernel,
        grid=(S // tm, N // tn),
        in_specs=[
            pl.BlockSpec((tm, D), lambda i, j: (i, 0)),
            pl.BlockSpec((1, D), lambda i, j: (0, 0)),
            pl.BlockSpec((D, tn), lambda i, j: (0, j)),
        ],
        out_specs=pl.BlockSpec((tm, tn), lambda i, j: (i, j)),
        out_shape=jax.ShapeDtypeStruct((S, N), F32),
        scratch_shapes=[pltpu.VMEM((tm, D), BF16)],
        compiler_params=_cparams(("arbitrary", "arbitrary")),
        name="norm_proj",
    )(x2, norm_w.reshape(1, D), w_all)


def _rope(x, cosf, sinf):
    lane = lax.broadcasted_iota(jnp.int32, x.shape, 1)
    rot = jnp.where(lane < ROT_HALF, pltpu.roll(x, LANES - ROT_HALF, 1), pltpu.roll(x, ROT_HALF, 1))
    return x * cosf + rot * sinf


def _rope_tables(S):
    pos = jnp.arange(S, dtype=F32)
    inv = ROPE_THETA ** (-jnp.arange(0, ROT_DIM, 2, dtype=F32) / ROT_DIM)
    ang = pos[:, None] * inv[None, :]
    cos, sin = jnp.cos(ang), jnp.sin(ang)
    rest = LANES - ROT_DIM
    cosf = jnp.concatenate([cos, cos, jnp.ones((S, rest), F32)], axis=1)
    sinf = jnp.concatenate([-sin, sin, jnp.zeros((S, rest), F32)], axis=1)
    return cosf, sinf


def _kv_prep_kernel(kv_ref, cos_ref, sin_ref, cmp_ref, kvb_ref):
    cosf = cos_ref[...]
    sinf = sin_ref[...]
    W = NSA_KVW
    for g in range(NSA_KV):
        kc = kv_ref[:, 0 * W + g * LANES:0 * W + (g + 1) * LANES]
        cmp_ref[0, g] = _rope(kc, cosf, sinf)
        cmp_ref[1, g] = kv_ref[:, 1 * W + g * LANES:1 * W + (g + 1) * LANES]
        ks = kv_ref[:, 2 * W + g * LANES:2 * W + (g + 1) * LANES]
        kvb_ref[:, 0 * W + g * LANES:0 * W + (g + 1) * LANES] = _rope(ks, cosf, sinf).astype(BF16)
        kvb_ref[:, 1 * W + g * LANES:1 * W + (g + 1) * LANES] = kv_ref[:, 3 * W + g * LANES:3 * W + (g + 1) * LANES].astype(BF16)
        kw = kv_ref[:, 4 * W + g * LANES:4 * W + (g + 1) * LANES]
        kvb_ref[:, 2 * W + g * LANES:2 * W + (g + 1) * LANES] = _rope(kw, cosf, sinf).astype(BF16)
        kvb_ref[:, 3 * W + g * LANES:3 * W + (g + 1) * LANES] = kv_ref[:, 5 * W + g * LANES:5 * W + (g + 1) * LANES].astype(BF16)


def _kv_prep(proj, cosf, sinf, tm=512):
    S = proj.shape[0]
    tm = min(tm, S)
    kvw = 6 * NSA_KVW
    return pl.pallas_call(
        _kv_prep_kernel,
        grid=(S // tm,),
        in_specs=[
            pl.BlockSpec((tm, kvw), lambda i: (i, OFF_KV // kvw)),
            pl.BlockSpec((tm, LANES), lambda i: (i, 0)),
            pl.BlockSpec((tm, LANES), lambda i: (i, 0)),
        ],
        out_specs=[
            pl.BlockSpec((2, NSA_KV, tm, LANES), lambda i: (0, 0, i, 0)),
            pl.BlockSpec((tm, 4 * NSA_KVW), lambda i: (i, 0)),
        ],
        out_shape=[
            jax.ShapeDtypeStruct((2, NSA_KV, S, LANES), F32),
            jax.ShapeDtypeStruct((S, 4 * NSA_KVW), BF16),
        ],
        compiler_params=_cparams(("arbitrary",)),
        name="kv_prep",
    )(proj, cosf, sinf)


def _compress_kernel(seg_ref, pe_ref, w1_ref, b1_ref, w2_ref, o_ref):
    half = (CMP_LEN // 2) * HEAD_DIM
    seg = seg_ref[0, 0]
    n_seg = seg.shape[0]
    pe = pe_ref[0]
    a = (seg + pe[:, :half]).astype(BF16)
    b = (seg + pe[:, half:]).astype(BF16)
    u = _dot(a, w1_ref[0, :half, :])
    v = _dot(b, w1_ref[0, half:, :])
    v_next = pltpu.roll(v, n_seg - 1, 0)
    pre = u + v_next + b1_ref[0]
    h = 0.5 * pre * (1.0 + jnp.tanh(np.sqrt(2.0 / np.pi).astype(np.float32) * (pre + 0.044715 * (pre * pre * pre))))
    o_ref[0, 0] = _dot(h.astype(BF16), w2_ref[0])


def _compress(segs, pe, w1, b1, w2):
    _, G, n_seg, segw = segs.shape
    return pl.pallas_call(
        _compress_kernel,
        grid=(2, G),
        in_specs=[
            pl.BlockSpec((1, 1, n_seg, segw), lambda a, g: (a, g, 0, 0)),
            pl.BlockSpec((1, 1, 2 * segw), lambda a, g: (a, 0, 0)),
            pl.BlockSpec((1, 2 * segw, CMP_HIDDEN), lambda a, g: (a, 0, 0)),
            pl.BlockSpec((1, 1, CMP_HIDDEN), lambda a, g: (a, 0, 0)),
            pl.BlockSpec((1, CMP_HIDDEN, HEAD_DIM), lambda a, g: (a, 0, 0)),
        ],
        out_specs=pl.BlockSpec((1, 1, n_seg, HEAD_DIM), lambda a, g: (a, g, 0, 0)),
        out_shape=jax.ShapeDtypeStruct((2, G, n_seg, HEAD_DIM), F32),
        compiler_params=_cparams(("arbitrary", "arbitrary")),
        name="compress",
    )(segs, pe, w1, b1, w2)


def _hgrn_consts(ch):
    nl = int(np.log2(ch))
    assert 1 << nl == ch
    t = np.arange(ch)[:, None]
    r = np.arange(ch)[None, :]
    mats = [r <= t]
    masks = [np.eye(ch, dtype=bool)]
    for l in range(nl):
        half = 1 << l
        blk = t // (2 * half)
        ref = blk * 2 * half + half - 1
        up = ((t >> l) & 1) == 1
        mats.append(np.where(up, (r > ref) & (r <= t), (r > t) & (r <= ref)))
        masks.append(up & (((r >> l) & 1) == 0) & (blk == r // (2 * half)))
    mats.append(r > t)
    mc = np.concatenate(mats, axis=0).astype(np.float32)
    lm = np.stack(masks, axis=0).astype(np.float32)
    return nl, mc, lm


def _hgrn_kernel(q_ref, f_ref, i_ref, z_ref, lbl_ref, nw_ref, mc_ref, lm_ref, o_ref, st_ref, e_ref, *, ch, nl):
    @pl.when(pl.program_id(0) == 0)
    def _():
        st_ref[...] = jnp.zeros_like(st_ref)

    lg = lbl_ref[...]
    ex = jnp.exp(lg - jnp.max(lg, axis=0, keepdims=True))
    lb = ex[0:1] / jnp.sum(ex, axis=0, keepdims=True)
    f = lb + (1.0 - lb) * jax.nn.sigmoid(f_ref[...])
    g_hi, g_lo = _split_bf16(jnp.log(f))
    mc = mc_ref[...]
    e_ref[...] = _dot(mc, g_hi) + _dot(mc, g_lo)
    row = lax.broadcasted_iota(jnp.int32, (ch, 1), 0)
    nw = nw_ref[...]

    for h in range(HG_HEADS):
        sl = slice(h * HG_DK, (h + 1) * HG_DK)
        q = q_ref[:, sl]
        k = 1.0 - f[:, sl]
        v = i_ref[:, sl]
        v_b = v.astype(BF16)
        b = e_ref[0:ch, sl]
        scores = _dot_nt(q.astype(BF16), k.astype(BF16)) * lm_ref[0]
        for l in range(nl):
            el = e_ref[(l + 1) * ch:(l + 2) * ch, sl]
            up = ((row >> l) & 1) == 1
            xl = (jnp.where(up, q, k) * jnp.exp(el)).astype(BF16)
            scores = scores + _dot_nt(xl, xl) * lm_ref[l + 1]
        intra = _dot(scores.astype(BF16), v_b)
        st = st_ref[h]
        inter = _dot_nt((q * jnp.exp(b)).astype(BF16), st.astype(BF16))
        ke = (k * jnp.exp(e_ref[(nl + 1) * ch:(nl + 2) * ch, sl])).astype(BF16)
        upd = lax.dot_general(v_b, ke, (((0,), (0,)), ((), ())), preferred_element_type=F32)
        st_ref[h] = st * jnp.exp(b[ch - 1:ch, :]) + upd
        o = inter + intra
        ms = jnp.mean(o * o, axis=-1, keepdims=True)
        z = z_ref[:, sl]
        o_ref[:, sl] = (o * lax.rsqrt(ms + EPS) * nw * (z * jax.nn.sigmoid(z))).astype(o_ref.dtype)


def _hgrn(proj, lb_logits, hg_norm_w, ch=64):
    S = proj.shape[0]
    nl, mc, lm = _hgrn_consts(ch)
    kern = functools.partial(_hgrn_kernel, ch=ch, nl=nl)
    nlb = lb_logits.shape[0]
    return pl.pallas_call(
        kern,
        grid=(S // ch,),
        in_specs=[
            pl.BlockSpec((ch, HG_FDIM), lambda c: (c, 0)),
            pl.BlockSpec((ch, HG_FDIM), lambda c: (c, 1)),
            pl.BlockSpec((ch, HG_WIDTH), lambda c: (c, 2)),
            pl.BlockSpec((ch, HG_WIDTH), lambda c: (c, 3)),
            pl.BlockSpec((nlb, HG_FDIM), lambda c: (0, 0)),
            pl.BlockSpec((1, HG_DV), lambda c: (0, 0)),
            pl.BlockSpec(mc.shape, lambda c: (0, 0)),
            pl.BlockSpec(lm.shape, lambda c: (0, 0, 0)),
        ],
        out_specs=pl.BlockSpec((ch, HG_WIDTH), lambda c: (c, 0)),
        out_shape=jax.ShapeDtypeStruct((S, HG_WIDTH), BF16),
        scratch_shapes=[
            pltpu.VMEM((HG_HEADS, HG_DV, HG_DK), F32),
            pltpu.VMEM(((nl + 2) * ch, HG_FDIM), F32),
        ],
        compiler_params=_cparams(("arbitrary",)),
        name="hgrn",
    )(proj, proj, proj, proj, lb_logits, hg_norm_w.reshape(1, HG_DV), jnp.asarray(mc, BF16), jnp.asarray(lm, F32))


def _nsa_kernel(q_ref, cos_ref, sin_ref, gate_ref, z_ref, kc_ref, vc_ref, ov_ref, ca_ref, wb_ref, cb_ref,
                ks_ref, vs_ref, kw_ref, vw_ref, o_ref, m_ref, accl_ref, sa_ref, sb_ref, *, tk):
    n = pl.program_id(1)
    q0 = n * Q_BLOCK
    R = NSA_HG * Q_BLOCK
    qscale = (HEAD_DIM ** -0.5) * LOG2E
    cosf = cos_ref[...]
    sinf = sin_ref[...]
    q4 = jnp.concatenate(
        [(_rope(q_ref[:, h * HEAD_DIM:(h + 1) * HEAD_DIM], cosf, sinf) * qscale).astype(BF16) for h in range(NSA_HG)],
        axis=0)

    def per_head(a):
        return jnp.concatenate([a] * NSA_HG, axis=0)

    def tpos(shape):
        return q0 + (lax.broadcasted_iota(jnp.int32, shape, 0) & (Q_BLOCK - 1))

    ncp = kc_ref.shape[2]
    nb = Q_BLOCK // CMP_STRIDE
    c_first = n * nb - (CMP_LEN // CMP_STRIDE - 1)
    crow = lax.broadcasted_iota(jnp.int32, (ncp, LANES), 0)
    clane = lax.broadcasted_iota(jnp.int32, (ncp, LANES), 1)
    flags = jnp.where(clane == 0, jnp.where(crow >= c_first + nb, NEG_BIG, 0.0),
                      jnp.where(crow == c_first + clane - 1, 1.0, 0.0)).astype(BF16)
    kca = jnp.concatenate([kc_ref[0, 0].astype(BF16), flags], axis=1)
    s = _dot_nt(jnp.concatenate([q4, per_head(ca_ref[...])], axis=1), kca)
    e = jnp.exp2(s - jnp.max(s, axis=-1, keepdims=True))
    vca = jnp.concatenate([vc_ref[0, 0].astype(BF16), jnp.ones((ncp, LANES), BF16)], axis=1)
    pv = _dot(e.astype(BF16), vca)
    any_visible = jnp.where(tpos((R, LANES)) >= CMP_LEN - 1, 1.0, 0.0)
    inv = any_visible / pv[:, LANES:]
    o_cmp = pv[:, :LANES] * inv
    p = e * jnp.concatenate([inv] * (ncp // LANES), axis=1)
    p_sum = p[0:Q_BLOCK]
    for h in range(1, NSA_HG):
        p_sum = p_sum + p[h * Q_BLOCK:(h + 1) * Q_BLOCK]
    ov = ov_ref[...]
    ps_hi, ps_lo = _split_bf16(p_sum)
    imp = _dot(ps_hi, ov) + _dot(ps_lo, ov)

    imp_t = imp.T
    nsel_pad = imp_t.shape[0]
    jj = lax.broadcasted_iota(jnp.int32, (nsel_pad, Q_BLOCK), 0)
    tq = q0 + lax.broadcasted_iota(jnp.int32, (nsel_pad, Q_BLOCK), 1)
    jt = tq >> 6
    forced = (jj == 0) | (jj == jt) | (jj == jt - 1)
    jjf = jj.astype(F32)
    score = jnp.where(jj * SEL_LEN <= tq, jnp.where(forced, FORCED_SCORE, imp_t), -1.0)
    sel = jnp.zeros((nsel_pad, Q_BLOCK), F32)
    for _ in range(SEL_TOPK):
        mx = jnp.max(score, axis=0, keepdims=True)
        first = jnp.min(jnp.where(score == mx, jjf, float(nsel_pad)), axis=0, keepdims=True)
        hit = jjf == first
        sel = jnp.where(hit, jnp.where(mx >= 0.0, 1.0, 0.0), sel)
        score = jnp.where(hit, -2.0, score)
    sel_bias = per_head(((sel.T - 1.0) * (-NEG_BIG)).astype(BF16))

    m_ref[...] = jnp.full(m_ref.shape, NEG_BIG, F32)
    accl_ref[...] = jnp.zeros(accl_ref.shape, F32)
    qa = jnp.concatenate([q4, sel_bias], axis=1)
    key_blk = lax.broadcasted_iota(jnp.int32, (tk, LANES), 0) >> 6
    blk_lane = lax.broadcasted_iota(jnp.int32, (tk, LANES), 1)
    ones_k = jnp.ones((tk, LANES), BF16)
    rep = tk // LANES

    n_q = tk // Q_BLOCK
    last = lax.shift_right_logical(n, n_q.bit_length() - 1)
    diag = n - last * n_q

    def score_tile(kt, dst_ref):
        k0 = pl.multiple_of(kt * tk, tk)
        onehot = jnp.where(key_blk + kt * (tk // SEL_LEN) == blk_lane, 1.0, 0.0).astype(BF16)
        dst_ref[...] = _dot_nt(qa, jnp.concatenate([ks_ref[pl.ds(k0, tk), :], onehot], axis=1))

    def consume_tile(kt, src_ref):
        k0 = pl.multiple_of(kt * tk, tk)
        sc = src_ref[...] + per_head(cb_ref[jnp.where(kt == last, diag, n_q)])
        m_prev = m_ref[...]
        m_next = jnp.maximum(m_prev, jnp.max(sc, axis=-1, keepdims=True))
        pr = jnp.exp2(sc - jnp.concatenate([m_next] * rep, axis=1))
        alpha = jnp.exp2(m_prev - m_next)
        va = jnp.concatenate([vs_ref[pl.ds(k0, tk), :], ones_k], axis=1)
        accl_ref[...] = jnp.concatenate([alpha, alpha], axis=1) * accl_ref[...] + _dot(pr.astype(BF16), va)
        m_ref[...] = m_next

    score_tile(0, sa_ref)

    def slc_pair(i, carry):
        score_tile(2 * i + 1, sb_ref)
        consume_tile(2 * i, sa_ref)
        score_tile(2 * i + 2, sa_ref)
        consume_tile(2 * i + 1, sb_ref)
        return carry

    lax.fori_loop(0, lax.shift_right_logical(last, 1), slc_pair, 0)

    @pl.when((last & 1) == 0)
    def _():
        consume_tile(last, sa_ref)

    @pl.when((last & 1) == 1)
    def _():
        score_tile(last, sb_ref)
        consume_tile(last - 1, sa_ref)
        consume_tile(last, sb_ref)

    o_slc = accl_ref[:, :LANES] / accl_ref[:, LANES:]

    wspan = WINDOW + Q_BLOCK
    w0 = pl.multiple_of(jnp.maximum(q0 - WINDOW, 0), Q_BLOCK)
    sw_raw = _dot_nt(q4, kw_ref[pl.ds(w0, wspan), :])

    def full_window(sw):
        return sw + per_head(wb_ref[...])

    def clipped_window(sw):
        kpos = lax.broadcasted_iota(jnp.int32, (R, wspan), 1)
        tw = tpos((R, wspan))
        ok = jnp.where(kpos <= tw, jnp.where(kpos > tw - WINDOW, 1.0, 0.0), 0.0) > 0.5
        return jnp.where(ok, sw, NEG_BIG)

    sw = lax.cond(q0 >= WINDOW, full_window, clipped_window, sw_raw)
    ew = jnp.exp2(sw - jnp.max(sw, axis=-1, keepdims=True))
    vwa = jnp.concatenate([vw_ref[pl.ds(w0, wspan), :], jnp.ones((wspan, LANES), BF16)], axis=1)
    pvw = _dot(ew.astype(BF16), vwa)
    o_win = pvw[:, :LANES] / pvw[:, LANES:]

    gsig = jax.nn.sigmoid(gate_ref[...])
    for h in range(NSA_HG):
        rs = slice(h * Q_BLOCK, (h + 1) * Q_BLOCK)
        cs = slice(h * HEAD_DIM, (h + 1) * HEAD_DIM)
        oh = (gsig[:, 3 * h:3 * h + 1] * o_cmp[rs]
              + gsig[:, 3 * h + 1:3 * h + 2] * o_slc[rs]
              + gsig[:, 3 * h + 2:3 * h + 3] * o_win[rs])
        z = z_ref[:, cs]
        o_ref[:, cs] = (oh * (z * jax.nn.sigmoid(z))).astype(o_ref.dtype)


def _cmp_staircase():
    nb = Q_BLOCK // CMP_STRIDE
    a = np.zeros((Q_BLOCK, LANES), np.float32)
    a[:, 0] = 1.0
    r = np.arange(Q_BLOCK)[:, None]
    i = np.arange(nb)[None, :]
    a[:, 1:1 + nb] = np.where(r < CMP_STRIDE * i + (CMP_STRIDE - 1), NEG_BIG, 0.0)
    return a


def _window_band():
    r = np.arange(Q_BLOCK)[:, None]
    c = np.arange(WINDOW + Q_BLOCK)[None, :]
    return np.where((c > r) & (c <= WINDOW + r), 0.0, NEG_BIG).astype(np.float32)


def _causal_staircases(tk):
    n_q = tk // Q_BLOCK
    r = np.arange(Q_BLOCK)[None, :, None]
    c = np.arange(tk)[None, None, :]
    d = np.arange(n_q + 1)[:, None, None]
    return np.where((c <= Q_BLOCK * d + r) | (d == n_q), 0.0, NEG_BIG).astype(np.float32)


def _overlap_matrix(ncp, nsel_pad, nsel):
    ci = np.arange(ncp)[:, None] * CMP_STRIDE
    sj = np.arange(nsel_pad)[None, :] * SEL_LEN
    ov = (ci < sj + SEL_LEN) & (ci + CMP_LEN > sj) & (np.arange(nsel_pad)[None, :] < nsel) & (np.arange(ncp)[:, None] < ncp - 1)
    return ov.astype(np.float32)


def _nsa(proj, cosf, sinf, kcvc, kvb, tk=512):
    S = proj.shape[0]
    ncp = kcvc.shape[2]
    nsel = S // SEL_LEN
    assert nsel <= LANES and S % tk == 0 and S >= WINDOW + Q_BLOCK
    assert CMP_LEN == 2 * CMP_STRIDE and Q_BLOCK % CMP_STRIDE == 0 and WINDOW % Q_BLOCK == 0
    ov = jnp.asarray(_overlap_matrix(ncp, LANES, nsel), BF16)
    ca = jnp.asarray(_cmp_staircase(), BF16)
    wb = jnp.asarray(_window_band(), F32)
    n_q = tk // Q_BLOCK
    assert n_q & (n_q - 1) == 0 and tk % SEL_LEN == 0
    cb = jnp.asarray(_causal_staircases(tk), F32)
    qw = NSA_HG * HEAD_DIM
    R = NSA_HG * Q_BLOCK
    kern = functools.partial(_nsa_kernel, tk=tk)
    return pl.pallas_call(
        kern,
        grid=(NSA_KV, S // Q_BLOCK),
        in_specs=[
            pl.BlockSpec((Q_BLOCK, qw), lambda g, n: (n, OFF_Q // qw + g)),
            pl.BlockSpec((Q_BLOCK, LANES), lambda g, n: (n, 0)),
            pl.BlockSpec((Q_BLOCK, LANES), lambda g, n: (n, 0)),
            pl.BlockSpec((Q_BLOCK, LANES), lambda g, n: (n, OFF_G // LANES + g)),
            pl.BlockSpec((Q_BLOCK, qw), lambda g, n: (n, OFF_Z // qw + g)),
            pl.BlockSpec((1, 1, ncp, HEAD_DIM), lambda g, n: (0, g, 0, 0)),
            pl.BlockSpec((1, 1, ncp, HEAD_DIM), lambda g, n: (1, g, 0, 0)),
            pl.BlockSpec((ncp, LANES), lambda g, n: (0, 0)),
            pl.BlockSpec((Q_BLOCK, LANES), lambda g, n: (0, 0)),
            pl.BlockSpec((Q_BLOCK, WINDOW + Q_BLOCK), lambda g, n: (0, 0)),
            pl.BlockSpec((n_q + 1, Q_BLOCK, tk), lambda g, n: (0, 0, 0)),
            pl.BlockSpec((S, HEAD_DIM), lambda g, n: (0, 0 * NSA_KV + g)),
            pl.BlockSpec((S, HEAD_DIM), lambda g, n: (0, 1 * NSA_KV + g)),
            pl.BlockSpec((S, HEAD_DIM), lambda g, n: (0, 2 * NSA_KV + g)),
            pl.BlockSpec((S, HEAD_DIM), lambda g, n: (0, 3 * NSA_KV + g)),
        ],
        out_specs=pl.BlockSpec((Q_BLOCK, qw), lambda g, n: (n, g)),
        out_shape=jax.ShapeDtypeStruct((S, NSA_WIDTH), BF16),
        scratch_shapes=[
            pltpu.VMEM((R, LANES), F32),
            pltpu.VMEM((R, HEAD_DIM + LANES), F32),
            pltpu.VMEM((R, tk), F32),
            pltpu.VMEM((R, tk), F32),
        ],
        compiler_params=_cparams(("arbitrary", "arbitrary")),
        name="nsa",
    )(proj, cosf, sinf, proj, proj, kcvc, kcvc, ov, ca, wb, cb, kvb, kvb, kvb, kvb)


def _merge_kernel(ohg_ref, onsa_ref, g0_ref, g1_ref, g2_ref, g3_ref, whg_ref, wnsa_ref, o_ref):
    y_hg = _dot(ohg_ref[...], whg_ref[...])
    y_nsa = _dot(onsa_ref[...], wnsa_ref[...])
    half = D_MODEL // 2
    gh = (g0_ref, g1_ref)
    gn = (g2_ref, g3_ref)
    for c in range(2):
        cs = slice(c * half, (c + 1) * half)
        m = jax.nn.sigmoid(gh[c][...]) * y_hg[:, cs] + jax.nn.sigmoid(gn[c][...]) * y_nsa[:, cs]
        o_ref[:, cs] = m.astype(o_ref.dtype)


def _merge(o_hg, o_nsa, proj, w_hg, w_nsa, tm=256):
    S = o_hg.shape[0]
    tm = min(tm, S)
    half = D_MODEL // 2
    gb = OFF_MG // half
    return pl.pallas_call(
        _merge_kernel,
        grid=(S // tm,),
        in_specs=[
            pl.BlockSpec((tm, HG_WIDTH), lambda i: (i, 0)),
            pl.BlockSpec((tm, NSA_WIDTH), lambda i: (i, 0)),
            pl.BlockSpec((tm, half), lambda i: (i, gb + 0)),
            pl.BlockSpec((tm, half), lambda i: (i, gb + 1)),
            pl.BlockSpec((tm, half), lambda i: (i, gb + 2)),
            pl.BlockSpec((tm, half), lambda i: (i, gb + 3)),
            pl.BlockSpec((HG_WIDTH, D_MODEL), lambda i: (0, 0)),
            pl.BlockSpec((NSA_WIDTH, D_MODEL), lambda i: (0, 0)),
        ],
        out_specs=pl.BlockSpec((tm, D_MODEL), lambda i: (i, 0)),
        out_shape=jax.ShapeDtypeStruct((S, D_MODEL), BF16),
        compiler_params=_cparams(("arbitrary",)),
        name="merge",
    )(o_hg, o_nsa, proj, proj, proj, proj, w_hg, w_nsa)


def _out_kernel(x_ref, m_ref, w_ref, nw_ref, o_ref):
    h = x_ref[...] + _dot(m_ref[...], w_ref[...])
    ms = jnp.mean(h * h, axis=-1, keepdims=True)
    o_ref[...] = h * lax.rsqrt(ms + EPS) * nw_ref[...]


def _out(x2, merged, w_out, final_w, tm=256):
    S, D = x2.shape
    tm = min(tm, S)
    return pl.pallas_call(
        _out_kernel,
        grid=(S // tm,),
        in_specs=[
            pl.BlockSpec((tm, D), lambda i: (i, 0)),
            pl.BlockSpec((tm, D), lambda i: (i, 0)),
            pl.BlockSpec((D, D), lambda i: (0, 0)),
            pl.BlockSpec((1, D), lambda i: (0, 0)),
        ],
        out_specs=pl.BlockSpec((tm, D), lambda i: (i, 0)),
        out_shape=jax.ShapeDtypeStruct((S, D), F32),
        compiler_params=_cparams(("arbitrary",)),
        name="out_proj",
    )(x2, merged, w_out, final_w.reshape(1, D))


def _layout_w_in(w):
    gates = w[:, IN_GATE_OFF:IN_Z_OFF].reshape(D_MODEL, NSA_KV, 3 * NSA_HG)
    gates = jnp.pad(gates, ((0, 0), (0, 0), (0, LANES - 3 * NSA_HG))).reshape(D_MODEL, NSA_KV * LANES)
    return jnp.concatenate(
        [w[:, :IN_GATE_OFF], w[:, IN_MG_OFF:IN_MG_OFF + 2 * D_MODEL], w[:, IN_Z_OFF:IN_MG_OFF], gates],
        axis=1).astype(BF16)


def _layer(x2, norm_w, w_in, lb_logits, hg_norm_w, cmp_k_pos, cmp_k_w1, cmp_k_b1, cmp_k_w2,
           cmp_v_pos, cmp_v_w1, cmp_v_b1, cmp_v_w2, w_branch_hg, w_branch_nsa, w_out, final_w):
    S = x2.shape[0]
    proj = _norm_proj(x2, norm_w, _layout_w_in(w_in))
    cosf, sinf = _rope_tables(S)
    cmp_in, kvb = _kv_prep(proj, cosf, sinf)
    segs = cmp_in.reshape(2, NSA_KV, S // CMP_STRIDE, CMP_STRIDE * HEAD_DIM)
    pe = jnp.stack([cmp_k_pos.reshape(1, -1), cmp_v_pos.reshape(1, -1)])
    w1 = jnp.stack([cmp_k_w1, cmp_v_w1]).astype(BF16)
    b1 = jnp.stack([cmp_k_b1.reshape(1, -1), cmp_v_b1.reshape(1, -1)])
    w2 = jnp.stack([cmp_k_w2, cmp_v_w2]).astype(BF16)
    kcvc = _compress(segs, pe, w1, b1, w2)
    o_hg = _hgrn(proj, lb_logits, hg_norm_w)
    o_nsa = _nsa(proj, cosf, sinf, kcvc, kvb)
    merged = _merge(o_hg, o_nsa, proj, w_branch_hg.astype(BF16), w_branch_nsa.astype(BF16))
    return _out(x2, merged, w_out.astype(BF16), final_w)


def kernel(x, norm_w, w_in, hg_lb_logits, hg_norm_w, cmp_k_pos, cmp_k_w1, cmp_k_b1, cmp_k_w2, cmp_v_pos, cmp_v_w1, cmp_v_b1, cmp_v_w2, w_branch_hg, w_branch_nsa, w_out, final_norm_w):
    B, S, D = x.shape
    assert B == 1 and D == D_MODEL and norm_w.shape[0] == 1
    y = _layer(x[0], norm_w[0], w_in[0], hg_lb_logits, hg_norm_w[0], cmp_k_pos[0], cmp_k_w1[0], cmp_k_b1[0],
               cmp_k_w2[0], cmp_v_pos[0], cmp_v_w1[0], cmp_v_b1[0], cmp_v_w2[0], w_branch_hg[0],
               w_branch_nsa[0], w_out[0], final_norm_w)
    return y[None]
```

```python
import functools

import numpy as np
import jax
import jax.numpy as jnp
from jax import lax
from jax.experimental import pallas as pl
from jax.experimental.pallas import tpu as pltpu

F32 = jnp.float32
BF16 = jnp.bfloat16

D_MODEL = 2048
HG_HEADS = 8
HG_DK = 128
HG_DV = 128
HG_FDIM = HG_HEADS * HG_DK
HG_WIDTH = HG_HEADS * HG_DV
NSA_HEADS = 16
NSA_KV = 4
NSA_HG = NSA_HEADS // NSA_KV
HEAD_DIM = 128
NSA_WIDTH = NSA_HEADS * HEAD_DIM
NSA_KVW = NSA_KV * HEAD_DIM
CMP_LEN = 32
CMP_STRIDE = 16
CMP_HIDDEN = 512
SEL_LEN = 64
SEL_TOPK = 16
WINDOW = 512
Q_BLOCK = 256
ROPE_THETA = 500000.0
ROT_DIM = HEAD_DIM // 4
ROT_HALF = ROT_DIM // 2
EPS = 1e-6
LOG2E = float(np.log2(np.e))

LANES = 128
NEG_BIG = -1e30
FORCED_SCORE = 1e30
VMEM_LIMIT = 56 * 1024 * 1024

OFF_HG = 0
OFF_Q = 4 * HG_FDIM
OFF_KV = OFF_Q + NSA_WIDTH
OFF_MG = OFF_KV + 6 * NSA_KVW
OFF_Z = OFF_MG + 2 * D_MODEL
OFF_G = OFF_Z + NSA_WIDTH
PROJ_COLS = OFF_G + NSA_KV * LANES
IN_GATE_OFF = OFF_MG
IN_Z_OFF = IN_GATE_OFF + 3 * NSA_HEADS
IN_MG_OFF = IN_Z_OFF + NSA_WIDTH


def _dot(a, b):
    return jnp.dot(a, b, preferred_element_type=F32)


def _dot_nt(a, b):
    return lax.dot_general(a, b, (((1,), (1,)), ((), ())), preferred_element_type=F32)


def _split_bf16(a):
    hi = a.astype(BF16)
    lo = (a - hi.astype(F32)).astype(BF16)
    return hi, lo


def _cparams(sem):
    return pltpu.CompilerParams(dimension_semantics=sem, vmem_limit_bytes=VMEM_LIMIT)


def _norm_proj_kernel(x_ref, nw_ref, w_ref, o_ref, xn_ref):
    @pl.when(pl.program_id(1) == 0)
    def _():
        x = x_ref[...]
        ms = jnp.mean(x * x, axis=-1, keepdims=True)
        xn_ref[...] = (x * lax.rsqrt(ms + EPS) * nw_ref[...]).astype(BF16)

    o_ref[...] = _dot(xn_ref[...], w_ref[...])


def _norm_proj(x2, norm_w, w_all, tm=1024, tn=512):
    S, D = x2.shape
    N = w_all.shape[1]
    tm = min(tm, S)
    return pl.pallas_call(
        _norm_proj_kernel,
        grid=(S // tm, N // tn),
        in_specs=[
            pl.BlockSpec((tm, D), lambda i, j: (i, 0)),
            pl.BlockSpec((1, D), lambda i, j: (0, 0)),
            pl.BlockSpec((D, tn), lambda i, j: (0, j)),
        ],
        out_specs=pl.BlockSpec((tm, tn), lambda i, j: (i, j)),
        out_shape=jax.ShapeDtypeStruct((S, N), F32),
        scratch_shapes=[pltpu.VMEM((tm, D), BF16)],
        compiler_params=_cparams(("arbitrary", "arbitrary")),
        name="norm_proj",
    )(x2, norm_w.reshape(1, D), w_all)


def _rope(x, cosf, sinf):
    lane = lax.broadcasted_iota(jnp.int32, x.shape, 1)
    rot = jnp.where(lane < ROT_HALF, pltpu.roll(x, LANES - ROT_HALF, 1), pltpu.roll(x, ROT_HALF, 1))
    return x * cosf + rot * sinf


def _rope_tables(S):
    pos = np.arange(S, dtype=np.float64)
    inv = ROPE_THETA ** (-np.arange(0, ROT_DIM, 2, dtype=np.float64) / ROT_DIM)
    ang = pos[:, None] * inv[None, :]
    cos, sin = np.cos(ang), np.sin(ang)
    rest = LANES - ROT_DIM
    cosf = np.concatenate([cos, cos, np.ones((S, rest))], axis=1)
    sinf = np.concatenate([-sin, sin, np.zeros((S, rest))], axis=1)
    return jnp.asarray(cosf, F32), jnp.asarray(sinf, F32)


def _kv_prep_kernel(kv_ref, cos_ref, sin_ref, cmp_ref, kvb_ref):
    cosf = cos_ref[...]
    sinf = sin_ref[...]
    W = NSA_KVW
    for g in range(NSA_KV):
        kc = kv_ref[:, 0 * W + g * LANES:0 * W + (g + 1) * LANES]
        cmp_ref[0, g] = _rope(kc, cosf, sinf)
        cmp_ref[1, g] = kv_ref[:, 1 * W + g * LANES:1 * W + (g + 1) * LANES]
        ks = kv_ref[:, 2 * W + g * LANES:2 * W + (g + 1) * LANES]
        kvb_ref[:, 0 * W + g * LANES:0 * W + (g + 1) * LANES] = _rope(ks, cosf, sinf).astype(BF16)
        kvb_ref[:, 1 * W + g * LANES:1 * W + (g + 1) * LANES] = kv_ref[:, 3 * W + g * LANES:3 * W + (g + 1) * LANES].astype(BF16)
        kw = kv_ref[:, 4 * W + g * LANES:4 * W + (g + 1) * LANES]
        kvb_ref[:, 2 * W + g * LANES:2 * W + (g + 1) * LANES] = _rope(kw, cosf, sinf).astype(BF16)
        kvb_ref[:, 3 * W + g * LANES:3 * W + (g + 1) * LANES] = kv_ref[:, 5 * W + g * LANES:5 * W + (g + 1) * LANES].astype(BF16)


def _kv_prep(proj, cosf, sinf, tm=512):
    S = proj.shape[0]
    tm = min(tm, S)
    kvw = 6 * NSA_KVW
    return pl.pallas_call(
        _kv_prep_kernel,
        grid=(S // tm,),
        in_specs=[
            pl.BlockSpec((tm, kvw), lambda i: (i, OFF_KV // kvw)),
            pl.BlockSpec((tm, LANES), lambda i: (i, 0)),
            pl.BlockSpec((tm, LANES), lambda i: (i, 0)),
        ],
        out_specs=[
            pl.BlockSpec((2, NSA_KV, tm, LANES), lambda i: (0, 0, i, 0)),
            pl.BlockSpec((tm, 4 * NSA_KVW), lambda i: (i, 0)),
        ],
        out_shape=[
            jax.ShapeDtypeStruct((2, NSA_KV, S, LANES), F32),
            jax.ShapeDtypeStruct((S, 4 * NSA_KVW), BF16),
        ],
        compiler_params=_cparams(("arbitrary",)),
        name="kv_prep",
    )(proj, cosf, sinf)


def _compress_kernel(seg_ref, pe_ref, w1_ref, b1_ref, w2_ref, o_ref):
    half = (CMP_LEN // 2) * HEAD_DIM
    seg = seg_ref[0, 0]
    n_seg = seg.shape[0]
    pe = pe_ref[0]
    a = (seg + pe[:, :half]).astype(BF16)
    b = (seg + pe[:, half:]).astype(BF16)
    u = _dot(a, w1_ref[0, :half, :])
    v = _dot(b, w1_ref[0, half:, :])
    v_next = pltpu.roll(v, n_seg - 1, 0)
    pre = u + v_next + b1_ref[0]
    h = 0.5 * pre * (1.0 + jnp.tanh(np.sqrt(2.0 / np.pi).astype(np.float32) * (pre + 0.044715 * (pre * pre * pre))))
    o_ref[0, 0] = _dot(h.astype(BF16), w2_ref[0])


def _compress(segs, pe, w1, b1, w2):
    _, G, n_seg, segw = segs.shape
    return pl.pallas_call(
        _compress_kernel,
        grid=(2, G),
        in_specs=[
            pl.BlockSpec((1, 1, n_seg, segw), lambda a, g: (a, g, 0, 0)),
            pl.BlockSpec((1, 1, 2 * segw), lambda a, g: (a, 0, 0)),
            pl.BlockSpec((1, 2 * segw, CMP_HIDDEN), lambda a, g: (a, 0, 0)),
            pl.BlockSpec((1, 1, CMP_HIDDEN), lambda a, g: (a, 0, 0)),
            pl.BlockSpec((1, CMP_HIDDEN, HEAD_DIM), lambda a, g: (a, 0, 0)),
        ],
        out_specs=pl.BlockSpec((1, 1, n_seg, HEAD_DIM), lambda a, g: (a, g, 0, 0)),
        out_shape=jax.ShapeDtypeStruct((2, G, n_seg, HEAD_DIM), F32),
        compiler_params=_cparams(("arbitrary", "arbitrary")),
        name="compress",
    )(segs, pe, w1, b1, w2)


def _hgrn_consts(ch):
    nl = int(np.log2(ch))
    assert 1 << nl == ch
    t = np.arange(ch)[:, None]
    r = np.arange(ch)[None, :]
    mats = [r <= t]
    masks = [np.eye(ch, dtype=bool)]
    for l in range(nl):
        half = 1 << l
        blk = t // (2 * half)
        ref = blk * 2 * half + half - 1
        up = ((t >> l) & 1) == 1
        mats.append(np.where(up, (r > ref) & (r <= t), (r > t) & (r <= ref)))
        masks.append(up & (((r >> l) & 1) == 0) & (blk == r // (2 * half)))
    mats.append(r > t)
    mc = np.concatenate(mats, axis=0).astype(np.float32)
    lm = np.stack(masks, axis=0).astype(np.float32)
    return nl, mc, lm


def _hgrn_kernel(q_ref, f_ref, i_ref, z_ref, lbl_ref, nw_ref, mc_ref, lm_ref, o_ref, st_ref, e_ref, *, ch, nl):
    @pl.when(pl.program_id(0) == 0)
    def _():
        st_ref[...] = jnp.zeros_like(st_ref)

    lg = lbl_ref[...]
    ex = jnp.exp(lg - jnp.max(lg, axis=0, keepdims=True))
    lb = ex[0:1] / jnp.sum(ex, axis=0, keepdims=True)
    f = lb + (1.0 - lb) * jax.nn.sigmoid(f_ref[...])
    g_hi, g_lo = _split_bf16(jnp.log(f))
    mc = mc_ref[...]
    e_ref[...] = _dot(mc, g_hi) + _dot(mc, g_lo)
    row = lax.broadcasted_iota(jnp.int32, (ch, 1), 0)
    nw = nw_ref[...]

    for h in range(HG_HEADS):
        sl = slice(h * HG_DK, (h + 1) * HG_DK)
        q = q_ref[:, sl]
        k = 1.0 - f[:, sl]
        v = i_ref[:, sl]
        v_b = v.astype(BF16)
        b = e_ref[0:ch, sl]
        scores = _dot_nt(q.astype(BF16), k.astype(BF16)) * lm_ref[0]
        for l in range(nl):
            el = e_ref[(l + 1) * ch:(l + 2) * ch, sl]
            up = ((row >> l) & 1) == 1
            xl = (jnp.where(up, q, k) * jnp.exp(el)).astype(BF16)
            scores = scores + _dot_nt(xl, xl) * lm_ref[l + 1]
        intra = _dot(scores.astype(BF16), v_b)
        st = st_ref[h]
        inter = _dot_nt((q * jnp.exp(b)).astype(BF16), st.astype(BF16))
        ke = (k * jnp.exp(e_ref[(nl + 1) * ch:(nl + 2) * ch, sl])).astype(BF16)
        upd = lax.dot_general(v_b, ke, (((0,), (0,)), ((), ())), preferred_element_type=F32)
        st_ref[h] = st * jnp.exp(b[ch - 1:ch, :]) + upd
        o = inter + intra
        ms = jnp.mean(o * o, axis=-1, keepdims=True)
        z = z_ref[:, sl]
        o_ref[:, sl] = (o * lax.rsqrt(ms + EPS) * nw * (z * jax.nn.sigmoid(z))).astype(o_ref.dtype)


def _hgrn(proj, lb_logits, hg_norm_w, ch=128):
    S = proj.shape[0]
    nl, mc, lm = _hgrn_consts(ch)
    kern = functools.partial(_hgrn_kernel, ch=ch, nl=nl)
    nlb = lb_logits.shape[0]
    return pl.pallas_call(
        kern,
        grid=(S // ch,),
        in_specs=[
            pl.BlockSpec((ch, HG_FDIM), lambda c: (c, 0)),
            pl.BlockSpec((ch, HG_FDIM), lambda c: (c, 1)),
            pl.BlockSpec((ch, HG_WIDTH), lambda c: (c, 2)),
            pl.BlockSpec((ch, HG_WIDTH), lambda c: (c, 3)),
            pl.BlockSpec((nlb, HG_FDIM), lambda c: (0, 0)),
            pl.BlockSpec((1, HG_DV), lambda c: (0, 0)),
            pl.BlockSpec(mc.shape, lambda c: (0, 0)),
            pl.BlockSpec(lm.shape, lambda c: (0, 0, 0)),
        ],
        out_specs=pl.BlockSpec((ch, HG_WIDTH), lambda c: (c, 0)),
        out_shape=jax.ShapeDtypeStruct((S, HG_WIDTH), BF16),
        scratch_shapes=[
            pltpu.VMEM((HG_HEADS, HG_DV, HG_DK), F32),
            pltpu.VMEM(((nl + 2) * ch, HG_FDIM), F32),
        ],
        compiler_params=_cparams(("arbitrary",)),
        name="hgrn",
    )(proj, proj, proj, proj, lb_logits, hg_norm_w.reshape(1, HG_DV), jnp.asarray(mc, BF16), jnp.asarray(lm, F32))


def _nsa_kernel(q_ref, cos_ref, sin_ref, gate_ref, z_ref, kc_ref, vc_ref, ov_ref, ca_ref, wb_ref, cb_ref,
                ks_ref, vs_ref, kw_ref, vw_ref, o_ref, m_ref, accl_ref, sa_ref, sb_ref, part_ref, *, tk):
    n = pl.program_id(1)
    q0 = n * Q_BLOCK
    R = NSA_HG * Q_BLOCK
    qscale = (HEAD_DIM ** -0.5) * LOG2E
    cosf = cos_ref[...]
    sinf = sin_ref[...]
    q4 = jnp.concatenate(
        [(_rope(q_ref[:, h * HEAD_DIM:(h + 1) * HEAD_DIM], cosf, sinf) * qscale).astype(BF16) for h in range(NSA_HG)],
        axis=0)

    def per_head(a):
        return jnp.concatenate([a] * NSA_HG, axis=0)

    def tpos(shape):
        return q0 + (lax.broadcasted_iota(jnp.int32, shape, 0) & (Q_BLOCK - 1))

    ncp = kc_ref.shape[2]
    nb = Q_BLOCK // CMP_STRIDE
    c_first = n * nb - (CMP_LEN // CMP_STRIDE - 1)
    crow = lax.broadcasted_iota(jnp.int32, (ncp, LANES), 0)
    clane = lax.broadcasted_iota(jnp.int32, (ncp, LANES), 1)
    flags = jnp.where(clane == 0, jnp.where(crow >= c_first + nb, NEG_BIG, 0.0),
                      jnp.where(crow == c_first + clane - 1, 1.0, 0.0)).astype(BF16)
    kca = jnp.concatenate([kc_ref[0, 0].astype(BF16), flags], axis=1)
    s = _dot_nt(jnp.concatenate([q4, per_head(ca_ref[...])], axis=1), kca)
    e = jnp.exp2(s - jnp.max(s, axis=-1, keepdims=True))
    vca = jnp.concatenate([vc_ref[0, 0].astype(BF16), jnp.ones((ncp, LANES), BF16)], axis=1)
    pv = _dot(e.astype(BF16), vca)
    any_visible = jnp.where(tpos((R, LANES)) >= CMP_LEN - 1, 1.0, 0.0)
    inv = any_visible / pv[:, LANES:]
    o_cmp = pv[:, :LANES] * inv
    p = e * jnp.concatenate([inv] * (ncp // LANES), axis=1)
    p_sum = p[0:Q_BLOCK]
    for h in range(1, NSA_HG):
        p_sum = p_sum + p[h * Q_BLOCK:(h + 1) * Q_BLOCK]
    ov = ov_ref[...]
    ps_hi, ps_lo = _split_bf16(p_sum)
    imp = _dot(ps_hi, ov) + _dot(ps_lo, ov)

    wspan = WINDOW + Q_BLOCK
    w0 = pl.multiple_of(jnp.maximum(q0 - WINDOW, 0), Q_BLOCK)
    sw = _dot_nt(q4, kw_ref[pl.ds(w0, wspan), :]) + per_head(wb_ref[jnp.minimum(n, WINDOW // Q_BLOCK)])
    ew = jnp.exp2(sw - jnp.max(sw, axis=-1, keepdims=True))
    vwa = jnp.concatenate([vw_ref[pl.ds(w0, wspan), :], jnp.ones((wspan, LANES), BF16)], axis=1)
    pvw = _dot(ew.astype(BF16), vwa)
    o_win = pvw[:, :LANES] / pvw[:, LANES:]

    gate = jax.nn.sigmoid(gate_ref[...])
    for h in range(NSA_HG):
        rs = slice(h * Q_BLOCK, (h + 1) * Q_BLOCK)
        part_ref[:, h * HEAD_DIM:(h + 1) * HEAD_DIM] = (
            gate[:, 3 * h:3 * h + 1] * o_cmp[rs] + gate[:, 3 * h + 2:3 * h + 3] * o_win[rs])

    imp_t = imp.T
    nsel_pad = imp_t.shape[0]
    jj = lax.broadcasted_iota(jnp.int32, (nsel_pad, Q_BLOCK), 0)
    tq = q0 + lax.broadcasted_iota(jnp.int32, (nsel_pad, Q_BLOCK), 1)
    jt = tq >> 6
    forced = (jj == 0) | (jj == jt) | (jj == jt - 1)
    jjf = jj.astype(F32)
    score = jnp.where(jj * SEL_LEN <= tq, jnp.where(forced, FORCED_SCORE, imp_t), -1.0)
    sel = jnp.zeros((nsel_pad, Q_BLOCK), F32)
    for _ in range(SEL_TOPK):
        mx = jnp.max(score, axis=0, keepdims=True)
        first = jnp.min(jnp.where(score == mx, jjf, float(nsel_pad)), axis=0, keepdims=True)
        hit = jjf == first
        sel = jnp.where(hit, jnp.where(mx >= 0.0, 1.0, 0.0), sel)
        score = jnp.where(hit, -2.0, score)
    sel_bias = per_head(((sel.T - 1.0) * (-NEG_BIG)).astype(BF16))

    m_ref[...] = jnp.full(m_ref.shape, NEG_BIG, F32)
    accl_ref[...] = jnp.zeros(accl_ref.shape, F32)
    qa = jnp.concatenate([q4, sel_bias], axis=1)
    key_blk = lax.broadcasted_iota(jnp.int32, (tk, LANES), 0) >> 6
    blk_lane = lax.broadcasted_iota(jnp.int32, (tk, LANES), 1)
    ones_k = jnp.ones((tk, LANES), BF16)
    rep = tk // LANES

    n_q = tk // Q_BLOCK
    last = lax.shift_right_logical(n, n_q.bit_length() - 1)
    diag = n - last * n_q

    def score_tile(kt, dst_ref):
        k0 = pl.multiple_of(kt * tk, tk)
        onehot = jnp.where(key_blk + kt * (tk // SEL_LEN) == blk_lane, 1.0, 0.0).astype(BF16)
        dst_ref[...] = _dot_nt(qa, jnp.concatenate([ks_ref[pl.ds(k0, tk), :], onehot], axis=1))

    def consume_tile(kt, src_ref):
        k0 = pl.multiple_of(kt * tk, tk)
        sc = src_ref[...] + per_head(cb_ref[jnp.where(kt == last, diag, n_q)])
        m_prev = m_ref[...]
        m_next = jnp.maximum(m_prev, jnp.max(sc, axis=-1, keepdims=True))
        pr = jnp.exp2(sc - jnp.concatenate([m_next] * rep, axis=1))
        alpha = jnp.exp2(m_prev - m_next)
        va = jnp.concatenate([vs_ref[pl.ds(k0, tk), :], ones_k], axis=1)
        accl_ref[...] = jnp.concatenate([alpha, alpha], axis=1) * accl_ref[...] + _dot(pr.astype(BF16), va)
        m_ref[...] = m_next

    score_tile(0, sa_ref)

    def slc_pair(i, carry):
        score_tile(2 * i + 1, sb_ref)
        consume_tile(2 * i, sa_ref)
        score_tile(2 * i + 2, sa_ref)
        consume_tile(2 * i + 1, sb_ref)
        return carry

    lax.fori_loop(0, lax.shift_right_logical(last, 1), slc_pair, 0)

    @pl.when((last & 1) == 0)
    def _():
        consume_tile(last, sa_ref)

    @pl.when((last & 1) == 1)
    def _():
        score_tile(last, sb_ref)
        consume_tile(last - 1, sa_ref)
        consume_tile(last, sb_ref)

    o_slc = accl_ref[:, :LANES] / accl_ref[:, LANES:]

    gsig = jax.nn.sigmoid(gate_ref[...])
    for h in range(NSA_HG):
        rs = slice(h * Q_BLOCK, (h + 1) * Q_BLOCK)
        cs = slice(h * HEAD_DIM, (h + 1) * HEAD_DIM)
        oh = part_ref[:, cs] + gsig[:, 3 * h + 1:3 * h + 2] * o_slc[rs]
        z = z_ref[:, cs]
        o_ref[:, cs] = (oh * (z * jax.nn.sigmoid(z))).astype(o_ref.dtype)


def _cmp_staircase():
    nb = Q_BLOCK // CMP_STRIDE
    a = np.zeros((Q_BLOCK, LANES), np.float32)
    a[:, 0] = 1.0
    r = np.arange(Q_BLOCK)[:, None]
    i = np.arange(nb)[None, :]
    a[:, 1:1 + nb] = np.where(r < CMP_STRIDE * i + (CMP_STRIDE - 1), NEG_BIG, 0.0)
    return a


def _window_masks():
    n_w = WINDOW // Q_BLOCK
    r = np.arange(Q_BLOCK)[None, :, None]
    c = np.arange(WINDOW + Q_BLOCK)[None, None, :]
    d = np.arange(n_w + 1)[:, None, None]
    ok = np.where(d < n_w, c <= Q_BLOCK * d + r, (c > r) & (c <= WINDOW + r))
    return np.where(ok, 0.0, NEG_BIG).astype(np.float32)


def _causal_staircases(tk):
    n_q = tk // Q_BLOCK
    r = np.arange(Q_BLOCK)[None, :, None]
    c = np.arange(tk)[None, None, :]
    d = np.arange(n_q + 1)[:, None, None]
    return np.where((c <= Q_BLOCK * d + r) | (d == n_q), 0.0, NEG_BIG).astype(np.float32)


def _overlap_matrix(ncp, nsel_pad, nsel):
    ci = np.arange(ncp)[:, None] * CMP_STRIDE
    sj = np.arange(nsel_pad)[None, :] * SEL_LEN
    ov = (ci < sj + SEL_LEN) & (ci + CMP_LEN > sj) & (np.arange(nsel_pad)[None, :] < nsel) & (np.arange(ncp)[:, None] < ncp - 1)
    return ov.astype(np.float32)


def _nsa(proj, cosf, sinf, kcvc, kvb, tk=512):
    S = proj.shape[0]
    ncp = kcvc.shape[2]
    nsel = S // SEL_LEN
    assert nsel <= LANES and S % tk == 0 and S >= WINDOW + Q_BLOCK
    assert CMP_LEN == 2 * CMP_STRIDE and Q_BLOCK % CMP_STRIDE == 0 and WINDOW % Q_BLOCK == 0
    ov = jnp.asarray(_overlap_matrix(ncp, LANES, nsel), BF16)
    ca = jnp.asarray(_cmp_staircase(), BF16)
    wb = jnp.asarray(_window_masks(), F32)
    n_q = tk // Q_BLOCK
    assert n_q & (n_q - 1) == 0 and tk % SEL_LEN == 0
    cb = jnp.asarray(_causal_staircases(tk), F32)
    qw = NSA_HG * HEAD_DIM
    R = NSA_HG * Q_BLOCK
    kern = functools.partial(_nsa_kernel, tk=tk)
    return pl.pallas_call(
        kern,
        grid=(NSA_KV, S // Q_BLOCK),
        in_specs=[
            pl.BlockSpec((Q_BLOCK, qw), lambda g, n: (n, OFF_Q // qw + g)),
            pl.BlockSpec((Q_BLOCK, LANES), lambda g, n: (n, 0)),
            pl.BlockSpec((Q_BLOCK, LANES), lambda g, n: (n, 0)),
            pl.BlockSpec((Q_BLOCK, LANES), lambda g, n: (n, OFF_G // LANES + g)),
            pl.BlockSpec((Q_BLOCK, qw), lambda g, n: (n, OFF_Z // qw + g)),
            pl.BlockSpec((1, 1, ncp, HEAD_DIM), lambda g, n: (0, g, 0, 0)),
            pl.BlockSpec((1, 1, ncp, HEAD_DIM), lambda g, n: (1, g, 0, 0)),
            pl.BlockSpec((ncp, LANES), lambda g, n: (0, 0)),
            pl.BlockSpec((Q_BLOCK, LANES), lambda g, n: (0, 0)),
            pl.BlockSpec((WINDOW // Q_BLOCK + 1, Q_BLOCK, WINDOW + Q_BLOCK), lambda g, n: (0, 0, 0)),
            pl.BlockSpec((n_q + 1, Q_BLOCK, tk), lambda g, n: (0, 0, 0)),
            pl.BlockSpec((S, HEAD_DIM), lambda g, n: (0, 0 * NSA_KV + g)),
            pl.BlockSpec((S, HEAD_DIM), lambda g, n: (0, 1 * NSA_KV + g)),
            pl.BlockSpec((S, HEAD_DIM), lambda g, n: (0, 2 * NSA_KV + g)),
            pl.BlockSpec((S, HEAD_DIM), lambda g, n: (0, 3 * NSA_KV + g)),
        ],
        out_specs=pl.BlockSpec((Q_BLOCK, qw), lambda g, n: (n, g)),
        out_shape=jax.ShapeDtypeStruct((S, NSA_WIDTH), BF16),
        scratch_shapes=[
            pltpu.VMEM((R, LANES), F32),
            pltpu.VMEM((R, HEAD_DIM + LANES), F32),
            pltpu.VMEM((R, tk), F32),
            pltpu.VMEM((R, tk), F32),
            pltpu.VMEM((Q_BLOCK, qw), F32),
        ],
        compiler_params=_cparams(("arbitrary", "arbitrary")),
        name="nsa",
    )(proj, cosf, sinf, proj, proj, kcvc, kcvc, ov, ca, wb, cb, kvb, kvb, kvb, kvb)


def _merge_kernel(ohg_ref, onsa_ref, g0_ref, g1_ref, g2_ref, g3_ref, whg_ref, wnsa_ref, o_ref):
    y_hg = _dot(ohg_ref[...], whg_ref[...])
    y_nsa = _dot(onsa_ref[...], wnsa_ref[...])
    half = D_MODEL // 2
    gh = (g0_ref, g1_ref)
    gn = (g2_ref, g3_ref)
    for c in range(2):
        cs = slice(c * half, (c + 1) * half)
        m = jax.nn.sigmoid(gh[c][...]) * y_hg[:, cs] + jax.nn.sigmoid(gn[c][...]) * y_nsa[:, cs]
        o_ref[:, cs] = m.astype(o_ref.dtype)


def _merge(o_hg, o_nsa, proj, w_hg, w_nsa, tm=256):
    S = o_hg.shape[0]
    tm = min(tm, S)
    half = D_MODEL // 2
    gb = OFF_MG // half
    return pl.pallas_call(
        _merge_kernel,
        grid=(S // tm,),
        in_specs=[
            pl.BlockSpec((tm, HG_WIDTH), lambda i: (i, 0)),
            pl.BlockSpec((tm, NSA_WIDTH), lambda i: (i, 0)),
            pl.BlockSpec((tm, half), lambda i: (i, gb + 0)),
            pl.BlockSpec((tm, half), lambda i: (i, gb + 1)),
            pl.BlockSpec((tm, half), lambda i: (i, gb + 2)),
            pl.BlockSpec((tm, half), lambda i: (i, gb + 3)),
            pl.BlockSpec((HG_WIDTH, D_MODEL), lambda i: (0, 0)),
            pl.BlockSpec((NSA_WIDTH, D_MODEL), lambda i: (0, 0)),
        ],
        out_specs=pl.BlockSpec((tm, D_MODEL), lambda i: (i, 0)),
        out_shape=jax.ShapeDtypeStruct((S, D_MODEL), BF16),
        compiler_params=_cparams(("arbitrary",)),
        name="merge",
    )(o_hg, o_nsa, proj, proj, proj, proj, w_hg, w_nsa)


def _out_kernel(x_ref, m_ref, w_ref, nw_ref, o_ref):
    h = x_ref[...] + _dot(m_ref[...], w_ref[...])
    ms = jnp.mean(h * h, axis=-1, keepdims=True)
    o_ref[...] = h * lax.rsqrt(ms + EPS) * nw_ref[...]


def _out(x2, merged, w_out, final_w, tm=256):
    S, D = x2.shape
    tm = min(tm, S)
    return pl.pallas_call(
        _out_kernel,
        grid=(S // tm,),
        in_specs=[
            pl.BlockSpec((tm, D), lambda i: (i, 0)),
            pl.BlockSpec((tm, D), lambda i: (i, 0)),
            pl.BlockSpec((D, D), lambda i: (0, 0)),
            pl.BlockSpec((1, D), lambda i: (0, 0)),
        ],
        out_specs=pl.BlockSpec((tm, D), lambda i: (i, 0)),
        out_shape=jax.ShapeDtypeStruct((S, D), F32),
        compiler_params=_cparams(("arbitrary",)),
        name="out_proj",
    )(x2, merged, w_out, final_w.reshape(1, D))


def _layout_w_in(w):
    gates = w[:, IN_GATE_OFF:IN_Z_OFF].reshape(D_MODEL, NSA_KV, 3 * NSA_HG)
    gates = jnp.pad(gates, ((0, 0), (0, 0), (0, LANES - 3 * NSA_HG))).reshape(D_MODEL, NSA_KV * LANES)
    return jnp.concatenate(
        [w[:, :IN_GATE_OFF], w[:, IN_MG_OFF:IN_MG_OFF + 2 * D_MODEL], w[:, IN_Z_OFF:IN_MG_OFF], gates],
        axis=1).astype(BF16)


def _layer(x2, norm_w, w_in, lb_logits, hg_norm_w, cmp_k_pos, cmp_k_w1, cmp_k_b1, cmp_k_w2,
           cmp_v_pos, cmp_v_w1, cmp_v_b1, cmp_v_w2, w_branch_hg, w_branch_nsa, w_out, final_w):
    S = x2.shape[0]
    proj = _norm_proj(x2, norm_w, _layout_w_in(w_in))
    cosf, sinf = _rope_tables(S)
    cmp_in, kvb = _kv_prep(proj, cosf, sinf)
    segs = cmp_in.reshape(2, NSA_KV, S // CMP_STRIDE, CMP_STRIDE * HEAD_DIM)
    pe = jnp.stack([cmp_k_pos.reshape(1, -1), cmp_v_pos.reshape(1, -1)])
    w1 = jnp.stack([cmp_k_w1, cmp_v_w1]).astype(BF16)
    b1 = jnp.stack([cmp_k_b1.reshape(1, -1), cmp_v_b1.reshape(1, -1)])
    w2 = jnp.stack([cmp_k_w2, cmp_v_w2]).astype(BF16)
    kcvc = _compress(segs, pe, w1, b1, w2)
    o_hg = _hgrn(proj, lb_logits, hg_norm_w)
    o_nsa = _nsa(proj, cosf, sinf, kcvc, kvb)
    merged = _merge(o_hg, o_nsa, proj, w_branch_hg.astype(BF16), w_branch_nsa.astype(BF16))
    return _out(x2, merged, w_out.astype(BF16), final_w)


def kernel(x, norm_w, w_in, hg_lb_logits, hg_norm_w, cmp_k_pos, cmp_k_w1, cmp_k_b1, cmp_k_w2, cmp_v_pos, cmp_v_w1, cmp_v_b1, cmp_v_w2, w_branch_hg, w_branch_nsa, w_out, final_norm_w):
    B, S, D = x.shape
    assert B == 1 and D == D_MODEL and norm_w.shape[0] == 1
    y = _layer(x[0], norm_w[0], w_in[0], hg_lb_logits, hg_norm_w[0], cmp_k_pos[0], cmp_k_w1[0], cmp_k_b1[0],
               cmp_k_w2[0], cmp_v_pos[0], cmp_v_w1[0], cmp_v_b1[0], cmp_v_w2[0], w_branch_hg[0],
               w_branch_nsa[0], w_out[0], final_norm_w)
    return y[None]
```

```python
import functools

import numpy as np
import jax
import jax.numpy as jnp
from jax import lax
from jax.experimental import pallas as pl
from jax.experimental.pallas import tpu as pltpu

F32 = jnp.float32
BF16 = jnp.bfloat16

D_MODEL = 2048
HG_HEADS = 8
HG_DK = 128
HG_DV = 128
HG_FDIM = HG_HEADS * HG_DK
HG_WIDTH = HG_HEADS * HG_DV
NSA_HEADS = 16
NSA_KV = 4
NSA_HG = NSA_HEADS // NSA_KV
HEAD_DIM = 128
NSA_WIDTH = NSA_HEADS * HEAD_DIM
NSA_KVW = NSA_KV * HEAD_DIM
CMP_LEN = 32
CMP_STRIDE = 16
CMP_HIDDEN = 512
SEL_LEN = 64
SEL_TOPK = 16
WINDOW = 512
Q_BLOCK = 256
ROPE_THETA = 500000.0
ROT_DIM = HEAD_DIM // 4
ROT_HALF = ROT_DIM // 2
EPS = 1e-6
LOG2E = float(np.log2(np.e))

LANES = 128
NEG_BIG = -1e30
FORCED_SCORE = 1e30
VMEM_LIMIT = 56 * 1024 * 1024

OFF_HG = 0
OFF_Q = 4 * HG_FDIM
OFF_KV = OFF_Q + NSA_WIDTH
OFF_MG = OFF_KV + 6 * NSA_KVW
OFF_Z = OFF_MG + 2 * D_MODEL
OFF_G = OFF_Z + NSA_WIDTH
PROJ_COLS = OFF_G + NSA_KV * LANES
IN_GATE_OFF = OFF_MG
IN_Z_OFF = IN_GATE_OFF + 3 * NSA_HEADS
IN_MG_OFF = IN_Z_OFF + NSA_WIDTH


def _dot(a, b):
    return jnp.dot(a, b, preferred_element_type=F32)


def _dot_nt(a, b):
    return lax.dot_general(a, b, (((1,), (1,)), ((), ())), preferred_element_type=F32)


def _halves(dot, a, b):
    h = a.shape[0] // 2
    return jnp.concatenate([dot(a[:h], b), dot(a[h:], b)], axis=0)


def _split_bf16(a):
    hi = a.astype(BF16)
    lo = (a - hi.astype(F32)).astype(BF16)
    return hi, lo


def _cparams(sem):
    return pltpu.CompilerParams(dimension_semantics=sem, vmem_limit_bytes=VMEM_LIMIT)


def _norm_proj_kernel(x_ref, nw_ref, wa_ref, wb_ref, o_ref, xn_ref, *, n_main):
    j = pl.program_id(1)

    @pl.when(j == 0)
    def _():
        x = x_ref[...]
        ms = jnp.mean(x * x, axis=-1, keepdims=True)
        xn_ref[...] = (x * lax.rsqrt(ms + EPS) * nw_ref[...]).astype(BF16)

    @pl.when(j < n_main)
    def _():
        o_ref[...] = _dot(xn_ref[...], wa_ref[...].astype(BF16))

    @pl.when(j >= n_main)
    def _():
        o_ref[...] = _dot(xn_ref[...], wb_ref[...])


def _norm_proj(x2, norm_w, w_in, w_tail, tm=1024, tn=512):
    S, D = x2.shape
    n_main = OFF_MG // tn
    n_tail = w_tail.shape[1] // tn
    assert OFF_MG % tn == 0 and w_tail.shape[1] % tn == 0
    tm = min(tm, S)
    return pl.pallas_call(
        functools.partial(_norm_proj_kernel, n_main=n_main),
        grid=(S // tm, n_main + n_tail),
        in_specs=[
            pl.BlockSpec((tm, D), lambda i, j: (i, 0)),
            pl.BlockSpec((1, D), lambda i, j: (0, 0)),
            pl.BlockSpec((D, tn), lambda i, j: (0, jnp.minimum(j, n_main - 1))),
            pl.BlockSpec((D, tn), lambda i, j: (0, jnp.maximum(j - n_main, 0))),
        ],
        out_specs=pl.BlockSpec((tm, tn), lambda i, j: (i, j)),
        out_shape=jax.ShapeDtypeStruct((S, (n_main + n_tail) * tn), F32),
        scratch_shapes=[pltpu.VMEM((tm, D), BF16)],
        compiler_params=_cparams(("arbitrary", "arbitrary")),
        name="norm_proj",
    )(x2, norm_w.reshape(1, D), w_in, w_tail)


def _rope(x, cosf, sinf):
    lane = lax.broadcasted_iota(jnp.int32, x.shape, 1)
    rot = jnp.where(lane < ROT_HALF, pltpu.roll(x, LANES - ROT_HALF, 1), pltpu.roll(x, ROT_HALF, 1))
    return x * cosf + rot * sinf


def _rope_tables(S):
    pos = np.arange(S, dtype=np.float64)
    inv = ROPE_THETA ** (-np.arange(0, ROT_DIM, 2, dtype=np.float64) / ROT_DIM)
    ang = pos[:, None] * inv[None, :]
    cos, sin = np.cos(ang), np.sin(ang)
    rest = LANES - ROT_DIM
    cosf = np.concatenate([cos, cos, np.ones((S, rest))], axis=1)
    sinf = np.concatenate([-sin, sin, np.zeros((S, rest))], axis=1)
    return jnp.asarray(cosf, F32), jnp.asarray(sinf, F32)


def _kv_prep_kernel(kv_ref, cos_ref, sin_ref, cmp_ref, kvb_ref, row_ref):
    cosf = cos_ref[...]
    sinf = sin_ref[...]
    W = NSA_KVW
    n_seg = row_ref.shape[0] // CMP_STRIDE

    def to_segments(a, g, rows):
        row_ref[...] = rows
        for l in range(CMP_STRIDE):
            cmp_ref[a, g, :, l * HEAD_DIM:(l + 1) * HEAD_DIM] = row_ref[pl.ds(l, n_seg, stride=CMP_STRIDE), :]

    for g in range(NSA_KV):
        kc = kv_ref[:, 0 * W + g * LANES:0 * W + (g + 1) * LANES]
        to_segments(0, g, _rope(kc, cosf, sinf))
        to_segments(1, g, kv_ref[:, 1 * W + g * LANES:1 * W + (g + 1) * LANES])
        ks = kv_ref[:, 2 * W + g * LANES:2 * W + (g + 1) * LANES]
        kvb_ref[:, 0 * W + g * LANES:0 * W + (g + 1) * LANES] = _rope(ks, cosf, sinf).astype(BF16)
        kvb_ref[:, 1 * W + g * LANES:1 * W + (g + 1) * LANES] = kv_ref[:, 3 * W + g * LANES:3 * W + (g + 1) * LANES].astype(BF16)
        kw = kv_ref[:, 4 * W + g * LANES:4 * W + (g + 1) * LANES]
        kvb_ref[:, 2 * W + g * LANES:2 * W + (g + 1) * LANES] = _rope(kw, cosf, sinf).astype(BF16)
        kvb_ref[:, 3 * W + g * LANES:3 * W + (g + 1) * LANES] = kv_ref[:, 5 * W + g * LANES:5 * W + (g + 1) * LANES].astype(BF16)


def _kv_prep(proj, cosf, sinf, tm=512):
    S = proj.shape[0]
    tm = min(tm, S)
    kvw = 6 * NSA_KVW
    return pl.pallas_call(
        _kv_prep_kernel,
        grid=(S // tm,),
        in_specs=[
            pl.BlockSpec((tm, kvw), lambda i: (i, OFF_KV // kvw)),
            pl.BlockSpec((tm, LANES), lambda i: (i, 0)),
            pl.BlockSpec((tm, LANES), lambda i: (i, 0)),
        ],
        out_specs=[
            pl.BlockSpec((2, NSA_KV, tm // CMP_STRIDE, CMP_STRIDE * HEAD_DIM), lambda i: (0, 0, i, 0)),
            pl.BlockSpec((tm, 4 * NSA_KVW), lambda i: (i, 0)),
        ],
        out_shape=[
            jax.ShapeDtypeStruct((2, NSA_KV, S // CMP_STRIDE, CMP_STRIDE * HEAD_DIM), F32),
            jax.ShapeDtypeStruct((S, 4 * NSA_KVW), BF16),
        ],
        scratch_shapes=[pltpu.VMEM((tm, HEAD_DIM), F32)],
        compiler_params=_cparams(("arbitrary",)),
        name="kv_prep",
    )(proj, cosf, sinf)


def _compress_kernel(seg_ref, pe_ref, w1_ref, b1_ref, w2_ref, o_ref):
    half = (CMP_LEN // 2) * HEAD_DIM
    seg = seg_ref[0, 0]
    n_seg = seg.shape[0]
    pe = pe_ref[0]
    a = (seg + pe[:, :half]).astype(BF16)
    b = (seg + pe[:, half:]).astype(BF16)
    u = _dot(a, w1_ref[0, :half, :])
    v = _dot(b, w1_ref[0, half:, :])
    v_next = pltpu.roll(v, n_seg - 1, 0)
    pre = u + v_next + b1_ref[0]
    h = 0.5 * pre * (1.0 + jnp.tanh(np.sqrt(2.0 / np.pi).astype(np.float32) * (pre + 0.044715 * (pre * pre * pre))))
    o_ref[0, 0] = _dot(h.astype(BF16), w2_ref[0])


def _compress(segs, pe, w1, b1, w2):
    _, G, n_seg, segw = segs.shape
    return pl.pallas_call(
        _compress_kernel,
        grid=(2, G),
        in_specs=[
            pl.BlockSpec((1, 1, n_seg, segw), lambda a, g: (a, g, 0, 0)),
            pl.BlockSpec((1, 1, 2 * segw), lambda a, g: (a, 0, 0)),
            pl.BlockSpec((1, 2 * segw, CMP_HIDDEN), lambda a, g: (a, 0, 0)),
            pl.BlockSpec((1, 1, CMP_HIDDEN), lambda a, g: (a, 0, 0)),
            pl.BlockSpec((1, CMP_HIDDEN, HEAD_DIM), lambda a, g: (a, 0, 0)),
        ],
        out_specs=pl.BlockSpec((1, 1, n_seg, HEAD_DIM), lambda a, g: (a, g, 0, 0)),
        out_shape=jax.ShapeDtypeStruct((2, G, n_seg, HEAD_DIM), F32),
        compiler_params=_cparams(("arbitrary", "arbitrary")),
        name="compress",
    )(segs, pe, w1, b1, w2)


def _hgrn_consts(ch):
    nl = int(np.log2(ch))
    assert 1 << nl == ch
    t = np.arange(ch)[:, None]
    r = np.arange(ch)[None, :]
    mats = [r <= t]
    masks = [np.eye(ch, dtype=bool)]
    for l in range(nl):
        half = 1 << l
        blk = t // (2 * half)
        ref = blk * 2 * half + half - 1
        up = ((t >> l) & 1) == 1
        mats.append(np.where(up, (r > ref) & (r <= t), (r > t) & (r <= ref)))
        masks.append(up & (((r >> l) & 1) == 0) & (blk == r // (2 * half)))
    mats.append(r > t)
    mc = np.concatenate(mats, axis=0).astype(np.float32)
    lm = np.stack(masks, axis=0).astype(np.float32)
    return nl, mc, lm


def _hgrn_kernel(q_ref, f_ref, i_ref, z_ref, lbl_ref, nw_ref, mc_ref, lm_ref, o_ref, st_ref, e_ref, *, ch, nl):
    @pl.when(pl.program_id(0) == 0)
    def _():
        st_ref[...] = jnp.zeros_like(st_ref)

    lg = lbl_ref[...]
    ex = jnp.exp(lg - jnp.max(lg, axis=0, keepdims=True))
    lb = ex[0:1] / jnp.sum(ex, axis=0, keepdims=True)
    f = lb + (1.0 - lb) * jax.nn.sigmoid(f_ref[...])
    g_hi, g_lo = _split_bf16(jnp.log(f))
    mc = mc_ref[...]
    e_ref[...] = _dot(mc, g_hi) + _dot(mc, g_lo)
    row = lax.broadcasted_iota(jnp.int32, (ch, 1), 0)
    nw = nw_ref[...]

    for h in range(HG_HEADS):
        sl = slice(h * HG_DK, (h + 1) * HG_DK)
        q = q_ref[:, sl]
        k = 1.0 - f[:, sl]
        v = i_ref[:, sl]
        v_b = v.astype(BF16)
        b = e_ref[0:ch, sl]
        scores = _dot_nt(q.astype(BF16), k.astype(BF16)) * lm_ref[0]
        for l in range(nl):
            el = e_ref[(l + 1) * ch:(l + 2) * ch, sl]
            up = ((row >> l) & 1) == 1
            xl = (jnp.where(up, q, k) * jnp.exp(el)).astype(BF16)
            scores = scores + _dot_nt(xl, xl) * lm_ref[l + 1]
        intra = _dot(scores.astype(BF16), v_b)
        st = st_ref[h]
        inter = _dot_nt((q * jnp.exp(b)).astype(BF16), st.astype(BF16))
        ke = (k * jnp.exp(e_ref[(nl + 1) * ch:(nl + 2) * ch, sl])).astype(BF16)
        upd = lax.dot_general(v_b, ke, (((0,), (0,)), ((), ())), preferred_element_type=F32)
        st_ref[h] = st * jnp.exp(b[ch - 1:ch, :]) + upd
        o = inter + intra
        ms = jnp.mean(o * o, axis=-1, keepdims=True)
        z = z_ref[:, sl]
        o_ref[:, sl] = (o * lax.rsqrt(ms + EPS) * nw * (z * jax.nn.sigmoid(z))).astype(o_ref.dtype)


def _hgrn(proj, lb_logits, hg_norm_w, ch=128):
    S = proj.shape[0]
    nl, mc, lm = _hgrn_consts(ch)
    kern = functools.partial(_hgrn_kernel, ch=ch, nl=nl)
    nlb = lb_logits.shape[0]
    return pl.pallas_call(
        kern,
        grid=(S // ch,),
        in_specs=[
            pl.BlockSpec((ch, HG_FDIM), lambda c: (c, 0)),
            pl.BlockSpec((ch, HG_FDIM), lambda c: (c, 1)),
            pl.BlockSpec((ch, HG_WIDTH), lambda c: (c, 2)),
            pl.BlockSpec((ch, HG_WIDTH), lambda c: (c, 3)),
            pl.BlockSpec((nlb, HG_FDIM), lambda c: (0, 0)),
            pl.BlockSpec((1, HG_DV), lambda c: (0, 0)),
            pl.BlockSpec(mc.shape, lambda c: (0, 0)),
            pl.BlockSpec(lm.shape, lambda c: (0, 0, 0)),
        ],
        out_specs=pl.BlockSpec((ch, HG_WIDTH), lambda c: (c, 0)),
        out_shape=jax.ShapeDtypeStruct((S, HG_WIDTH), BF16),
        scratch_shapes=[
            pltpu.VMEM((HG_HEADS, HG_DV, HG_DK), F32),
            pltpu.VMEM(((nl + 2) * ch, HG_FDIM), F32),
        ],
        compiler_params=_cparams(("arbitrary",)),
        name="hgrn",
    )(proj, proj, proj, proj, lb_logits, hg_norm_w.reshape(1, HG_DV), jnp.asarray(mc, BF16), jnp.asarray(lm, F32))


def _nsa_kernel(q_ref, cos_ref, sin_ref, gate_ref, z_ref, kc_ref, vc_ref, ov_ref, ca_ref, wb_ref, cb_ref,
                ks_ref, vs_ref, kw_ref, vw_ref, o_ref, m_ref, accl_ref, sa_ref, sb_ref, part_ref, *, tk):
    n = pl.program_id(1)
    q0 = n * Q_BLOCK
    R = NSA_HG * Q_BLOCK
    qscale = (HEAD_DIM ** -0.5) * LOG2E
    cosf = cos_ref[...]
    sinf = sin_ref[...]
    q4 = jnp.concatenate(
        [(_rope(q_ref[:, h * HEAD_DIM:(h + 1) * HEAD_DIM], cosf, sinf) * qscale).astype(BF16) for h in range(NSA_HG)],
        axis=0)

    def per_head(a):
        return jnp.concatenate([a] * NSA_HG, axis=0)

    def tpos(shape):
        return q0 + (lax.broadcasted_iota(jnp.int32, shape, 0) & (Q_BLOCK - 1))

    ncp = kc_ref.shape[2]
    nb = Q_BLOCK // CMP_STRIDE
    c_first = n * nb - (CMP_LEN // CMP_STRIDE - 1)
    crow = lax.broadcasted_iota(jnp.int32, (ncp, LANES), 0)
    clane = lax.broadcasted_iota(jnp.int32, (ncp, LANES), 1)
    flags = jnp.where(clane == 0, jnp.where(crow >= c_first + nb, NEG_BIG, 0.0),
                      jnp.where(crow == c_first + clane - 1, 1.0, 0.0)).astype(BF16)
    kca = jnp.concatenate([kc_ref[0, 0].astype(BF16), flags], axis=1)
    s = _halves(_dot_nt, jnp.concatenate([q4, per_head(ca_ref[...])], axis=1), kca)
    e = jnp.exp2(s - jnp.max(s, axis=-1, keepdims=True))
    vca = jnp.concatenate([vc_ref[0, 0].astype(BF16), jnp.ones((ncp, LANES), BF16)], axis=1)
    pv = _halves(_dot, e.astype(BF16), vca)
    any_visible = jnp.where(tpos((R, LANES)) >= CMP_LEN - 1, 1.0, 0.0)
    inv = any_visible / pv[:, LANES:]
    o_cmp = pv[:, :LANES] * inv
    p = e * jnp.concatenate([inv] * (ncp // LANES), axis=1)
    p_sum = p[0:Q_BLOCK]
    for h in range(1, NSA_HG):
        p_sum = p_sum + p[h * Q_BLOCK:(h + 1) * Q_BLOCK]
    ov = ov_ref[...]
    ps_hi, ps_lo = _split_bf16(p_sum)
    imp = _dot(ps_hi, ov) + _dot(ps_lo, ov)

    wspan = WINDOW + Q_BLOCK
    w0 = pl.multiple_of(jnp.maximum(q0 - WINDOW, 0), Q_BLOCK)
    sw = (_halves(_dot_nt, q4, kw_ref[pl.ds(w0, wspan), :])
          + per_head(wb_ref[jnp.minimum(n, WINDOW // Q_BLOCK)]))
    ew = jnp.exp2(sw - jnp.max(sw, axis=-1, keepdims=True))
    vwa = jnp.concatenate([vw_ref[pl.ds(w0, wspan), :], jnp.ones((wspan, LANES), BF16)], axis=1)
    pvw = _halves(_dot, ew.astype(BF16), vwa)
    o_win = pvw[:, :LANES] / pvw[:, LANES:]

    gate = jax.nn.sigmoid(gate_ref[...])
    for h in range(NSA_HG):
        rs = slice(h * Q_BLOCK, (h + 1) * Q_BLOCK)
        part_ref[:, h * HEAD_DIM:(h + 1) * HEAD_DIM] = (
            gate[:, 3 * h:3 * h + 1] * o_cmp[rs] + gate[:, 3 * h + 2:3 * h + 3] * o_win[rs])

    imp_t = imp.T
    nsel_pad = imp_t.shape[0]
    jj = lax.broadcasted_iota(jnp.int32, (nsel_pad, Q_BLOCK), 0)
    tq = q0 + lax.broadcasted_iota(jnp.int32, (nsel_pad, Q_BLOCK), 1)
    jt = tq >> 6
    forced = (jj == 0) | (jj == jt) | (jj == jt - 1)
    jjf = jj.astype(F32)
    score = jnp.where(jj * SEL_LEN <= tq, jnp.where(forced, FORCED_SCORE, imp_t), -1.0)
    sel = jnp.zeros((nsel_pad, Q_BLOCK), F32)
    for _ in range(SEL_TOPK):
        mx = jnp.max(score, axis=0, keepdims=True)
        first = jnp.min(jnp.where(score == mx, jjf, float(nsel_pad)), axis=0, keepdims=True)
        hit = jjf == first
        sel = jnp.where(hit, jnp.where(mx >= 0.0, 1.0, 0.0), sel)
        score = jnp.where(hit, -2.0, score)
    sel_bias = per_head(((sel.T - 1.0) * (-NEG_BIG)).astype(BF16))

    m_ref[...] = jnp.full(m_ref.shape, NEG_BIG, F32)
    accl_ref[...] = jnp.zeros(accl_ref.shape, F32)
    qa = jnp.concatenate([q4, sel_bias], axis=1)
    key_blk = lax.broadcasted_iota(jnp.int32, (tk, LANES), 0) >> 6
    blk_lane = lax.broadcasted_iota(jnp.int32, (tk, LANES), 1)
    ones_k = jnp.ones((tk, LANES), BF16)
    rep = tk // LANES

    n_q = tk // Q_BLOCK
    last = lax.shift_right_logical(n, n_q.bit_length() - 1)
    diag = n - last * n_q

    def score_tile(kt, dst_ref):
        k0 = pl.multiple_of(kt * tk, tk)
        onehot = jnp.where(key_blk + kt * (tk // SEL_LEN) == blk_lane, 1.0, 0.0).astype(BF16)
        dst_ref[...] = _dot_nt(qa, jnp.concatenate([ks_ref[pl.ds(k0, tk), :], onehot], axis=1))

    def consume_tile(kt, src_ref):
        k0 = pl.multiple_of(kt * tk, tk)
        sc = src_ref[...] + per_head(cb_ref[jnp.where(kt == last, diag, n_q)])
        m_prev = m_ref[...]
        m_next = jnp.maximum(m_prev, jnp.max(sc, axis=-1, keepdims=True))
        pr = jnp.exp2(sc - jnp.concatenate([m_next] * rep, axis=1))
        alpha = jnp.exp2(m_prev - m_next)
        va = jnp.concatenate([vs_ref[pl.ds(k0, tk), :], ones_k], axis=1)
        accl_ref[...] = jnp.concatenate([alpha, alpha], axis=1) * accl_ref[...] + _dot(pr.astype(BF16), va)
        m_ref[...] = m_next

    score_tile(0, sa_ref)

    def slc_pair(i, carry):
        score_tile(2 * i + 1, sb_ref)
        consume_tile(2 * i, sa_ref)
        score_tile(2 * i + 2, sa_ref)
        consume_tile(2 * i + 1, sb_ref)
        return carry

    lax.fori_loop(0, lax.shift_right_logical(last, 1), slc_pair, 0)

    @pl.when((last & 1) == 0)
    def _():
        consume_tile(last, sa_ref)

    @pl.when((last & 1) == 1)
    def _():
        score_tile(last, sb_ref)
        consume_tile(last - 1, sa_ref)
        consume_tile(last, sb_ref)

    o_slc = accl_ref[:, :LANES] / accl_ref[:, LANES:]

    gsig = jax.nn.sigmoid(gate_ref[...])
    for h in range(NSA_HG):
        rs = slice(h * Q_BLOCK, (h + 1) * Q_BLOCK)
        cs = slice(h * HEAD_DIM, (h + 1) * HEAD_DIM)
        oh = part_ref[:, cs] + gsig[:, 3 * h + 1:3 * h + 2] * o_slc[rs]
        z = z_ref[:, cs]
        o_ref[:, cs] = (oh * (z * jax.nn.sigmoid(z))).astype(o_ref.dtype)


def _cmp_staircase():
    nb = Q_BLOCK // CMP_STRIDE
    a = np.zeros((Q_BLOCK, LANES), np.float32)
    a[:, 0] = 1.0
    r = np.arange(Q_BLOCK)[:, None]
    i = np.arange(nb)[None, :]
    a[:, 1:1 + nb] = np.where(r < CMP_STRIDE * i + (CMP_STRIDE - 1), NEG_BIG, 0.0)
    return a


def _window_masks():
    n_w = WINDOW // Q_BLOCK
    r = np.arange(Q_BLOCK)[None, :, None]
    c = np.arange(WINDOW + Q_BLOCK)[None, None, :]
    d = np.arange(n_w + 1)[:, None, None]
    ok = np.where(d < n_w, c <= Q_BLOCK * d + r, (c > r) & (c <= WINDOW + r))
    return np.where(ok, 0.0, NEG_BIG).astype(np.float32)


def _causal_staircases(tk):
    n_q = tk // Q_BLOCK
    r = np.arange(Q_BLOCK)[None, :, None]
    c = np.arange(tk)[None, None, :]
    d = np.arange(n_q + 1)[:, None, None]
    return np.where((c <= Q_BLOCK * d + r) | (d == n_q), 0.0, NEG_BIG).astype(np.float32)


def _overlap_matrix(ncp, nsel_pad, nsel):
    ci = np.arange(ncp)[:, None] * CMP_STRIDE
    sj = np.arange(nsel_pad)[None, :] * SEL_LEN
    ov = (ci < sj + SEL_LEN) & (ci + CMP_LEN > sj) & (np.arange(nsel_pad)[None, :] < nsel) & (np.arange(ncp)[:, None] < ncp - 1)
    return ov.astype(np.float32)


def _nsa(proj, cosf, sinf, kcvc, kvb, tk=512):
    S = proj.shape[0]
    ncp = kcvc.shape[2]
    nsel = S // SEL_LEN
    assert nsel <= LANES and S % tk == 0 and S >= WINDOW + Q_BLOCK
    assert CMP_LEN == 2 * CMP_STRIDE and Q_BLOCK % CMP_STRIDE == 0 and WINDOW % Q_BLOCK == 0
    ov = jnp.asarray(_overlap_matrix(ncp, LANES, nsel), BF16)
    ca = jnp.asarray(_cmp_staircase(), BF16)
    wb = jnp.asarray(_window_masks(), F32)
    n_q = tk // Q_BLOCK
    assert n_q & (n_q - 1) == 0 and tk % SEL_LEN == 0
    cb = jnp.asarray(_causal_staircases(tk), F32)
    qw = NSA_HG * HEAD_DIM
    R = NSA_HG * Q_BLOCK
    kern = functools.partial(_nsa_kernel, tk=tk)
    return pl.pallas_call(
        kern,
        grid=(NSA_KV, S // Q_BLOCK),
        in_specs=[
            pl.BlockSpec((Q_BLOCK, qw), lambda g, n: (n, OFF_Q // qw + g)),
            pl.BlockSpec((Q_BLOCK, LANES), lambda g, n: (n, 0)),
            pl.BlockSpec((Q_BLOCK, LANES), lambda g, n: (n, 0)),
            pl.BlockSpec((Q_BLOCK, LANES), lambda g, n: (n, OFF_G // LANES + g)),
            pl.BlockSpec((Q_BLOCK, qw), lambda g, n: (n, OFF_Z // qw + g)),
            pl.BlockSpec((1, 1, ncp, HEAD_DIM), lambda g, n: (0, g, 0, 0)),
            pl.BlockSpec((1, 1, ncp, HEAD_DIM), lambda g, n: (1, g, 0, 0)),
            pl.BlockSpec((ncp, LANES), lambda g, n: (0, 0)),
            pl.BlockSpec((Q_BLOCK, LANES), lambda g, n: (0, 0)),
            pl.BlockSpec((WINDOW // Q_BLOCK + 1, Q_BLOCK, WINDOW + Q_BLOCK), lambda g, n: (0, 0, 0)),
            pl.BlockSpec((n_q + 1, Q_BLOCK, tk), lambda g, n: (0, 0, 0)),
            pl.BlockSpec((S, HEAD_DIM), lambda g, n: (0, 0 * NSA_KV + g)),
            pl.BlockSpec((S, HEAD_DIM), lambda g, n: (0, 1 * NSA_KV + g)),
            pl.BlockSpec((S, HEAD_DIM), lambda g, n: (0, 2 * NSA_KV + g)),
            pl.BlockSpec((S, HEAD_DIM), lambda g, n: (0, 3 * NSA_KV + g)),
        ],
        out_specs=pl.BlockSpec((Q_BLOCK, qw), lambda g, n: (n, g)),
        out_shape=jax.ShapeDtypeStruct((S, NSA_WIDTH), BF16),
        scratch_shapes=[
            pltpu.VMEM((R, LANES), F32),
            pltpu.VMEM((R, HEAD_DIM + LANES), F32),
            pltpu.VMEM((R, tk), F32),
            pltpu.VMEM((R, tk), F32),
            pltpu.VMEM((Q_BLOCK, qw), F32),
        ],
        compiler_params=_cparams(("arbitrary", "arbitrary")),
        name="nsa",
    )(proj, cosf, sinf, proj, proj, kcvc, kcvc, ov, ca, wb, cb, kvb, kvb, kvb, kvb)


def _merge_kernel(ohg_ref, onsa_ref, g0_ref, g1_ref, g2_ref, g3_ref, whg_ref, wnsa_ref, o_ref):
    y_hg = _dot(ohg_ref[...], whg_ref[...])
    y_nsa = _dot(onsa_ref[...], wnsa_ref[...])
    half = D_MODEL // 2
    gh = (g0_ref, g1_ref)
    gn = (g2_ref, g3_ref)
    for c in range(2):
        cs = slice(c * half, (c + 1) * half)
        m = jax.nn.sigmoid(gh[c][...]) * y_hg[:, cs] + jax.nn.sigmoid(gn[c][...]) * y_nsa[:, cs]
        o_ref[:, cs] = m.astype(o_ref.dtype)


def _merge(o_hg, o_nsa, proj, w_hg, w_nsa, tm=256):
    S = o_hg.shape[0]
    tm = min(tm, S)
    half = D_MODEL // 2
    gb = OFF_MG // half
    return pl.pallas_call(
        _merge_kernel,
        grid=(S // tm,),
        in_specs=[
            pl.BlockSpec((tm, HG_WIDTH), lambda i: (i, 0)),
            pl.BlockSpec((tm, NSA_WIDTH), lambda i: (i, 0)),
            pl.BlockSpec((tm, half), lambda i: (i, gb + 0)),
            pl.BlockSpec((tm, half), lambda i: (i, gb + 1)),
            pl.BlockSpec((tm, half), lambda i: (i, gb + 2)),
            pl.BlockSpec((tm, half), lambda i: (i, gb + 3)),
            pl.BlockSpec((HG_WIDTH, D_MODEL), lambda i: (0, 0)),
            pl.BlockSpec((NSA_WIDTH, D_MODEL), lambda i: (0, 0)),
        ],
        out_specs=pl.BlockSpec((tm, D_MODEL), lambda i: (i, 0)),
        out_shape=jax.ShapeDtypeStruct((S, D_MODEL), BF16),
        compiler_params=_cparams(("arbitrary",)),
        name="merge",
    )(o_hg, o_nsa, proj, proj, proj, proj, w_hg, w_nsa)


def _out_kernel(x_ref, m_ref, w_ref, nw_ref, o_ref):
    h = x_ref[...] + _dot(m_ref[...], w_ref[...])
    ms = jnp.mean(h * h, axis=-1, keepdims=True)
    o_ref[...] = h * lax.rsqrt(ms + EPS) * nw_ref[...]


def _out(x2, merged, w_out, final_w, tm=256):
    S, D = x2.shape
    tm = min(tm, S)
    return pl.pallas_call(
        _out_kernel,
        grid=(S // tm,),
        in_specs=[
            pl.BlockSpec((tm, D), lambda i: (i, 0)),
            pl.BlockSpec((tm, D), lambda i: (i, 0)),
            pl.BlockSpec((D, D), lambda i: (0, 0)),
            pl.BlockSpec((1, D), lambda i: (0, 0)),
        ],
        out_specs=pl.BlockSpec((tm, D), lambda i: (i, 0)),
        out_shape=jax.ShapeDtypeStruct((S, D), F32),
        compiler_params=_cparams(("arbitrary",)),
        name="out_proj",
    )(x2, merged, w_out, final_w.reshape(1, D))


def _layout_w_tail(w):
    gates = w[:, IN_GATE_OFF:IN_Z_OFF].reshape(D_MODEL, NSA_KV, 3 * NSA_HG)
    gates = jnp.pad(gates, ((0, 0), (0, 0), (0, LANES - 3 * NSA_HG))).reshape(D_MODEL, NSA_KV * LANES)
    return jnp.concatenate(
        [w[:, IN_MG_OFF:IN_MG_OFF + 2 * D_MODEL], w[:, IN_Z_OFF:IN_MG_OFF], gates], axis=1).astype(BF16)


def _layer(x2, norm_w, w_in, lb_logits, hg_norm_w, cmp_k_pos, cmp_k_w1, cmp_k_b1, cmp_k_w2,
           cmp_v_pos, cmp_v_w1, cmp_v_b1, cmp_v_w2, w_branch_hg, w_branch_nsa, w_out, final_w):
    S = x2.shape[0]
    proj = _norm_proj(x2, norm_w, w_in, _layout_w_tail(w_in))
    cosf, sinf = _rope_tables(S)
    segs, kvb = _kv_prep(proj, cosf, sinf)
    pe = jnp.stack([cmp_k_pos.reshape(1, -1), cmp_v_pos.reshape(1, -1)])
    w1 = jnp.stack([cmp_k_w1, cmp_v_w1]).astype(BF16)
    b1 = jnp.stack([cmp_k_b1.reshape(1, -1), cmp_v_b1.reshape(1, -1)])
    w2 = jnp.stack([cmp_k_w2, cmp_v_w2]).astype(BF16)
    kcvc = _compress(segs, pe, w1, b1, w2)
    o_hg = _hgrn(proj, lb_logits, hg_norm_w)
    o_nsa = _nsa(proj, cosf, sinf, kcvc, kvb)
    merged = _merge(o_hg, o_nsa, proj, w_branch_hg.astype(BF16), w_branch_nsa.astype(BF16))
    return _out(x2, merged, w_out.astype(BF16), final_w)


def kernel(x, norm_w, w_in, hg_lb_logits, hg_norm_w, cmp_k_pos, cmp_k_w1, cmp_k_b1, cmp_k_w2, cmp_v_pos, cmp_v_w1, cmp_v_b1, cmp_v_w2, w_branch_hg, w_branch_nsa, w_out, final_norm_w):
    B, S, D = x.shape
    assert B == 1 and D == D_MODEL and norm_w.shape[0] == 1
    y = _layer(x[0], norm_w[0], w_in[0], hg_lb_logits, hg_norm_w[0], cmp_k_pos[0], cmp_k_w1[0], cmp_k_b1[0],
               cmp_k_w2[0], cmp_v_pos[0], cmp_v_w1[0], cmp_v_b1[0], cmp_v_w2[0], w_branch_hg[0],
               w_branch_nsa[0], w_out[0], final_norm_w)
    return y[None]
```

```python
import functools

import numpy as np
import jax
import jax.numpy as jnp
from jax import lax
from jax.experimental import pallas as pl
from jax.experimental.pallas import tpu as pltpu

F32 = jnp.float32
BF16 = jnp.bfloat16

D_MODEL = 2048
HG_HEADS = 8
HG_DK = 128
HG_DV = 128
HG_FDIM = HG_HEADS * HG_DK
HG_WIDTH = HG_HEADS * HG_DV
NSA_HEADS = 16
NSA_KV = 4
NSA_HG = NSA_HEADS // NSA_KV
HEAD_DIM = 128
NSA_WIDTH = NSA_HEADS * HEAD_DIM
NSA_KVW = NSA_KV * HEAD_DIM
CMP_LEN = 32
CMP_STRIDE = 16
CMP_HIDDEN = 512
SEL_LEN = 64
SEL_TOPK = 16
WINDOW = 512
Q_BLOCK = 256
ROPE_THETA = 500000.0
ROT_DIM = HEAD_DIM // 4
ROT_HALF = ROT_DIM // 2
EPS = 1e-6
LOG2E = float(np.log2(np.e))

LANES = 128
NEG_BIG = -1e30
FORCED_SCORE = 1e30
VMEM_LIMIT = 56 * 1024 * 1024

OFF_HG = 0
OFF_Q = 4 * HG_FDIM
OFF_KV = OFF_Q + NSA_WIDTH
OFF_MG = OFF_KV + 6 * NSA_KVW
OFF_Z = OFF_MG + 2 * D_MODEL
OFF_G = OFF_Z + NSA_WIDTH
PROJ_COLS = OFF_G + NSA_KV * LANES
IN_GATE_OFF = OFF_MG
IN_Z_OFF = IN_GATE_OFF + 3 * NSA_HEADS
IN_MG_OFF = IN_Z_OFF + NSA_WIDTH


def _dot(a, b):
    return jnp.dot(a, b, preferred_element_type=F32)


def _dot_nt(a, b):
    return lax.dot_general(a, b, (((1,), (1,)), ((), ())), preferred_element_type=F32)


def _halves(dot, a, b):
    h = a.shape[0] // 2
    return jnp.concatenate([dot(a[:h], b), dot(a[h:], b)], axis=0)


def _split_bf16(a):
    hi = a.astype(BF16)
    lo = (a - hi.astype(F32)).astype(BF16)
    return hi, lo


def _cparams(sem):
    return pltpu.CompilerParams(dimension_semantics=sem, vmem_limit_bytes=VMEM_LIMIT)


def _w_prep_kernel(wt_ref, wg_ref, o_ref, *, n_direct):
    j = pl.program_id(0)

    @pl.when(j < n_direct)
    def _():
        o_ref[...] = wt_ref[...].astype(BF16)

    @pl.when(j >= n_direct)
    def _():
        o_ref[...] = wg_ref[...].astype(BF16)


def _w_prep(wt, tn=512):
    D = wt.shape[1]
    n_main, n_mg, n_z = OFF_MG // tn, 2 * D_MODEL // tn, NSA_WIDTH // tn
    n_direct = n_main + n_mg + n_z
    assert OFF_MG % tn == 0 and D_MODEL % tn == 0 and NSA_KV * LANES == tn
    gates = wt[IN_GATE_OFF:IN_Z_OFF].reshape(NSA_KV, 3 * NSA_HG, D)
    gates = jnp.pad(gates, ((0, 0), (0, LANES - 3 * NSA_HG), (0, 0))).reshape(NSA_KV * LANES, D)

    sub = 8
    assert IN_MG_OFF % sub == 0 and IN_Z_OFF % sub == 0 and tn % sub == 0

    def src_row(j):
        in_mg = IN_MG_OFF // sub + (j - n_main) * (tn // sub)
        in_z = IN_Z_OFF // sub + (jnp.minimum(j, n_direct - 1) - n_main - n_mg) * (tn // sub)
        return jnp.where(j < n_main, j * (tn // sub), jnp.where(j < n_main + n_mg, in_mg, in_z)) * sub

    return pl.pallas_call(
        functools.partial(_w_prep_kernel, n_direct=n_direct),
        grid=(n_direct + 1,),
        in_specs=[
            pl.BlockSpec((pl.Element(tn), pl.Element(D)), lambda j: (src_row(j), 0)),
            pl.BlockSpec((tn, D), lambda j: (0, 0)),
        ],
        out_specs=pl.BlockSpec((tn, D), lambda j: (j, 0)),
        out_shape=jax.ShapeDtypeStruct(((n_direct + 1) * tn, D), BF16),
        compiler_params=_cparams(("arbitrary",)),
        name="w_prep",
    )(wt, gates)


def _norm_proj_kernel(x_ref, nw_ref, w_ref, o_ref, xn_ref):
    @pl.when(pl.program_id(1) == 0)
    def _():
        x = x_ref[...]
        ms = jnp.mean(x * x, axis=-1, keepdims=True)
        xn_ref[...] = (x * lax.rsqrt(ms + EPS) * nw_ref[...]).astype(BF16)

    o_ref[...] = _dot_nt(xn_ref[...], w_ref[...])


def _norm_proj(x2, norm_w, wtb, tm=1024, tn=512):
    S, D = x2.shape
    N = wtb.shape[0]
    tm = min(tm, S)
    return pl.pallas_call(
        _norm_proj_kernel,
        grid=(S // tm, N // tn),
        in_specs=[
            pl.BlockSpec((tm, D), lambda i, j: (i, 0)),
            pl.BlockSpec((1, D), lambda i, j: (0, 0)),
            pl.BlockSpec((tn, D), lambda i, j: (j, 0)),
        ],
        out_specs=pl.BlockSpec((tm, tn), lambda i, j: (i, j)),
        out_shape=jax.ShapeDtypeStruct((S, N), F32),
        scratch_shapes=[pltpu.VMEM((tm, D), BF16)],
        compiler_params=_cparams(("arbitrary", "arbitrary")),
        name="norm_proj",
    )(x2, norm_w.reshape(1, D), wtb)


def _rope(x, cosf, sinf):
    lane = lax.broadcasted_iota(jnp.int32, x.shape, 1)
    rot = jnp.where(lane < ROT_HALF, pltpu.roll(x, LANES - ROT_HALF, 1), pltpu.roll(x, ROT_HALF, 1))
    return x * cosf + rot * sinf


def _rope_tables(S):
    pos = np.arange(S, dtype=np.float64)
    inv = ROPE_THETA ** (-np.arange(0, ROT_DIM, 2, dtype=np.float64) / ROT_DIM)
    ang = pos[:, None] * inv[None, :]
    cos, sin = np.cos(ang), np.sin(ang)
    rest = LANES - ROT_DIM
    cosf = np.concatenate([cos, cos, np.ones((S, rest))], axis=1)
    sinf = np.concatenate([-sin, sin, np.zeros((S, rest))], axis=1)
    return jnp.asarray(cosf, F32), jnp.asarray(sinf, F32)


def _kv_prep_kernel(kv_ref, cos_ref, sin_ref, cmp_ref, kvb_ref, row_ref):
    cosf = cos_ref[...]
    sinf = sin_ref[...]
    W = NSA_KVW
    n_seg = row_ref.shape[0] // CMP_STRIDE

    def to_segments(a, g, rows):
        row_ref[...] = rows
        for l in range(CMP_STRIDE):
            cmp_ref[a, g, :, l * HEAD_DIM:(l + 1) * HEAD_DIM] = row_ref[pl.ds(l, n_seg, stride=CMP_STRIDE), :]

    for g in range(NSA_KV):
        kc = kv_ref[:, 0 * W + g * LANES:0 * W + (g + 1) * LANES]
        to_segments(0, g, _rope(kc, cosf, sinf))
        to_segments(1, g, kv_ref[:, 1 * W + g * LANES:1 * W + (g + 1) * LANES])
        ks = kv_ref[:, 2 * W + g * LANES:2 * W + (g + 1) * LANES]
        kvb_ref[:, 0 * W + g * LANES:0 * W + (g + 1) * LANES] = _rope(ks, cosf, sinf).astype(BF16)
        kvb_ref[:, 1 * W + g * LANES:1 * W + (g + 1) * LANES] = kv_ref[:, 3 * W + g * LANES:3 * W + (g + 1) * LANES].astype(BF16)
        kw = kv_ref[:, 4 * W + g * LANES:4 * W + (g + 1) * LANES]
        kvb_ref[:, 2 * W + g * LANES:2 * W + (g + 1) * LANES] = _rope(kw, cosf, sinf).astype(BF16)
        kvb_ref[:, 3 * W + g * LANES:3 * W + (g + 1) * LANES] = kv_ref[:, 5 * W + g * LANES:5 * W + (g + 1) * LANES].astype(BF16)


def _kv_prep(proj, cosf, sinf, tm=512):
    S = proj.shape[0]
    tm = min(tm, S)
    kvw = 6 * NSA_KVW
    return pl.pallas_call(
        _kv_prep_kernel,
        grid=(S // tm,),
        in_specs=[
            pl.BlockSpec((tm, kvw), lambda i: (i, OFF_KV // kvw)),
            pl.BlockSpec((tm, LANES), lambda i: (i, 0)),
            pl.BlockSpec((tm, LANES), lambda i: (i, 0)),
        ],
        out_specs=[
            pl.BlockSpec((2, NSA_KV, tm // CMP_STRIDE, CMP_STRIDE * HEAD_DIM), lambda i: (0, 0, i, 0)),
            pl.BlockSpec((tm, 4 * NSA_KVW), lambda i: (i, 0)),
        ],
        out_shape=[
            jax.ShapeDtypeStruct((2, NSA_KV, S // CMP_STRIDE, CMP_STRIDE * HEAD_DIM), F32),
            jax.ShapeDtypeStruct((S, 4 * NSA_KVW), BF16),
        ],
        scratch_shapes=[pltpu.VMEM((tm, HEAD_DIM), F32)],
        compiler_params=_cparams(("arbitrary",)),
        name="kv_prep",
    )(proj, cosf, sinf)


def _compress_kernel(seg_ref, pe_ref, w1_ref, b1_ref, w2_ref, o_ref):
    half = (CMP_LEN // 2) * HEAD_DIM
    seg = seg_ref[0, 0]
    n_seg = seg.shape[0]
    pe = pe_ref[0]
    a = (seg + pe[:, :half]).astype(BF16)
    b = (seg + pe[:, half:]).astype(BF16)
    u = _dot(a, w1_ref[0, :half, :])
    v = _dot(b, w1_ref[0, half:, :])
    v_next = pltpu.roll(v, n_seg - 1, 0)
    pre = u + v_next + b1_ref[0]
    h = 0.5 * pre * (1.0 + jnp.tanh(np.sqrt(2.0 / np.pi).astype(np.float32) * (pre + 0.044715 * (pre * pre * pre))))
    o_ref[0, 0] = _dot(h.astype(BF16), w2_ref[0])


def _compress(segs, pe, w1, b1, w2):
    _, G, n_seg, segw = segs.shape
    return pl.pallas_call(
        _compress_kernel,
        grid=(2, G),
        in_specs=[
            pl.BlockSpec((1, 1, n_seg, segw), lambda a, g: (a, g, 0, 0)),
            pl.BlockSpec((1, 1, 2 * segw), lambda a, g: (a, 0, 0)),
            pl.BlockSpec((1, 2 * segw, CMP_HIDDEN), lambda a, g: (a, 0, 0)),
            pl.BlockSpec((1, 1, CMP_HIDDEN), lambda a, g: (a, 0, 0)),
            pl.BlockSpec((1, CMP_HIDDEN, HEAD_DIM), lambda a, g: (a, 0, 0)),
        ],
        out_specs=pl.BlockSpec((1, 1, n_seg, HEAD_DIM), lambda a, g: (a, g, 0, 0)),
        out_shape=jax.ShapeDtypeStruct((2, G, n_seg, HEAD_DIM), F32),
        compiler_params=_cparams(("arbitrary", "arbitrary")),
        name="compress",
    )(segs, pe, w1, b1, w2)


def _hgrn_consts(ch):
    nl = int(np.log2(ch))
    assert 1 << nl == ch
    t = np.arange(ch)[:, None]
    r = np.arange(ch)[None, :]
    mats = [r <= t]
    masks = [np.eye(ch, dtype=bool)]
    for l in range(nl):
        half = 1 << l
        blk = t // (2 * half)
        ref = blk * 2 * half + half - 1
        up = ((t >> l) & 1) == 1
        mats.append(np.where(up, (r > ref) & (r <= t), (r > t) & (r <= ref)))
        masks.append(up & (((r >> l) & 1) == 0) & (blk == r // (2 * half)))
    mats.append(r > t)
    mc = np.concatenate(mats, axis=0).astype(np.float32)
    lm = np.stack(masks, axis=0).astype(np.float32)
    return nl, mc, lm


def _hgrn_kernel(q_ref, f_ref, i_ref, z_ref, lbl_ref, nw_ref, mc_ref, lm_ref, o_ref, st_ref, e_ref, *, ch, nl):
    @pl.when(pl.program_id(0) == 0)
    def _():
        st_ref[...] = jnp.zeros_like(st_ref)

    lg = lbl_ref[...]
    ex = jnp.exp(lg - jnp.max(lg, axis=0, keepdims=True))
    lb = ex[0:1] / jnp.sum(ex, axis=0, keepdims=True)
    f = lb + (1.0 - lb) * jax.nn.sigmoid(f_ref[...])
    g_hi, g_lo = _split_bf16(jnp.log(f))
    mc = mc_ref[...]
    e_ref[...] = _dot(mc, g_hi) + _dot(mc, g_lo)
    row = lax.broadcasted_iota(jnp.int32, (ch, 1), 0)
    nw = nw_ref[...]

    for h in range(HG_HEADS):
        sl = slice(h * HG_DK, (h + 1) * HG_DK)
        q = q_ref[:, sl]
        k = 1.0 - f[:, sl]
        v = i_ref[:, sl]
        v_b = v.astype(BF16)
        b = e_ref[0:ch, sl]
        scores = _dot_nt(q.astype(BF16), k.astype(BF16)) * lm_ref[0]
        for l in range(nl):
            el = e_ref[(l + 1) * ch:(l + 2) * ch, sl]
            up = ((row >> l) & 1) == 1
            xl = (jnp.where(up, q, k) * jnp.exp(el)).astype(BF16)
            scores = scores + _dot_nt(xl, xl) * lm_ref[l + 1]
        intra = _dot(scores.astype(BF16), v_b)
        st = st_ref[h]
        inter = _dot_nt((q * jnp.exp(b)).astype(BF16), st.astype(BF16))
        ke = (k * jnp.exp(e_ref[(nl + 1) * ch:(nl + 2) * ch, sl])).astype(BF16)
        upd = lax.dot_general(v_b, ke, (((0,), (0,)), ((), ())), preferred_element_type=F32)
        st_ref[h] = st * jnp.exp(b[ch - 1:ch, :]) + upd
        o = inter + intra
        ms = jnp.mean(o * o, axis=-1, keepdims=True)
        z = z_ref[:, sl]
        o_ref[:, sl] = (o * lax.rsqrt(ms + EPS) * nw * (z * jax.nn.sigmoid(z))).astype(o_ref.dtype)


def _hgrn(proj, lb_logits, hg_norm_w, ch=128):
    S = proj.shape[0]
    nl, mc, lm = _hgrn_consts(ch)
    kern = functools.partial(_hgrn_kernel, ch=ch, nl=nl)
    nlb = lb_logits.shape[0]
    return pl.pallas_call(
        kern,
        grid=(S // ch,),
        in_specs=[
            pl.BlockSpec((ch, HG_FDIM), lambda c: (c, 0)),
            pl.BlockSpec((ch, HG_FDIM), lambda c: (c, 1)),
            pl.BlockSpec((ch, HG_WIDTH), lambda c: (c, 2)),
            pl.BlockSpec((ch, HG_WIDTH), lambda c: (c, 3)),
            pl.BlockSpec((nlb, HG_FDIM), lambda c: (0, 0)),
            pl.BlockSpec((1, HG_DV), lambda c: (0, 0)),
            pl.BlockSpec(mc.shape, lambda c: (0, 0)),
            pl.BlockSpec(lm.shape, lambda c: (0, 0, 0)),
        ],
        out_specs=pl.BlockSpec((ch, HG_WIDTH), lambda c: (c, 0)),
        out_shape=jax.ShapeDtypeStruct((S, HG_WIDTH), BF16),
        scratch_shapes=[
            pltpu.VMEM((HG_HEADS, HG_DV, HG_DK), F32),
            pltpu.VMEM(((nl + 2) * ch, HG_FDIM), F32),
        ],
        compiler_params=_cparams(("arbitrary",)),
        name="hgrn",
    )(proj, proj, proj, proj, lb_logits, hg_norm_w.reshape(1, HG_DV), jnp.asarray(mc, BF16), jnp.asarray(lm, F32))


def _nsa_kernel(q_ref, cos_ref, sin_ref, gate_ref, z_ref, kc_ref, vc_ref, ov_ref, ca_ref, wb_ref, cb_ref,
                ks_ref, vs_ref, kw_ref, vw_ref, o_ref, m_ref, accl_ref, sa_ref, sb_ref, part_ref, *, tk):
    n = pl.program_id(1)
    q0 = n * Q_BLOCK
    R = NSA_HG * Q_BLOCK
    qscale = (HEAD_DIM ** -0.5) * LOG2E
    cosf = cos_ref[...]
    sinf = sin_ref[...]
    q4 = jnp.concatenate(
        [(_rope(q_ref[:, h * HEAD_DIM:(h + 1) * HEAD_DIM], cosf, sinf) * qscale).astype(BF16) for h in range(NSA_HG)],
        axis=0)

    def per_head(a):
        return jnp.concatenate([a] * NSA_HG, axis=0)

    def tpos(shape):
        return q0 + (lax.broadcasted_iota(jnp.int32, shape, 0) & (Q_BLOCK - 1))

    ncp = kc_ref.shape[2]
    nb = Q_BLOCK // CMP_STRIDE
    c_first = n * nb - (CMP_LEN // CMP_STRIDE - 1)
    crow = lax.broadcasted_iota(jnp.int32, (ncp, LANES), 0)
    clane = lax.broadcasted_iota(jnp.int32, (ncp, LANES), 1)
    flags = jnp.where(clane == 0, jnp.where(crow >= c_first + nb, NEG_BIG, 0.0),
                      jnp.where(crow == c_first + clane - 1, 1.0, 0.0)).astype(BF16)
    kca = jnp.concatenate([kc_ref[0, 0].astype(BF16), flags], axis=1)
    s = _halves(_dot_nt, jnp.concatenate([q4, per_head(ca_ref[...])], axis=1), kca)
    e = jnp.exp2(s - jnp.max(s, axis=-1, keepdims=True))
    vca = jnp.concatenate([vc_ref[0, 0].astype(BF16), jnp.ones((ncp, LANES), BF16)], axis=1)
    pv = _halves(_dot, e.astype(BF16), vca)
    any_visible = jnp.where(tpos((R, LANES)) >= CMP_LEN - 1, 1.0, 0.0)
    inv = any_visible / pv[:, LANES:]
    o_cmp = pv[:, :LANES] * inv
    p = e * jnp.concatenate([inv] * (ncp // LANES), axis=1)
    p_sum = p[0:Q_BLOCK]
    for h in range(1, NSA_HG):
        p_sum = p_sum + p[h * Q_BLOCK:(h + 1) * Q_BLOCK]
    ov = ov_ref[...]
    ps_hi, ps_lo = _split_bf16(p_sum)
    imp = _dot(ps_hi, ov) + _dot(ps_lo, ov)

    wspan = WINDOW + Q_BLOCK
    w0 = pl.multiple_of(jnp.maximum(q0 - WINDOW, 0), Q_BLOCK)
    sw = (_halves(_dot_nt, q4, kw_ref[pl.ds(w0, wspan), :])
          + per_head(wb_ref[jnp.minimum(n, WINDOW // Q_BLOCK)]))
    ew = jnp.exp2(sw - jnp.max(sw, axis=-1, keepdims=True))
    vwa = jnp.concatenate([vw_ref[pl.ds(w0, wspan), :], jnp.ones((wspan, LANES), BF16)], axis=1)
    pvw = _halves(_dot, ew.astype(BF16), vwa)
    o_win = pvw[:, :LANES] / pvw[:, LANES:]

    gate = jax.nn.sigmoid(gate_ref[...])
    for h in range(NSA_HG):
        rs = slice(h * Q_BLOCK, (h + 1) * Q_BLOCK)
        part_ref[:, h * HEAD_DIM:(h + 1) * HEAD_DIM] = (
            gate[:, 3 * h:3 * h + 1] * o_cmp[rs] + gate[:, 3 * h + 2:3 * h + 3] * o_win[rs])

    imp_t = imp.T
    nsel_pad = imp_t.shape[0]
    jj = lax.broadcasted_iota(jnp.int32, (nsel_pad, Q_BLOCK), 0)
    tq = q0 + lax.broadcasted_iota(jnp.int32, (nsel_pad, Q_BLOCK), 1)
    jt = tq >> 6
    forced = (jj == 0) | (jj == jt) | (jj == jt - 1)
    jjf = jj.astype(F32)
    score = jnp.where(jj * SEL_LEN <= tq, jnp.where(forced, FORCED_SCORE, imp_t), -1.0)
    sel = jnp.zeros((nsel_pad, Q_BLOCK), F32)
    for _ in range(SEL_TOPK):
        mx = jnp.max(score, axis=0, keepdims=True)
        first = jnp.min(jnp.where(score == mx, jjf, float(nsel_pad)), axis=0, keepdims=True)
        hit = jjf == first
        sel = jnp.where(hit, jnp.where(mx >= 0.0, 1.0, 0.0), sel)
        score = jnp.where(hit, -2.0, score)
    sel_bias = per_head(((sel.T - 1.0) * (-NEG_BIG)).astype(BF16))

    m_ref[...] = jnp.full(m_ref.shape, NEG_BIG, F32)
    accl_ref[...] = jnp.zeros(accl_ref.shape, F32)
    qa = jnp.concatenate([q4, sel_bias], axis=1)
    key_blk = lax.broadcasted_iota(jnp.int32, (tk, LANES), 0) >> 6
    blk_lane = lax.broadcasted_iota(jnp.int32, (tk, LANES), 1)
    ones_k = jnp.ones((tk, LANES), BF16)
    rep = tk // LANES

    n_q = tk // Q_BLOCK
    last = lax.shift_right_logical(n, n_q.bit_length() - 1)
    diag = n - last * n_q

    def score_tile(kt, dst_ref):
        k0 = pl.multiple_of(kt * tk, tk)
        onehot = jnp.where(key_blk + kt * (tk // SEL_LEN) == blk_lane, 1.0, 0.0).astype(BF16)
        dst_ref[...] = _dot_nt(qa, jnp.concatenate([ks_ref[pl.ds(k0, tk), :], onehot], axis=1))

    def consume_tile(kt, src_ref):
        k0 = pl.multiple_of(kt * tk, tk)
        sc = src_ref[...] + per_head(cb_ref[jnp.where(kt == last, diag, n_q)])
        m_prev = m_ref[...]
        m_next = jnp.maximum(m_prev, jnp.max(sc, axis=-1, keepdims=True))
        pr = jnp.exp2(sc - jnp.concatenate([m_next] * rep, axis=1))
        alpha = jnp.exp2(m_prev - m_next)
        va = jnp.concatenate([vs_ref[pl.ds(k0, tk), :], ones_k], axis=1)
        accl_ref[...] = jnp.concatenate([alpha, alpha], axis=1) * accl_ref[...] + _dot(pr.astype(BF16), va)
        m_ref[...] = m_next

    score_tile(0, sa_ref)

    def slc_pair(i, carry):
        score_tile(2 * i + 1, sb_ref)
        consume_tile(2 * i, sa_ref)
        score_tile(2 * i + 2, sa_ref)
        consume_tile(2 * i + 1, sb_ref)
        return carry

    lax.fori_loop(0, lax.shift_right_logical(last, 1), slc_pair, 0)

    @pl.when((last & 1) == 0)
    def _():
        consume_tile(last, sa_ref)

    @pl.when((last & 1) == 1)
    def _():
        score_tile(last, sb_ref)
        consume_tile(last - 1, sa_ref)
        consume_tile(last, sb_ref)

    o_slc = accl_ref[:, :LANES] / accl_ref[:, LANES:]

    gsig = jax.nn.sigmoid(gate_ref[...])
    for h in range(NSA_HG):
        rs = slice(h * Q_BLOCK, (h + 1) * Q_BLOCK)
        cs = slice(h * HEAD_DIM, (h + 1) * HEAD_DIM)
        oh = part_ref[:, cs] + gsig[:, 3 * h + 1:3 * h + 2] * o_slc[rs]
        z = z_ref[:, cs]
        o_ref[:, cs] = (oh * (z * jax.nn.sigmoid(z))).astype(o_ref.dtype)


def _cmp_staircase():
    nb = Q_BLOCK // CMP_STRIDE
    a = np.zeros((Q_BLOCK, LANES), np.float32)
    a[:, 0] = 1.0
    r = np.arange(Q_BLOCK)[:, None]
    i = np.arange(nb)[None, :]
    a[:, 1:1 + nb] = np.where(r < CMP_STRIDE * i + (CMP_STRIDE - 1), NEG_BIG, 0.0)
    return a


def _window_masks():
    n_w = WINDOW // Q_BLOCK
    r = np.arange(Q_BLOCK)[None, :, None]
    c = np.arange(WINDOW + Q_BLOCK)[None, None, :]
    d = np.arange(n_w + 1)[:, None, None]
    ok = np.where(d < n_w, c <= Q_BLOCK * d + r, (c > r) & (c <= WINDOW + r))
    return np.where(ok, 0.0, NEG_BIG).astype(np.float32)


def _causal_staircases(tk):
    n_q = tk // Q_BLOCK
    r = np.arange(Q_BLOCK)[None, :, None]
    c = np.arange(tk)[None, None, :]
    d = np.arange(n_q + 1)[:, None, None]
    return np.where((c <= Q_BLOCK * d + r) | (d == n_q), 0.0, NEG_BIG).astype(np.float32)


def _overlap_matrix(ncp, nsel_pad, nsel):
    ci = np.arange(ncp)[:, None] * CMP_STRIDE
    sj = np.arange(nsel_pad)[None, :] * SEL_LEN
    ov = (ci < sj + SEL_LEN) & (ci + CMP_LEN > sj) & (np.arange(nsel_pad)[None, :] < nsel) & (np.arange(ncp)[:, None] < ncp - 1)
    return ov.astype(np.float32)


def _nsa(proj, cosf, sinf, kcvc, kvb, tk=512):
    S = proj.shape[0]
    ncp = kcvc.shape[2]
    nsel = S // SEL_LEN
    assert nsel <= LANES and S % tk == 0 and S >= WINDOW + Q_BLOCK
    assert CMP_LEN == 2 * CMP_STRIDE and Q_BLOCK % CMP_STRIDE == 0 and WINDOW % Q_BLOCK == 0
    ov = jnp.asarray(_overlap_matrix(ncp, LANES, nsel), BF16)
    ca = jnp.asarray(_cmp_staircase(), BF16)
    wb = jnp.asarray(_window_masks(), F32)
    n_q = tk // Q_BLOCK
    assert n_q & (n_q - 1) == 0 and tk % SEL_LEN == 0
    cb = jnp.asarray(_causal_staircases(tk), F32)
    qw = NSA_HG * HEAD_DIM
    R = NSA_HG * Q_BLOCK
    kern = functools.partial(_nsa_kernel, tk=tk)
    return pl.pallas_call(
        kern,
        grid=(NSA_KV, S // Q_BLOCK),
        in_specs=[
            pl.BlockSpec((Q_BLOCK, qw), lambda g, n: (n, OFF_Q // qw + g)),
            pl.BlockSpec((Q_BLOCK, LANES), lambda g, n: (n, 0)),
            pl.BlockSpec((Q_BLOCK, LANES), lambda g, n: (n, 0)),
            pl.BlockSpec((Q_BLOCK, LANES), lambda g, n: (n, OFF_G // LANES + g)),
            pl.BlockSpec((Q_BLOCK, qw), lambda g, n: (n, OFF_Z // qw + g)),
            pl.BlockSpec((1, 1, ncp, HEAD_DIM), lambda g, n: (0, g, 0, 0)),
            pl.BlockSpec((1, 1, ncp, HEAD_DIM), lambda g, n: (1, g, 0, 0)),
            pl.BlockSpec((ncp, LANES), lambda g, n: (0, 0)),
            pl.BlockSpec((Q_BLOCK, LANES), lambda g, n: (0, 0)),
            pl.BlockSpec((WINDOW // Q_BLOCK + 1, Q_BLOCK, WINDOW + Q_BLOCK), lambda g, n: (0, 0, 0)),
            pl.BlockSpec((n_q + 1, Q_BLOCK, tk), lambda g, n: (0, 0, 0)),
            pl.BlockSpec((S, HEAD_DIM), lambda g, n: (0, 0 * NSA_KV + g)),
            pl.BlockSpec((S, HEAD_DIM), lambda g, n: (0, 1 * NSA_KV + g)),
            pl.BlockSpec((S, HEAD_DIM), lambda g, n: (0, 2 * NSA_KV + g)),
            pl.BlockSpec((S, HEAD_DIM), lambda g, n: (0, 3 * NSA_KV + g)),
        ],
        out_specs=pl.BlockSpec((Q_BLOCK, qw), lambda g, n: (n, g)),
        out_shape=jax.ShapeDtypeStruct((S, NSA_WIDTH), BF16),
        scratch_shapes=[
            pltpu.VMEM((R, LANES), F32),
            pltpu.VMEM((R, HEAD_DIM + LANES), F32),
            pltpu.VMEM((R, tk), F32),
            pltpu.VMEM((R, tk), F32),
            pltpu.VMEM((Q_BLOCK, qw), F32),
        ],
        compiler_params=_cparams(("arbitrary", "arbitrary")),
        name="nsa",
    )(proj, cosf, sinf, proj, proj, kcvc, kcvc, ov, ca, wb, cb, kvb, kvb, kvb, kvb)


def _merge_kernel(ohg_ref, onsa_ref, g0_ref, g1_ref, g2_ref, g3_ref, whg_ref, wnsa_ref, o_ref):
    y_hg = _dot(ohg_ref[...], whg_ref[...])
    y_nsa = _dot(onsa_ref[...], wnsa_ref[...])
    half = D_MODEL // 2
    gh = (g0_ref, g1_ref)
    gn = (g2_ref, g3_ref)
    for c in range(2):
        cs = slice(c * half, (c + 1) * half)
        m = jax.nn.sigmoid(gh[c][...]) * y_hg[:, cs] + jax.nn.sigmoid(gn[c][...]) * y_nsa[:, cs]
        o_ref[:, cs] = m.astype(o_ref.dtype)


def _merge(o_hg, o_nsa, proj, w_hg, w_nsa, tm=256):
    S = o_hg.shape[0]
    tm = min(tm, S)
    half = D_MODEL // 2
    gb = OFF_MG // half
    return pl.pallas_call(
        _merge_kernel,
        grid=(S // tm,),
        in_specs=[
            pl.BlockSpec((tm, HG_WIDTH), lambda i: (i, 0)),
            pl.BlockSpec((tm, NSA_WIDTH), lambda i: (i, 0)),
            pl.BlockSpec((tm, half), lambda i: (i, gb + 0)),
            pl.BlockSpec((tm, half), lambda i: (i, gb + 1)),
            pl.BlockSpec((tm, half), lambda i: (i, gb + 2)),
            pl.BlockSpec((tm, half), lambda i: (i, gb + 3)),
            pl.BlockSpec((HG_WIDTH, D_MODEL), lambda i: (0, 0)),
            pl.BlockSpec((NSA_WIDTH, D_MODEL), lambda i: (0, 0)),
        ],
        out_specs=pl.BlockSpec((tm, D_MODEL), lambda i: (i, 0)),
        out_shape=jax.ShapeDtypeStruct((S, D_MODEL), BF16),
        compiler_params=_cparams(("arbitrary",)),
        name="merge",
    )(o_hg, o_nsa, proj, proj, proj, proj, w_hg, w_nsa)


def _out_kernel(x_ref, m_ref, w_ref, nw_ref, o_ref):
    h = x_ref[...] + _dot(m_ref[...], w_ref[...])
    ms = jnp.mean(h * h, axis=-1, keepdims=True)
    o_ref[...] = h * lax.rsqrt(ms + EPS) * nw_ref[...]


def _out(x2, merged, w_out, final_w, tm=256):
    S, D = x2.shape
    tm = min(tm, S)
    return pl.pallas_call(
        _out_kernel,
        grid=(S // tm,),
        in_specs=[
            pl.BlockSpec((tm, D), lambda i: (i, 0)),
            pl.BlockSpec((tm, D), lambda i: (i, 0)),
            pl.BlockSpec((D, D), lambda i: (0, 0)),
            pl.BlockSpec((1, D), lambda i: (0, 0)),
        ],
        out_specs=pl.BlockSpec((tm, D), lambda i: (i, 0)),
        out_shape=jax.ShapeDtypeStruct((S, D), F32),
        compiler_params=_cparams(("arbitrary",)),
        name="out_proj",
    )(x2, merged, w_out, final_w.reshape(1, D))


def _layer(x2, norm_w, w_in, lb_logits, hg_norm_w, cmp_k_pos, cmp_k_w1, cmp_k_b1, cmp_k_w2,
           cmp_v_pos, cmp_v_w1, cmp_v_b1, cmp_v_w2, w_branch_hg, w_branch_nsa, w_out, final_w):
    S = x2.shape[0]
    proj = _norm_proj(x2, norm_w, _w_prep(w_in.T))
    cosf, sinf = _rope_tables(S)
    segs, kvb = _kv_prep(proj, cosf, sinf)
    pe = jnp.stack([cmp_k_pos.reshape(1, -1), cmp_v_pos.reshape(1, -1)])
    w1 = jnp.stack([cmp_k_w1, cmp_v_w1]).astype(BF16)
    b1 = jnp.stack([cmp_k_b1.reshape(1, -1), cmp_v_b1.reshape(1, -1)])
    w2 = jnp.stack([cmp_k_w2, cmp_v_w2]).astype(BF16)
    kcvc = _compress(segs, pe, w1, b1, w2)
    o_hg = _hgrn(proj, lb_logits, hg_norm_w)
    o_nsa = _nsa(proj, cosf, sinf, kcvc, kvb)
    merged = _merge(o_hg, o_nsa, proj, w_branch_hg.astype(BF16), w_branch_nsa.astype(BF16))
    return _out(x2, merged, w_out.astype(BF16), final_w)


def kernel(x, norm_w, w_in, hg_lb_logits, hg_norm_w, cmp_k_pos, cmp_k_w1, cmp_k_b1, cmp_k_w2, cmp_v_pos, cmp_v_w1, cmp_v_b1, cmp_v_w2, w_branch_hg, w_branch_nsa, w_out, final_norm_w):
    B, S, D = x.shape
    assert B == 1 and D == D_MODEL and norm_w.shape[0] == 1
    y = _layer(x[0], norm_w[0], w_in[0], hg_lb_logits, hg_norm_w[0], cmp_k_pos[0], cmp_k_w1[0], cmp_k_b1[0],
               cmp_k_w2[0], cmp_v_pos[0], cmp_v_w1[0], cmp_v_b1[0], cmp_v_w2[0], w_branch_hg[0],
               w_branch_nsa[0], w_out[0], final_norm_w)
    return y[None]
```

```python
import functools

import numpy as np
import jax
import jax.numpy as jnp
from jax import lax
from jax.experimental import pallas as pl
from jax.experimental.pallas import tpu as pltpu

F32 = jnp.float32
BF16 = jnp.bfloat16

D_MODEL = 2048
HG_HEADS = 8
HG_DK = 128
HG_DV = 128
HG_FDIM = HG_HEADS * HG_DK
HG_WIDTH = HG_HEADS * HG_DV
HG_MM_LEVELS = 3
NSA_HEADS = 16
NSA_KV = 4
NSA_HG = NSA_HEADS // NSA_KV
HEAD_DIM = 128
NSA_WIDTH = NSA_HEADS * HEAD_DIM
NSA_KVW = NSA_KV * HEAD_DIM
CMP_LEN = 32
CMP_STRIDE = 16
CMP_HIDDEN = 512
SEL_LEN = 64
SEL_TOPK = 16
WINDOW = 512
Q_BLOCK = 256
ROPE_THETA = 500000.0
ROT_DIM = HEAD_DIM // 4
ROT_HALF = ROT_DIM // 2
EPS = 1e-6
LOG2E = float(np.log2(np.e))

LANES = 128
NEG_BIG = -1e30
FORCED_SCORE = 1e30
VMEM_LIMIT = 56 * 1024 * 1024

OFF_HG = 0
OFF_Q = 4 * HG_FDIM
OFF_KV = OFF_Q + NSA_WIDTH
OFF_MG = OFF_KV + 6 * NSA_KVW
OFF_Z = OFF_MG + 2 * D_MODEL
OFF_G = OFF_Z + NSA_WIDTH
PROJ_COLS = OFF_G + NSA_KV * LANES
IN_GATE_OFF = OFF_MG
IN_Z_OFF = IN_GATE_OFF + 3 * NSA_HEADS
IN_MG_OFF = IN_Z_OFF + NSA_WIDTH


def _dot(a, b):
    return jnp.dot(a, b, preferred_element_type=F32)


def _dot_nt(a, b):
    return lax.dot_general(a, b, (((1,), (1,)), ((), ())), preferred_element_type=F32)


def _halves(dot, a, b):
    h = a.shape[0] // 2
    return jnp.concatenate([dot(a[:h], b), dot(a[h:], b)], axis=0)


def _split_bf16(a):
    hi = a.astype(BF16)
    lo = (a - hi.astype(F32)).astype(BF16)
    return hi, lo


def _cparams(sem):
    return pltpu.CompilerParams(dimension_semantics=sem, vmem_limit_bytes=VMEM_LIMIT)


def _w_prep_kernel(wt_ref, wg_ref, o_ref, *, n_direct):
    j = pl.program_id(0)

    @pl.when(j < n_direct)
    def _():
        o_ref[...] = wt_ref[...].astype(BF16)

    @pl.when(j >= n_direct)
    def _():
        o_ref[...] = wg_ref[...].astype(BF16)


def _w_prep(wt, tn=512):
    D = wt.shape[1]
    n_main, n_mg, n_z = OFF_MG // tn, 2 * D_MODEL // tn, NSA_WIDTH // tn
    n_direct = n_main + n_mg + n_z
    assert OFF_MG % tn == 0 and D_MODEL % tn == 0 and NSA_KV * LANES == tn
    gates = wt[IN_GATE_OFF:IN_Z_OFF].reshape(NSA_KV, 3 * NSA_HG, D)
    gates = jnp.pad(gates, ((0, 0), (0, LANES - 3 * NSA_HG), (0, 0))).reshape(NSA_KV * LANES, D)

    sub = 8
    assert IN_MG_OFF % sub == 0 and IN_Z_OFF % sub == 0 and tn % sub == 0

    def src_row(j):
        in_mg = IN_MG_OFF // sub + (j - n_main) * (tn // sub)
        in_z = IN_Z_OFF // sub + (jnp.minimum(j, n_direct - 1) - n_main - n_mg) * (tn // sub)
        return jnp.where(j < n_main, j * (tn // sub), jnp.where(j < n_main + n_mg, in_mg, in_z)) * sub

    return pl.pallas_call(
        functools.partial(_w_prep_kernel, n_direct=n_direct),
        grid=(n_direct + 1,),
        in_specs=[
            pl.BlockSpec((pl.Element(tn), pl.Element(D)), lambda j: (src_row(j), 0)),
            pl.BlockSpec((tn, D), lambda j: (0, 0)),
        ],
        out_specs=pl.BlockSpec((tn, D), lambda j: (j, 0)),
        out_shape=jax.ShapeDtypeStruct(((n_direct + 1) * tn, D), BF16),
        compiler_params=_cparams(("arbitrary",)),
        name="w_prep",
    )(wt, gates)


def _norm_proj_kernel(x_ref, nw_ref, w_ref, o_ref, xn_ref):
    @pl.when(pl.program_id(1) == 0)
    def _():
        x = x_ref[...]
        ms = jnp.mean(x * x, axis=-1, keepdims=True)
        xn_ref[...] = (x * lax.rsqrt(ms + EPS) * nw_ref[...]).astype(BF16)

    o_ref[...] = _dot_nt(xn_ref[...], w_ref[...])


def _norm_proj(x2, norm_w, wtb, tm=1024, tn=512):
    S, D = x2.shape
    N = wtb.shape[0]
    tm = min(tm, S)
    return pl.pallas_call(
        _norm_proj_kernel,
        grid=(S // tm, N // tn),
        in_specs=[
            pl.BlockSpec((tm, D), lambda i, j: (i, 0)),
            pl.BlockSpec((1, D), lambda i, j: (0, 0)),
            pl.BlockSpec((tn, D), lambda i, j: (j, 0)),
        ],
        out_specs=pl.BlockSpec((tm, tn), lambda i, j: (i, j)),
        out_shape=jax.ShapeDtypeStruct((S, N), F32),
        scratch_shapes=[pltpu.VMEM((tm, D), BF16)],
        compiler_params=_cparams(("arbitrary", "arbitrary")),
        name="norm_proj",
    )(x2, norm_w.reshape(1, D), wtb)


def _rope(x, cosf, sinf):
    lane = lax.broadcasted_iota(jnp.int32, x.shape, 1)
    rot = jnp.where(lane < ROT_HALF, pltpu.roll(x, LANES - ROT_HALF, 1), pltpu.roll(x, ROT_HALF, 1))
    return x * cosf + rot * sinf


def _rope_tables(S):
    pos = np.arange(S, dtype=np.float64)
    inv = ROPE_THETA ** (-np.arange(0, ROT_DIM, 2, dtype=np.float64) / ROT_DIM)
    ang = pos[:, None] * inv[None, :]
    cos, sin = np.cos(ang), np.sin(ang)
    rest = LANES - ROT_DIM
    cosf = np.concatenate([cos, cos, np.ones((S, rest))], axis=1)
    sinf = np.concatenate([-sin, sin, np.zeros((S, rest))], axis=1)
    return jnp.asarray(cosf, F32), jnp.asarray(sinf, F32)


def _kv_prep_kernel(kv_ref, cos_ref, sin_ref, cmp_ref, kvb_ref, row_ref):
    cosf = cos_ref[...]
    sinf = sin_ref[...]
    W = NSA_KVW
    n_seg = row_ref.shape[0] // CMP_STRIDE

    def to_segments(a, g, rows):
        row_ref[...] = rows
        for l in range(CMP_STRIDE):
            cmp_ref[a, g, :, l * HEAD_DIM:(l + 1) * HEAD_DIM] = row_ref[pl.ds(l, n_seg, stride=CMP_STRIDE), :]

    for g in range(NSA_KV):
        kc = kv_ref[:, 0 * W + g * LANES:0 * W + (g + 1) * LANES]
        to_segments(0, g, _rope(kc, cosf, sinf))
        to_segments(1, g, kv_ref[:, 1 * W + g * LANES:1 * W + (g + 1) * LANES])
        ks = kv_ref[:, 2 * W + g * LANES:2 * W + (g + 1) * LANES]
        kvb_ref[:, 0 * W + g * LANES:0 * W + (g + 1) * LANES] = _rope(ks, cosf, sinf).astype(BF16)
        kvb_ref[:, 1 * W + g * LANES:1 * W + (g + 1) * LANES] = kv_ref[:, 3 * W + g * LANES:3 * W + (g + 1) * LANES].astype(BF16)
        kw = kv_ref[:, 4 * W + g * LANES:4 * W + (g + 1) * LANES]
        kvb_ref[:, 2 * W + g * LANES:2 * W + (g + 1) * LANES] = _rope(kw, cosf, sinf).astype(BF16)
        kvb_ref[:, 3 * W + g * LANES:3 * W + (g + 1) * LANES] = kv_ref[:, 5 * W + g * LANES:5 * W + (g + 1) * LANES].astype(BF16)


def _kv_prep(proj, cosf, sinf, tm=512):
    S = proj.shape[0]
    tm = min(tm, S)
    kvw = 6 * NSA_KVW
    return pl.pallas_call(
        _kv_prep_kernel,
        grid=(S // tm,),
        in_specs=[
            pl.BlockSpec((tm, kvw), lambda i: (i, OFF_KV // kvw)),
            pl.BlockSpec((tm, LANES), lambda i: (i, 0)),
            pl.BlockSpec((tm, LANES), lambda i: (i, 0)),
        ],
        out_specs=[
            pl.BlockSpec((2, NSA_KV, tm // CMP_STRIDE, CMP_STRIDE * HEAD_DIM), lambda i: (0, 0, i, 0)),
            pl.BlockSpec((tm, 4 * NSA_KVW), lambda i: (i, 0)),
        ],
        out_shape=[
            jax.ShapeDtypeStruct((2, NSA_KV, S // CMP_STRIDE, CMP_STRIDE * HEAD_DIM), F32),
            jax.ShapeDtypeStruct((S, 4 * NSA_KVW), BF16),
        ],
        scratch_shapes=[pltpu.VMEM((tm, HEAD_DIM), F32)],
        compiler_params=_cparams(("arbitrary",)),
        name="kv_prep",
    )(proj, cosf, sinf)


def _compress_kernel(seg_ref, pe_ref, w1_ref, b1_ref, w2_ref, o_ref):
    half = (CMP_LEN // 2) * HEAD_DIM
    seg = seg_ref[0, 0]
    n_seg = seg.shape[0]
    pe = pe_ref[0]
    a = (seg + pe[:, :half]).astype(BF16)
    b = (seg + pe[:, half:]).astype(BF16)
    u = _dot(a, w1_ref[0, :half, :])
    v = _dot(b, w1_ref[0, half:, :])
    v_next = pltpu.roll(v, n_seg - 1, 0)
    pre = u + v_next + b1_ref[0]
    h = 0.5 * pre * (1.0 + jnp.tanh(np.sqrt(2.0 / np.pi).astype(np.float32) * (pre + 0.044715 * (pre * pre * pre))))
    o_ref[0, 0] = _dot(h.astype(BF16), w2_ref[0])


def _compress(segs, pe, w1, b1, w2):
    _, G, n_seg, segw = segs.shape
    return pl.pallas_call(
        _compress_kernel,
        grid=(2, G),
        in_specs=[
            pl.BlockSpec((1, 1, n_seg, segw), lambda a, g: (a, g, 0, 0)),
            pl.BlockSpec((1, 1, 2 * segw), lambda a, g: (a, 0, 0)),
            pl.BlockSpec((1, 2 * segw, CMP_HIDDEN), lambda a, g: (a, 0, 0)),
            pl.BlockSpec((1, 1, CMP_HIDDEN), lambda a, g: (a, 0, 0)),
            pl.BlockSpec((1, CMP_HIDDEN, HEAD_DIM), lambda a, g: (a, 0, 0)),
        ],
        out_specs=pl.BlockSpec((1, 1, n_seg, HEAD_DIM), lambda a, g: (a, g, 0, 0)),
        out_shape=jax.ShapeDtypeStruct((2, G, n_seg, HEAD_DIM), F32),
        compiler_params=_cparams(("arbitrary", "arbitrary")),
        name="compress",
    )(segs, pe, w1, b1, w2)


def _hgrn_consts(ch):
    nl = int(np.log2(ch))
    assert 1 << nl == ch
    t = np.arange(ch)[:, None]
    r = np.arange(ch)[None, :]
    mats = [r <= t]
    masks = [np.eye(ch, dtype=bool)]
    for l in range(nl):
        half = 1 << l
        blk = t // (2 * half)
        ref = blk * 2 * half + half - 1
        up = ((t >> l) & 1) == 1
        mats.append(np.where(up, (r > ref) & (r <= t), (r > t) & (r <= ref)))
        masks.append(up & (((r >> l) & 1) == 0) & (blk == r // (2 * half)))
    mc = np.concatenate(mats[:1 + HG_MM_LEVELS], axis=0).astype(np.float32)
    lm = np.stack(masks, axis=0).astype(np.float32)
    return nl, mc, lm


def _hgrn_kernel(q_ref, f_ref, i_ref, z_ref, lbl_ref, nw_ref, mc_ref, lm_ref, o_ref, st_ref, e_ref, *, ch, nl):
    @pl.when(pl.program_id(0) == 0)
    def _():
        st_ref[...] = jnp.zeros_like(st_ref)

    lg = lbl_ref[...]
    ex = jnp.exp(lg - jnp.max(lg, axis=0, keepdims=True))
    lb = ex[0:1] / jnp.sum(ex, axis=0, keepdims=True)
    f = lb + (1.0 - lb) * jax.nn.sigmoid(f_ref[...])
    g_hi, g_lo = _split_bf16(jnp.log(f))
    mc = mc_ref[...]
    e_ref[...] = _dot(mc, g_hi) + _dot(mc, g_lo)
    row = lax.broadcasted_iota(jnp.int32, (ch, 1), 0)
    nw = nw_ref[...]

    heads = [slice(h * HG_DK, (h + 1) * HG_DK) for h in range(HG_HEADS)]
    qs = [q_ref[:, sl] for sl in heads]
    ks = [1.0 - f[:, sl] for sl in heads]
    scs = [_dot_nt(q.astype(BF16), k.astype(BF16)) * lm_ref[0] for q, k in zip(qs, ks)]
    bs = [e_ref[0:ch, sl] for sl in heads]
    for l in range(nl):
        up = ((row >> l) & 1) == 1
        half = 1 << l
        for h, sl in enumerate(heads):
            if l < HG_MM_LEVELS:
                el = e_ref[(l + 1) * ch:(l + 2) * ch, sl]
                xl = jnp.where(up, qs[h], ks[h]) * jnp.exp(el)
            else:
                parts = []
                for r0 in range(0, ch, 2 * half):
                    mid = r0 + half
                    edge = bs[h][mid - 1:mid, :]
                    parts.append(ks[h][r0:mid] * jnp.exp(edge - bs[h][r0:mid]))
                    parts.append(qs[h][mid:mid + half] * jnp.exp(bs[h][mid:mid + half] - edge))
                xl = jnp.concatenate(parts, axis=0)
            xl = xl.astype(BF16)
            scs[h] = scs[h] + _dot_nt(xl, xl) * lm_ref[l + 1]

    vbs = [i_ref[:, sl].astype(BF16) for sl in heads]
    sts = [st_ref[h] for h in range(HG_HEADS)]
    inters = [_dot_nt((qs[h] * jnp.exp(bs[h])).astype(BF16), sts[h].astype(BF16)) for h in range(HG_HEADS)]
    for h, sl in enumerate(heads):
        ke = (ks[h] * jnp.exp(bs[h][ch - 1:ch, :] - bs[h])).astype(BF16)
        upd = lax.dot_general(vbs[h], ke, (((0,), (0,)), ((), ())), preferred_element_type=F32)
        st_ref[h] = sts[h] * jnp.exp(bs[h][ch - 1:ch, :]) + upd
    outs = [inters[h] + _dot(scs[h].astype(BF16), vbs[h]) for h in range(HG_HEADS)]
    for h, sl in enumerate(heads):
        o = outs[h]
        ms = jnp.mean(o * o, axis=-1, keepdims=True)
        z = z_ref[:, sl]
        o_ref[:, sl] = (o * lax.rsqrt(ms + EPS) * nw * (z * jax.nn.sigmoid(z))).astype(o_ref.dtype)


def _hgrn(proj, lb_logits, hg_norm_w, ch=128):
    S = proj.shape[0]
    nl, mc, lm = _hgrn_consts(ch)
    kern = functools.partial(_hgrn_kernel, ch=ch, nl=nl)
    nlb = lb_logits.shape[0]
    return pl.pallas_call(
        kern,
        grid=(S // ch,),
        in_specs=[
            pl.BlockSpec((ch, HG_FDIM), lambda c: (c, 0)),
            pl.BlockSpec((ch, HG_FDIM), lambda c: (c, 1)),
            pl.BlockSpec((ch, HG_WIDTH), lambda c: (c, 2)),
            pl.BlockSpec((ch, HG_WIDTH), lambda c: (c, 3)),
            pl.BlockSpec((nlb, HG_FDIM), lambda c: (0, 0)),
            pl.BlockSpec((1, HG_DV), lambda c: (0, 0)),
            pl.BlockSpec(mc.shape, lambda c: (0, 0)),
            pl.BlockSpec(lm.shape, lambda c: (0, 0, 0)),
        ],
        out_specs=pl.BlockSpec((ch, HG_WIDTH), lambda c: (c, 0)),
        out_shape=jax.ShapeDtypeStruct((S, HG_WIDTH), BF16),
        scratch_shapes=[
            pltpu.VMEM((HG_HEADS, HG_DV, HG_DK), F32),
            pltpu.VMEM((mc.shape[0], HG_FDIM), F32),
        ],
        compiler_params=_cparams(("arbitrary",)),
        name="hgrn",
    )(proj, proj, proj, proj, lb_logits, hg_norm_w.reshape(1, HG_DV), jnp.asarray(mc, BF16), jnp.asarray(lm, F32))


def _nsa_kernel(q_ref, cos_ref, sin_ref, gate_ref, z_ref, kc_ref, vc_ref, ov_ref, ca_ref, wb_ref, cb_ref,
                ks_ref, vs_ref, kw_ref, vw_ref, o_ref, m_ref, accl_ref, sa_ref, sb_ref, part_ref, *, tk):
    n = pl.program_id(1)
    q0 = n * Q_BLOCK
    R = NSA_HG * Q_BLOCK
    qscale = (HEAD_DIM ** -0.5) * LOG2E
    cosf = cos_ref[...]
    sinf = sin_ref[...]
    q4 = jnp.concatenate(
        [(_rope(q_ref[:, h * HEAD_DIM:(h + 1) * HEAD_DIM], cosf, sinf) * qscale).astype(BF16) for h in range(NSA_HG)],
        axis=0)

    def per_head(a):
        return jnp.concatenate([a] * NSA_HG, axis=0)

    def tpos(shape):
        return q0 + (lax.broadcasted_iota(jnp.int32, shape, 0) & (Q_BLOCK - 1))

    ncp = kc_ref.shape[2]
    nb = Q_BLOCK // CMP_STRIDE
    c_first = n * nb - (CMP_LEN // CMP_STRIDE - 1)
    crow = lax.broadcasted_iota(jnp.int32, (ncp, LANES), 0)
    clane = lax.broadcasted_iota(jnp.int32, (ncp, LANES), 1)
    flags = jnp.where(clane == 0, jnp.where(crow >= c_first + nb, NEG_BIG, 0.0),
                      jnp.where(crow == c_first + clane - 1, 1.0, 0.0)).astype(BF16)
    kca = jnp.concatenate([kc_ref[0, 0].astype(BF16), flags], axis=1)
    s = _halves(_dot_nt, jnp.concatenate([q4, per_head(ca_ref[...])], axis=1), kca)
    e = jnp.exp2(s - jnp.max(s, axis=-1, keepdims=True))
    vca = jnp.concatenate([vc_ref[0, 0].astype(BF16), jnp.ones((ncp, LANES), BF16)], axis=1)
    pv = _halves(_dot, e.astype(BF16), vca)
    any_visible = jnp.where(tpos((R, LANES)) >= CMP_LEN - 1, 1.0, 0.0)
    inv = any_visible / pv[:, LANES:]
    o_cmp = pv[:, :LANES] * inv
    p = e * jnp.concatenate([inv] * (ncp // LANES), axis=1)
    p_sum = p[0:Q_BLOCK]
    for h in range(1, NSA_HG):
        p_sum = p_sum + p[h * Q_BLOCK:(h + 1) * Q_BLOCK]
    ov = ov_ref[...]
    ps_hi, ps_lo = _split_bf16(p_sum)
    imp = _dot(ps_hi, ov) + _dot(ps_lo, ov)

    wspan = WINDOW + Q_BLOCK
    w0 = pl.multiple_of(jnp.maximum(q0 - WINDOW, 0), Q_BLOCK)
    sw = (_halves(_dot_nt, q4, kw_ref[pl.ds(w0, wspan), :])
          + per_head(wb_ref[jnp.minimum(n, WINDOW // Q_BLOCK)]))
    ew = jnp.exp2(sw - jnp.max(sw, axis=-1, keepdims=True))
    vwa = jnp.concatenate([vw_ref[pl.ds(w0, wspan), :], jnp.ones((wspan, LANES), BF16)], axis=1)
    pvw = _halves(_dot, ew.astype(BF16), vwa)
    o_win = pvw[:, :LANES] / pvw[:, LANES:]

    gate = jax.nn.sigmoid(gate_ref[...])
    for h in range(NSA_HG):
        rs = slice(h * Q_BLOCK, (h + 1) * Q_BLOCK)
        part_ref[:, h * HEAD_DIM:(h + 1) * HEAD_DIM] = (
            gate[:, 3 * h:3 * h + 1] * o_cmp[rs] + gate[:, 3 * h + 2:3 * h + 3] * o_win[rs])

    imp_t = imp.T
    nsel_pad = imp_t.shape[0]
    jj = lax.broadcasted_iota(jnp.int32, (nsel_pad, Q_BLOCK), 0)
    tq = q0 + lax.broadcasted_iota(jnp.int32, (nsel_pad, Q_BLOCK), 1)
    jt = tq >> 6
    forced = (jj == 0) | (jj == jt) | (jj == jt - 1)
    jjf = jj.astype(F32)
    score = jnp.where(jj * SEL_LEN <= tq, jnp.where(forced, FORCED_SCORE, imp_t), -1.0)
    sel = jnp.zeros((nsel_pad, Q_BLOCK), F32)
    for _ in range(SEL_TOPK):
        mx = jnp.max(score, axis=0, keepdims=True)
        first = jnp.min(jnp.where(score == mx, jjf, float(nsel_pad)), axis=0, keepdims=True)
        hit = jjf == first
        sel = jnp.where(hit, jnp.where(mx >= 0.0, 1.0, 0.0), sel)
        score = jnp.where(hit, -2.0, score)
    sel_bias = per_head(((sel.T - 1.0) * (-NEG_BIG)).astype(BF16))

    m_ref[...] = jnp.full(m_ref.shape, NEG_BIG, F32)
    accl_ref[...] = jnp.zeros(accl_ref.shape, F32)
    qa = jnp.concatenate([q4, sel_bias], axis=1)
    key_blk = lax.broadcasted_iota(jnp.int32, (tk, LANES), 0) >> 6
    blk_lane = lax.broadcasted_iota(jnp.int32, (tk, LANES), 1)
    ones_k = jnp.ones((tk, LANES), BF16)
    rep = tk // LANES

    n_q = tk // Q_BLOCK
    last = lax.shift_right_logical(n, n_q.bit_length() - 1)
    diag = n - last * n_q

    def score_tile(kt, dst_ref):
        k0 = pl.multiple_of(kt * tk, tk)
        onehot = jnp.where(key_blk + kt * (tk // SEL_LEN) == blk_lane, 1.0, 0.0).astype(BF16)
        dst_ref[...] = _dot_nt(qa, jnp.concatenate([ks_ref[pl.ds(k0, tk), :], onehot], axis=1))

    def consume_tile(kt, src_ref):
        k0 = pl.multiple_of(kt * tk, tk)
        sc = src_ref[...] + per_head(cb_ref[jnp.where(kt == last, diag, n_q)])
        m_prev = m_ref[...]
        m_next = jnp.maximum(m_prev, jnp.max(sc, axis=-1, keepdims=True))
        pr = jnp.exp2(sc - jnp.concatenate([m_next] * rep, axis=1))
        alpha = jnp.exp2(m_prev - m_next)
        va = jnp.concatenate([vs_ref[pl.ds(k0, tk), :], ones_k], axis=1)
        accl_ref[...] = jnp.concatenate([alpha, alpha], axis=1) * accl_ref[...] + _dot(pr.astype(BF16), va)
        m_ref[...] = m_next

    score_tile(0, sa_ref)

    def slc_pair(i, carry):
        score_tile(2 * i + 1, sb_ref)
        consume_tile(2 * i, sa_ref)
        score_tile(2 * i + 2, sa_ref)
        consume_tile(2 * i + 1, sb_ref)
        return carry

    lax.fori_loop(0, lax.shift_right_logical(last, 1), slc_pair, 0)

    @pl.when((last & 1) == 0)
    def _():
        consume_tile(last, sa_ref)

    @pl.when((last & 1) == 1)
    def _():
        score_tile(last, sb_ref)
        consume_tile(last - 1, sa_ref)
        consume_tile(last, sb_ref)

    o_slc = accl_ref[:, :LANES] / accl_ref[:, LANES:]

    gsig = jax.nn.sigmoid(gate_ref[...])
    for h in range(NSA_HG):
        rs = slice(h * Q_BLOCK, (h + 1) * Q_BLOCK)
        cs = slice(h * HEAD_DIM, (h + 1) * HEAD_DIM)
        oh = part_ref[:, cs] + gsig[:, 3 * h + 1:3 * h + 2] * o_slc[rs]
        z = z_ref[:, cs]
        o_ref[:, cs] = (oh * (z * jax.nn.sigmoid(z))).astype(o_ref.dtype)


def _cmp_staircase():
    nb = Q_BLOCK // CMP_STRIDE
    a = np.zeros((Q_BLOCK, LANES), np.float32)
    a[:, 0] = 1.0
    r = np.arange(Q_BLOCK)[:, None]
    i = np.arange(nb)[None, :]
    a[:, 1:1 + nb] = np.where(r < CMP_STRIDE * i + (CMP_STRIDE - 1), NEG_BIG, 0.0)
    return a


def _window_masks():
    n_w = WINDOW // Q_BLOCK
    r = np.arange(Q_BLOCK)[None, :, None]
    c = np.arange(WINDOW + Q_BLOCK)[None, None, :]
    d = np.arange(n_w + 1)[:, None, None]
    ok = np.where(d < n_w, c <= Q_BLOCK * d + r, (c > r) & (c <= WINDOW + r))
    return np.where(ok, 0.0, NEG_BIG).astype(np.float32)


def _causal_staircases(tk):
    n_q = tk // Q_BLOCK
    r = np.arange(Q_BLOCK)[None, :, None]
    c = np.arange(tk)[None, None, :]
    d = np.arange(n_q + 1)[:, None, None]
    return np.where((c <= Q_BLOCK * d + r) | (d == n_q), 0.0, NEG_BIG).astype(np.float32)


def _overlap_matrix(ncp, nsel_pad, nsel):
    ci = np.arange(ncp)[:, None] * CMP_STRIDE
    sj = np.arange(nsel_pad)[None, :] * SEL_LEN
    ov = (ci < sj + SEL_LEN) & (ci + CMP_LEN > sj) & (np.arange(nsel_pad)[None, :] < nsel) & (np.arange(ncp)[:, None] < ncp - 1)
    return ov.astype(np.float32)


def _nsa(proj, cosf, sinf, kcvc, kvb, tk=512):
    S = proj.shape[0]
    ncp = kcvc.shape[2]
    nsel = S // SEL_LEN
    assert nsel <= LANES and S % tk == 0 and S >= WINDOW + Q_BLOCK
    assert CMP_LEN == 2 * CMP_STRIDE and Q_BLOCK % CMP_STRIDE == 0 and WINDOW % Q_BLOCK == 0
    ov = jnp.asarray(_overlap_matrix(ncp, LANES, nsel), BF16)
    ca = jnp.asarray(_cmp_staircase(), BF16)
    wb = jnp.asarray(_window_masks(), F32)
    n_q = tk // Q_BLOCK
    assert n_q & (n_q - 1) == 0 and tk % SEL_LEN == 0
    cb = jnp.asarray(_causal_staircases(tk), F32)
    qw = NSA_HG * HEAD_DIM
    R = NSA_HG * Q_BLOCK
    kern = functools.partial(_nsa_kernel, tk=tk)
    return pl.pallas_call(
        kern,
        grid=(NSA_KV, S // Q_BLOCK),
        in_specs=[
            pl.BlockSpec((Q_BLOCK, qw), lambda g, n: (n, OFF_Q // qw + g)),
            pl.BlockSpec((Q_BLOCK, LANES), lambda g, n: (n, 0)),
            pl.BlockSpec((Q_BLOCK, LANES), lambda g, n: (n, 0)),
            pl.BlockSpec((Q_BLOCK, LANES), lambda g, n: (n, OFF_G // LANES + g)),
            pl.BlockSpec((Q_BLOCK, qw), lambda g, n: (n, OFF_Z // qw + g)),
            pl.BlockSpec((1, 1, ncp, HEAD_DIM), lambda g, n: (0, g, 0, 0)),
            pl.BlockSpec((1, 1, ncp, HEAD_DIM), lambda g, n: (1, g, 0, 0)),
            pl.BlockSpec((ncp, LANES), lambda g, n: (0, 0)),
            pl.BlockSpec((Q_BLOCK, LANES), lambda g, n: (0, 0)),
            pl.BlockSpec((WINDOW // Q_BLOCK + 1, Q_BLOCK, WINDOW + Q_BLOCK), lambda g, n: (0, 0, 0)),
            pl.BlockSpec((n_q + 1, Q_BLOCK, tk), lambda g, n: (0, 0, 0)),
            pl.BlockSpec((S, HEAD_DIM), lambda g, n: (0, 0 * NSA_KV + g)),
            pl.BlockSpec((S, HEAD_DIM), lambda g, n: (0, 1 * NSA_KV + g)),
            pl.BlockSpec((S, HEAD_DIM), lambda g, n: (0, 2 * NSA_KV + g)),
            pl.BlockSpec((S, HEAD_DIM), lambda g, n: (0, 3 * NSA_KV + g)),
        ],
        out_specs=pl.BlockSpec((Q_BLOCK, qw), lambda g, n: (n, g)),
        out_shape=jax.ShapeDtypeStruct((S, NSA_WIDTH), BF16),
        scratch_shapes=[
            pltpu.VMEM((R, LANES), F32),
            pltpu.VMEM((R, HEAD_DIM + LANES), F32),
            pltpu.VMEM((R, tk), F32),
            pltpu.VMEM((R, tk), F32),
            pltpu.VMEM((Q_BLOCK, qw), F32),
        ],
        compiler_params=_cparams(("arbitrary", "arbitrary")),
        name="nsa",
    )(proj, cosf, sinf, proj, proj, kcvc, kcvc, ov, ca, wb, cb, kvb, kvb, kvb, kvb)


def _merge_kernel(ohg_ref, onsa_ref, g0_ref, g1_ref, g2_ref, g3_ref, whg_ref, wnsa_ref, o_ref):
    y_hg = _dot(ohg_ref[...], whg_ref[...])
    y_nsa = _dot(onsa_ref[...], wnsa_ref[...])
    half = D_MODEL // 2
    gh = (g0_ref, g1_ref)
    gn = (g2_ref, g3_ref)
    for c in range(2):
        cs = slice(c * half, (c + 1) * half)
        m = jax.nn.sigmoid(gh[c][...]) * y_hg[:, cs] + jax.nn.sigmoid(gn[c][...]) * y_nsa[:, cs]
        o_ref[:, cs] = m.astype(o_ref.dtype)


def _merge(o_hg, o_nsa, proj, w_hg, w_nsa, tm=256):
    S = o_hg.shape[0]
    tm = min(tm, S)
    half = D_MODEL // 2
    gb = OFF_MG // half
    return pl.pallas_call(
        _merge_kernel,
        grid=(S // tm,),
        in_specs=[
            pl.BlockSpec((tm, HG_WIDTH), lambda i: (i, 0)),
            pl.BlockSpec((tm, NSA_WIDTH), lambda i: (i, 0)),
            pl.BlockSpec((tm, half), lambda i: (i, gb + 0)),
            pl.BlockSpec((tm, half), lambda i: (i, gb + 1)),
            pl.BlockSpec((tm, half), lambda i: (i, gb + 2)),
            pl.BlockSpec((tm, half), lambda i: (i, gb + 3)),
            pl.BlockSpec((HG_WIDTH, D_MODEL), lambda i: (0, 0)),
            pl.BlockSpec((NSA_WIDTH, D_MODEL), lambda i: (0, 0)),
        ],
        out_specs=pl.BlockSpec((tm, D_MODEL), lambda i: (i, 0)),
        out_shape=jax.ShapeDtypeStruct((S, D_MODEL), BF16),
        compiler_params=_cparams(("arbitrary",)),
        name="merge",
    )(o_hg, o_nsa, proj, proj, proj, proj, w_hg, w_nsa)


def _out_kernel(x_ref, m_ref, w_ref, nw_ref, o_ref):
    h = x_ref[...] + _dot(m_ref[...], w_ref[...])
    ms = jnp.mean(h * h, axis=-1, keepdims=True)
    o_ref[...] = h * lax.rsqrt(ms + EPS) * nw_ref[...]


def _out(x2, merged, w_out, final_w, tm=256):
    S, D = x2.shape
    tm = min(tm, S)
    return pl.pallas_call(
        _out_kernel,
        grid=(S // tm,),
        in_specs=[
            pl.BlockSpec((tm, D), lambda i: (i, 0)),
            pl.BlockSpec((tm, D), lambda i: (i, 0)),
            pl.BlockSpec((D, D), lambda i: (0, 0)),
            pl.BlockSpec((1, D), lambda i: (0, 0)),
        ],
        out_specs=pl.BlockSpec((tm, D), lambda i: (i, 0)),
        out_shape=jax.ShapeDtypeStruct((S, D), F32),
        compiler_params=_cparams(("arbitrary",)),
        name="out_proj",
    )(x2, merged, w_out, final_w.reshape(1, D))


def _layer(x2, norm_w, w_in, lb_logits, hg_norm_w, cmp_k_pos, cmp_k_w1, cmp_k_b1, cmp_k_w2,
           cmp_v_pos, cmp_v_w1, cmp_v_b1, cmp_v_w2, w_branch_hg, w_branch_nsa, w_out, final_w):
    S = x2.shape[0]
    proj = _norm_proj(x2, norm_w, _w_prep(w_in.T))
    cosf, sinf = _rope_tables(S)
    segs, kvb = _kv_prep(proj, cosf, sinf)
    pe = jnp.stack([cmp_k_pos.reshape(1, -1), cmp_v_pos.reshape(1, -1)])
    w1 = jnp.stack([cmp_k_w1, cmp_v_w1]).astype(BF16)
    b1 = jnp.stack([cmp_k_b1.reshape(1, -1), cmp_v_b1.reshape(1, -1)])
    w2 = jnp.stack([cmp_k_w2, cmp_v_w2]).astype(BF16)
    kcvc = _compress(segs, pe, w1, b1, w2)
    o_hg = _hgrn(proj, lb_logits, hg_norm_w)
    o_nsa = _nsa(proj, cosf, sinf, kcvc, kvb)
    merged = _merge(o_hg, o_nsa, proj, w_branch_hg.astype(BF16), w_branch_nsa.astype(BF16))
    return _out(x2, merged, w_out.astype(BF16), final_w)


def kernel(x, norm_w, w_in, hg_lb_logits, hg_norm_w, cmp_k_pos, cmp_k_w1, cmp_k_b1, cmp_k_w2, cmp_v_pos, cmp_v_w1, cmp_v_b1, cmp_v_w2, w_branch_hg, w_branch_nsa, w_out, final_norm_w):
    B, S, D = x.shape
    assert B == 1 and D == D_MODEL and norm_w.shape[0] == 1
    y = _layer(x[0], norm_w[0], w_in[0], hg_lb_logits, hg_norm_w[0], cmp_k_pos[0], cmp_k_w1[0], cmp_k_b1[0],
               cmp_k_w2[0], cmp_v_pos[0], cmp_v_w1[0], cmp_v_b1[0], cmp_v_w2[0], w_branch_hg[0],
               w_branch_nsa[0], w_out[0], final_norm_w)
    return y[None]
```

```python
import functools

import numpy as np
import jax
import jax.numpy as jnp
from jax import lax
from jax.experimental import pallas as pl
from jax.experimental.pallas import tpu as pltpu

F32 = jnp.float32
BF16 = jnp.bfloat16

D_MODEL = 2048
HG_HEADS = 8
HG_DK = 128
HG_DV = 128
HG_FDIM = HG_HEADS * HG_DK
HG_WIDTH = HG_HEADS * HG_DV
HG_MM_LEVELS = 3
NSA_HEADS = 16
NSA_KV = 4
NSA_HG = NSA_HEADS // NSA_KV
HEAD_DIM = 128
NSA_WIDTH = NSA_HEADS * HEAD_DIM
NSA_KVW = NSA_KV * HEAD_DIM
CMP_LEN = 32
CMP_STRIDE = 16
CMP_HIDDEN = 512
SEL_LEN = 64
SEL_TOPK = 16
WINDOW = 512
Q_BLOCK = 256
ROPE_THETA = 500000.0
ROT_DIM = HEAD_DIM // 4
ROT_HALF = ROT_DIM // 2
EPS = 1e-6
LOG2E = float(np.log2(np.e))

LANES = 128
NEG_BIG = -1e30
VMEM_LIMIT = 56 * 1024 * 1024

OFF_HG = 0
OFF_Q = 4 * HG_FDIM
OFF_KV = OFF_Q + NSA_WIDTH
OFF_MG = OFF_KV + 6 * NSA_KVW
OFF_Z = OFF_MG + 2 * D_MODEL
OFF_G = OFF_Z + NSA_WIDTH
PROJ_COLS = OFF_G + NSA_KV * LANES
IN_GATE_OFF = OFF_MG
IN_Z_OFF = IN_GATE_OFF + 3 * NSA_HEADS
IN_MG_OFF = IN_Z_OFF + NSA_WIDTH


def _dot(a, b):
    return jnp.dot(a, b, preferred_element_type=F32)


def _dot_nt(a, b):
    return lax.dot_general(a, b, (((1,), (1,)), ((), ())), preferred_element_type=F32)


def _halves(dot, a, b):
    h = a.shape[0] // 2
    return jnp.concatenate([dot(a[:h], b), dot(a[h:], b)], axis=0)


def _split_bf16(a):
    hi = a.astype(BF16)
    lo = (a - hi.astype(F32)).astype(BF16)
    return hi, lo


def _cparams(sem):
    return pltpu.CompilerParams(dimension_semantics=sem, vmem_limit_bytes=VMEM_LIMIT)


def _w_prep_kernel(wt_ref, wg_ref, o_ref, *, n_direct):
    j = pl.program_id(0)

    @pl.when(j < n_direct)
    def _():
        o_ref[...] = wt_ref[...].astype(BF16)

    @pl.when(j >= n_direct)
    def _():
        o_ref[...] = wg_ref[...].astype(BF16)


def _w_prep(wt, tn=512):
    D = wt.shape[1]
    n_main, n_mg, n_z = OFF_MG // tn, 2 * D_MODEL // tn, NSA_WIDTH // tn
    n_direct = n_main + n_mg + n_z
    assert OFF_MG % tn == 0 and D_MODEL % tn == 0 and NSA_KV * LANES == tn
    gates = wt[IN_GATE_OFF:IN_Z_OFF].reshape(NSA_KV, 3 * NSA_HG, D)
    gates = jnp.pad(gates, ((0, 0), (0, LANES - 3 * NSA_HG), (0, 0))).reshape(NSA_KV * LANES, D)

    sub = 8
    assert IN_MG_OFF % sub == 0 and IN_Z_OFF % sub == 0 and tn % sub == 0

    def src_row(j):
        in_mg = IN_MG_OFF // sub + (j - n_main) * (tn // sub)
        in_z = IN_Z_OFF // sub + (jnp.minimum(j, n_direct - 1) - n_main - n_mg) * (tn // sub)
        return jnp.where(j < n_main, j * (tn // sub), jnp.where(j < n_main + n_mg, in_mg, in_z)) * sub

    return pl.pallas_call(
        functools.partial(_w_prep_kernel, n_direct=n_direct),
        grid=(n_direct + 1,),
        in_specs=[
            pl.BlockSpec((pl.Element(tn), pl.Element(D)), lambda j: (src_row(j), 0)),
            pl.BlockSpec((tn, D), lambda j: (0, 0)),
        ],
        out_specs=pl.BlockSpec((tn, D), lambda j: (j, 0)),
        out_shape=jax.ShapeDtypeStruct(((n_direct + 1) * tn, D), BF16),
        compiler_params=_cparams(("arbitrary",)),
        name="w_prep",
    )(wt, gates)


def _norm_proj_kernel(x_ref, nw_ref, w_ref, o_ref, xn_ref):
    @pl.when(pl.program_id(1) == 0)
    def _():
        x = x_ref[...]
        ms = jnp.mean(x * x, axis=-1, keepdims=True)
        xn_ref[...] = (x * lax.rsqrt(ms + EPS) * nw_ref[...]).astype(BF16)

    o_ref[...] = _dot_nt(xn_ref[...], w_ref[...])


def _norm_proj(x2, norm_w, wtb, tm=1024, tn=512):
    S, D = x2.shape
    N = wtb.shape[0]
    tm = min(tm, S)
    return pl.pallas_call(
        _norm_proj_kernel,
        grid=(S // tm, N // tn),
        in_specs=[
            pl.BlockSpec((tm, D), lambda i, j: (i, 0)),
            pl.BlockSpec((1, D), lambda i, j: (0, 0)),
            pl.BlockSpec((tn, D), lambda i, j: (j, 0)),
        ],
        out_specs=pl.BlockSpec((tm, tn), lambda i, j: (i, j)),
        out_shape=jax.ShapeDtypeStruct((S, N), F32),
        scratch_shapes=[pltpu.VMEM((tm, D), BF16)],
        compiler_params=_cparams(("arbitrary", "arbitrary")),
        name="norm_proj",
    )(x2, norm_w.reshape(1, D), wtb)


def _rope(x, cosf, sinf):
    lane = lax.broadcasted_iota(jnp.int32, x.shape, 1)
    rot = jnp.where(lane < ROT_HALF, pltpu.roll(x, LANES - ROT_HALF, 1), pltpu.roll(x, ROT_HALF, 1))
    return x * cosf + rot * sinf


def _rope_tables(S):
    pos = np.arange(S, dtype=np.float64)
    inv = ROPE_THETA ** (-np.arange(0, ROT_DIM, 2, dtype=np.float64) / ROT_DIM)
    ang = pos[:, None] * inv[None, :]
    cos, sin = np.cos(ang), np.sin(ang)
    rest = LANES - ROT_DIM
    cosf = np.concatenate([cos, cos, np.ones((S, rest))], axis=1)
    sinf = np.concatenate([-sin, sin, np.zeros((S, rest))], axis=1)
    return jnp.asarray(cosf, F32), jnp.asarray(sinf, F32)


def _kv_prep_kernel(kv_ref, cos_ref, sin_ref, cmp_ref, kvb_ref, row_ref):
    cosf = cos_ref[...]
    sinf = sin_ref[...]
    W = NSA_KVW
    n_seg = row_ref.shape[0] // CMP_STRIDE

    def to_segments(a, g, rows):
        row_ref[...] = rows
        for l in range(CMP_STRIDE):
            cmp_ref[a, g, :, l * HEAD_DIM:(l + 1) * HEAD_DIM] = row_ref[pl.ds(l, n_seg, stride=CMP_STRIDE), :]

    for g in range(NSA_KV):
        kc = kv_ref[:, 0 * W + g * LANES:0 * W + (g + 1) * LANES]
        to_segments(0, g, _rope(kc, cosf, sinf))
        to_segments(1, g, kv_ref[:, 1 * W + g * LANES:1 * W + (g + 1) * LANES])
        ks = kv_ref[:, 2 * W + g * LANES:2 * W + (g + 1) * LANES]
        kvb_ref[:, 0 * W + g * LANES:0 * W + (g + 1) * LANES] = _rope(ks, cosf, sinf).astype(BF16)
        kvb_ref[:, 1 * W + g * LANES:1 * W + (g + 1) * LANES] = kv_ref[:, 3 * W + g * LANES:3 * W + (g + 1) * LANES].astype(BF16)
        kw = kv_ref[:, 4 * W + g * LANES:4 * W + (g + 1) * LANES]
        kvb_ref[:, 2 * W + g * LANES:2 * W + (g + 1) * LANES] = _rope(kw, cosf, sinf).astype(BF16)
        kvb_ref[:, 3 * W + g * LANES:3 * W + (g + 1) * LANES] = kv_ref[:, 5 * W + g * LANES:5 * W + (g + 1) * LANES].astype(BF16)


def _kv_prep(proj, cosf, sinf, tm=512):
    S = proj.shape[0]
    tm = min(tm, S)
    kvw = 6 * NSA_KVW
    return pl.pallas_call(
        _kv_prep_kernel,
        grid=(S // tm,),
        in_specs=[
            pl.BlockSpec((tm, kvw), lambda i: (i, OFF_KV // kvw)),
            pl.BlockSpec((tm, LANES), lambda i: (i, 0)),
            pl.BlockSpec((tm, LANES), lambda i: (i, 0)),
        ],
        out_specs=[
            pl.BlockSpec((2, NSA_KV, tm // CMP_STRIDE, CMP_STRIDE * HEAD_DIM), lambda i: (0, 0, i, 0)),
            pl.BlockSpec((tm, 4 * NSA_KVW), lambda i: (i, 0)),
        ],
        out_shape=[
            jax.ShapeDtypeStruct((2, NSA_KV, S // CMP_STRIDE, CMP_STRIDE * HEAD_DIM), F32),
            jax.ShapeDtypeStruct((S, 4 * NSA_KVW), BF16),
        ],
        scratch_shapes=[pltpu.VMEM((tm, HEAD_DIM), F32)],
        compiler_params=_cparams(("arbitrary",)),
        name="kv_prep",
    )(proj, cosf, sinf)


def _compress_kernel(seg_ref, pe_ref, w1_ref, b1_ref, w2_ref, o_ref):
    half = (CMP_LEN // 2) * HEAD_DIM
    seg = seg_ref[0, 0]
    n_seg = seg.shape[0]
    pe = pe_ref[0]
    a = (seg + pe[:, :half]).astype(BF16)
    b = (seg + pe[:, half:]).astype(BF16)
    u = _dot(a, w1_ref[0, :half, :])
    v = _dot(b, w1_ref[0, half:, :])
    v_next = pltpu.roll(v, n_seg - 1, 0)
    pre = u + v_next + b1_ref[0]
    h = 0.5 * pre * (1.0 + jnp.tanh(np.sqrt(2.0 / np.pi).astype(np.float32) * (pre + 0.044715 * (pre * pre * pre))))
    o_ref[0, 0] = _dot(h.astype(BF16), w2_ref[0])


def _compress(segs, pe, w1, b1, w2):
    _, G, n_seg, segw = segs.shape
    return pl.pallas_call(
        _compress_kernel,
        grid=(2, G),
        in_specs=[
            pl.BlockSpec((1, 1, n_seg, segw), lambda a, g: (a, g, 0, 0)),
            pl.BlockSpec((1, 1, 2 * segw), lambda a, g: (a, 0, 0)),
            pl.BlockSpec((1, 2 * segw, CMP_HIDDEN), lambda a, g: (a, 0, 0)),
            pl.BlockSpec((1, 1, CMP_HIDDEN), lambda a, g: (a, 0, 0)),
            pl.BlockSpec((1, CMP_HIDDEN, HEAD_DIM), lambda a, g: (a, 0, 0)),
        ],
        out_specs=pl.BlockSpec((1, 1, n_seg, HEAD_DIM), lambda a, g: (a, g, 0, 0)),
        out_shape=jax.ShapeDtypeStruct((2, G, n_seg, HEAD_DIM), F32),
        compiler_params=_cparams(("arbitrary", "arbitrary")),
        name="compress",
    )(segs, pe, w1, b1, w2)


def _hgrn_consts(ch):
    nl = int(np.log2(ch))
    assert 1 << nl == ch
    t = np.arange(ch)[:, None]
    r = np.arange(ch)[None, :]
    mats = [r <= t]
    masks = [np.eye(ch, dtype=bool)]
    for l in range(nl):
        half = 1 << l
        blk = t // (2 * half)
        ref = blk * 2 * half + half - 1
        up = ((t >> l) & 1) == 1
        mats.append(np.where(up, (r > ref) & (r <= t), (r > t) & (r <= ref)))
        masks.append(up & (((r >> l) & 1) == 0) & (blk == r // (2 * half)))
    mc = np.concatenate(mats[:1 + HG_MM_LEVELS], axis=0).astype(np.float32)
    lm = np.stack(masks, axis=0).astype(np.float32)
    return nl, mc, lm


def _hgrn_kernel(q_ref, f_ref, i_ref, z_ref, lbl_ref, nw_ref, mc_ref, lm_ref, o_ref, st_ref, e_ref, *, ch, nl):
    @pl.when(pl.program_id(0) == 0)
    def _():
        st_ref[...] = jnp.zeros_like(st_ref)

    lg = lbl_ref[...]
    ex = jnp.exp(lg - jnp.max(lg, axis=0, keepdims=True))
    lb = ex[0:1] / jnp.sum(ex, axis=0, keepdims=True)
    f = lb + (1.0 - lb) * jax.nn.sigmoid(f_ref[...])
    g_hi, g_lo = _split_bf16(jnp.log(f))
    mc = mc_ref[...]
    e_ref[...] = _dot(mc, g_hi) + _dot(mc, g_lo)
    row = lax.broadcasted_iota(jnp.int32, (ch, 1), 0)
    nw = nw_ref[...]

    heads = [slice(h * HG_DK, (h + 1) * HG_DK) for h in range(HG_HEADS)]
    qs = [q_ref[:, sl] for sl in heads]
    ks = [1.0 - f[:, sl] for sl in heads]
    scs = [_dot_nt(q.astype(BF16), k.astype(BF16)) * lm_ref[0] for q, k in zip(qs, ks)]
    bs = [e_ref[0:ch, sl] for sl in heads]
    for l in range(nl):
        up = ((row >> l) & 1) == 1
        half = 1 << l
        for h, sl in enumerate(heads):
            if l < HG_MM_LEVELS:
                el = e_ref[(l + 1) * ch:(l + 2) * ch, sl]
                xl = jnp.where(up, qs[h], ks[h]) * jnp.exp(el)
            else:
                parts = []
                for r0 in range(0, ch, 2 * half):
                    mid = r0 + half
                    edge = bs[h][mid - 1:mid, :]
                    parts.append(ks[h][r0:mid] * jnp.exp(edge - bs[h][r0:mid]))
                    parts.append(qs[h][mid:mid + half] * jnp.exp(bs[h][mid:mid + half] - edge))
                xl = jnp.concatenate(parts, axis=0)
            xl = xl.astype(BF16)
            scs[h] = scs[h] + _dot_nt(xl, xl) * lm_ref[l + 1]

    vbs = [i_ref[:, sl].astype(BF16) for sl in heads]
    sts = [st_ref[h] for h in range(HG_HEADS)]
    inters = [_dot_nt((qs[h] * jnp.exp(bs[h])).astype(BF16), sts[h].astype(BF16)) for h in range(HG_HEADS)]
    for h, sl in enumerate(heads):
        ke = (ks[h] * jnp.exp(bs[h][ch - 1:ch, :] - bs[h])).astype(BF16)
        upd = lax.dot_general(vbs[h], ke, (((0,), (0,)), ((), ())), preferred_element_type=F32)
        st_ref[h] = sts[h] * jnp.exp(bs[h][ch - 1:ch, :]) + upd
    outs = [inters[h] + _dot(scs[h].astype(BF16), vbs[h]) for h in range(HG_HEADS)]
    for h, sl in enumerate(heads):
        o = outs[h]
        ms = jnp.mean(o * o, axis=-1, keepdims=True)
        z = z_ref[:, sl]
        o_ref[:, sl] = (o * lax.rsqrt(ms + EPS) * nw * (z * jax.nn.sigmoid(z))).astype(o_ref.dtype)


def _hgrn(proj, lb_logits, hg_norm_w, ch=128):
    S = proj.shape[0]
    nl, mc, lm = _hgrn_consts(ch)
    kern = functools.partial(_hgrn_kernel, ch=ch, nl=nl)
    nlb = lb_logits.shape[0]
    return pl.pallas_call(
        kern,
        grid=(S // ch,),
        in_specs=[
            pl.BlockSpec((ch, HG_FDIM), lambda c: (c, 0)),
            pl.BlockSpec((ch, HG_FDIM), lambda c: (c, 1)),
            pl.BlockSpec((ch, HG_WIDTH), lambda c: (c, 2)),
            pl.BlockSpec((ch, HG_WIDTH), lambda c: (c, 3)),
            pl.BlockSpec((nlb, HG_FDIM), lambda c: (0, 0)),
            pl.BlockSpec((1, HG_DV), lambda c: (0, 0)),
            pl.BlockSpec(mc.shape, lambda c: (0, 0)),
            pl.BlockSpec(lm.shape, lambda c: (0, 0, 0)),
        ],
        out_specs=pl.BlockSpec((ch, HG_WIDTH), lambda c: (c, 0)),
        out_shape=jax.ShapeDtypeStruct((S, HG_WIDTH), BF16),
        scratch_shapes=[
            pltpu.VMEM((HG_HEADS, HG_DV, HG_DK), F32),
            pltpu.VMEM((mc.shape[0], HG_FDIM), F32),
        ],
        compiler_params=_cparams(("arbitrary",)),
        name="hgrn",
    )(proj, proj, proj, proj, lb_logits, hg_norm_w.reshape(1, HG_DV), jnp.asarray(mc, BF16), jnp.asarray(lm, F32))


def _stack_heads(q_ref, rows, cosf, sinf):
    qscale = (HEAD_DIM ** -0.5) * LOG2E
    return jnp.concatenate(
        [(_rope(q_ref[rows, h * HEAD_DIM:(h + 1) * HEAD_DIM], cosf, sinf) * qscale).astype(BF16)
         for h in range(NSA_HG)], axis=0)


def _per_head(a):
    return jnp.concatenate([a] * NSA_HG, axis=0)


def _nsa_select_kernel(q_ref, cos_ref, sin_ref, gate_ref, kc_ref, vc_ref, ov_ref, ca_ref, selb_ref, pc_ref, *, nsub):
    n = pl.program_id(1)
    R = NSA_HG * Q_BLOCK
    ncp = kc_ref.shape[2]
    nb = Q_BLOCK // CMP_STRIDE
    subs = range(nsub)
    rows = [slice(i * Q_BLOCK, (i + 1) * Q_BLOCK) for i in subs]
    q0s = [(n * nsub + i) * Q_BLOCK for i in subs]

    crow = lax.broadcasted_iota(jnp.int32, (ncp, LANES), 0)
    clane = lax.broadcasted_iota(jnp.int32, (ncp, LANES), 1)
    kc = kc_ref[0, 0].astype(BF16)
    vca = jnp.concatenate([vc_ref[0, 0].astype(BF16), jnp.ones((ncp, LANES), BF16)], axis=1)
    ca4 = _per_head(ca_ref[...])
    ov = ov_ref[...]

    ss = []
    for i in subs:
        q4 = _stack_heads(q_ref, rows[i], cos_ref[rows[i], :], sin_ref[rows[i], :])
        c_first = (n * nsub + i) * nb - (CMP_LEN // CMP_STRIDE - 1)
        flags = jnp.where(clane == 0, jnp.where(crow >= c_first + nb, NEG_BIG, 0.0),
                          jnp.where(crow == c_first + clane - 1, 1.0, 0.0)).astype(BF16)
        ss.append(_halves(_dot_nt, jnp.concatenate([q4, ca4], axis=1), jnp.concatenate([kc, flags], axis=1)))
    es = [jnp.exp2(s - jnp.max(s, axis=-1, keepdims=True)) for s in ss]
    pvs = [_halves(_dot, e.astype(BF16), vca) for e in es]

    imps = []
    for i in subs:
        t_row = q0s[i] + (lax.broadcasted_iota(jnp.int32, (R, LANES), 0) & (Q_BLOCK - 1))
        inv = jnp.where(t_row >= CMP_LEN - 1, 1.0, 0.0) / pvs[i][:, LANES:]
        o_cmp = pvs[i][:, :LANES] * inv
        gate = jax.nn.sigmoid(gate_ref[rows[i], :])
        for h in range(NSA_HG):
            pc_ref[rows[i], h * HEAD_DIM:(h + 1) * HEAD_DIM] = (
                gate[:, 3 * h:3 * h + 1] * o_cmp[h * Q_BLOCK:(h + 1) * Q_BLOCK])
        p = es[i] * jnp.concatenate([inv] * (ncp // LANES), axis=1)
        p_sum = p[0:Q_BLOCK]
        for h in range(1, NSA_HG):
            p_sum = p_sum + p[h * Q_BLOCK:(h + 1) * Q_BLOCK]
        ps_hi, ps_lo = _split_bf16(p_sum)
        imps.append(_dot(ps_hi, ov) + _dot(ps_lo, ov))

    nsel_pad = ov.shape[1]
    jj = lax.broadcasted_iota(jnp.int32, (nsel_pad, Q_BLOCK), 0)
    jjf = jj.astype(F32)
    lane_q = lax.broadcasted_iota(jnp.int32, (nsel_pad, Q_BLOCK), 1)
    TAKEN = -2.0
    scores = []
    for i in subs:
        tq = q0s[i] + lane_q
        jt = tq >> 6
        forced = (jj == 0) | (jj == jt) | (jj == jt - 1)
        scores.append(jnp.where(jj * SEL_LEN <= tq, jnp.where(forced, TAKEN, imps[i].T), -1.0))
    for _ in range(SEL_TOPK - 3):
        for i in subs:
            mx = jnp.max(scores[i], axis=0, keepdims=True)
            first = jnp.min(jnp.where(scores[i] == mx, jjf, float(nsel_pad)), axis=0, keepdims=True)
            first = jnp.where(mx >= 0.0, first, -1.0)
            scores[i] = jnp.where(jjf == first, TAKEN, scores[i])
    for i in subs:
        selb_ref[rows[i], :] = jnp.where(scores[i] == TAKEN, 0.0, NEG_BIG).T.astype(BF16)


def _nsa_select(proj, cosf, sinf, kcvc, nsub=4):
    S = proj.shape[0]
    ncp = kcvc.shape[2]
    nsel = S // SEL_LEN
    qs = min(nsub * Q_BLOCK, S)
    nsub = qs // Q_BLOCK
    assert nsel <= LANES and ncp % LANES == 0 and S % qs == 0
    assert CMP_LEN == 2 * CMP_STRIDE and Q_BLOCK % CMP_STRIDE == 0 and Q_BLOCK // CMP_STRIDE < LANES
    ov = jnp.asarray(_overlap_matrix(ncp, LANES, nsel), BF16)
    ca = jnp.asarray(_cmp_staircase(), BF16)
    qw = NSA_HG * HEAD_DIM
    return pl.pallas_call(
        functools.partial(_nsa_select_kernel, nsub=nsub),
        grid=(NSA_KV, S // qs),
        in_specs=[
            pl.BlockSpec((qs, qw), lambda g, n: (n, OFF_Q // qw + g)),
            pl.BlockSpec((qs, LANES), lambda g, n: (n, 0)),
            pl.BlockSpec((qs, LANES), lambda g, n: (n, 0)),
            pl.BlockSpec((qs, LANES), lambda g, n: (n, OFF_G // LANES + g)),
            pl.BlockSpec((1, 1, ncp, HEAD_DIM), lambda g, n: (0, g, 0, 0)),
            pl.BlockSpec((1, 1, ncp, HEAD_DIM), lambda g, n: (1, g, 0, 0)),
            pl.BlockSpec((ncp, LANES), lambda g, n: (0, 0)),
            pl.BlockSpec((Q_BLOCK, LANES), lambda g, n: (0, 0)),
        ],
        out_specs=[
            pl.BlockSpec((qs, LANES), lambda g, n: (n, g)),
            pl.BlockSpec((qs, qw), lambda g, n: (n, g)),
        ],
        out_shape=[
            jax.ShapeDtypeStruct((S, NSA_KV * LANES), BF16),
            jax.ShapeDtypeStruct((S, NSA_WIDTH), F32),
        ],
        compiler_params=_cparams(("arbitrary", "arbitrary")),
        name="nsa_select",
    )(proj, cosf, sinf, proj, kcvc, kcvc, ov, ca)


def _nsa_kernel(q_ref, cos_ref, sin_ref, gate_ref, z_ref, selb_ref, pc_ref, wb_ref, cb_ref,
                ks_ref, vs_ref, kw_ref, vw_ref, o_ref, m_ref, accl_ref, sa_ref, sb_ref, part_ref, *, tk):
    n = pl.program_id(1)
    q0 = n * Q_BLOCK
    R = NSA_HG * Q_BLOCK
    q4 = _stack_heads(q_ref, slice(None), cos_ref[...], sin_ref[...])
    per_head = _per_head

    def tpos(shape):
        return q0 + (lax.broadcasted_iota(jnp.int32, shape, 0) & (Q_BLOCK - 1))

    wspan = WINDOW + Q_BLOCK
    w0 = pl.multiple_of(jnp.maximum(q0 - WINDOW, 0), Q_BLOCK)
    sw = (_halves(_dot_nt, q4, kw_ref[pl.ds(w0, wspan), :])
          + per_head(wb_ref[jnp.minimum(n, WINDOW // Q_BLOCK)]))
    ew = jnp.exp2(sw - jnp.max(sw, axis=-1, keepdims=True))
    vwa = jnp.concatenate([vw_ref[pl.ds(w0, wspan), :], jnp.ones((wspan, LANES), BF16)], axis=1)
    pvw = _halves(_dot, ew.astype(BF16), vwa)
    o_win = pvw[:, :LANES] / pvw[:, LANES:]

    gate = jax.nn.sigmoid(gate_ref[...])
    for h in range(NSA_HG):
        cs = slice(h * HEAD_DIM, (h + 1) * HEAD_DIM)
        part_ref[:, cs] = pc_ref[:, cs] + gate[:, 3 * h + 2:3 * h + 3] * o_win[h * Q_BLOCK:(h + 1) * Q_BLOCK]

    sel_bias = per_head(selb_ref[...])

    m_ref[...] = jnp.full(m_ref.shape, NEG_BIG, F32)
    accl_ref[...] = jnp.zeros(accl_ref.shape, F32)
    qa = jnp.concatenate([q4, sel_bias], axis=1)
    key_blk = lax.broadcasted_iota(jnp.int32, (tk, LANES), 0) >> 6
    blk_lane = lax.broadcasted_iota(jnp.int32, (tk, LANES), 1)
    ones_k = jnp.ones((tk, LANES), BF16)
    rep = tk // LANES

    n_q = tk // Q_BLOCK
    last = lax.shift_right_logical(n, n_q.bit_length() - 1)
    diag = n - last * n_q

    def score_tile(kt, dst_ref):
        k0 = pl.multiple_of(kt * tk, tk)
        onehot = jnp.where(key_blk + kt * (tk // SEL_LEN) == blk_lane, 1.0, 0.0).astype(BF16)
        dst_ref[...] = _dot_nt(qa, jnp.concatenate([ks_ref[pl.ds(k0, tk), :], onehot], axis=1))

    def consume_tile(kt, src_ref):
        k0 = pl.multiple_of(kt * tk, tk)
        sc = src_ref[...] + per_head(cb_ref[jnp.where(kt == last, diag, n_q)])
        m_prev = m_ref[...]
        m_next = jnp.maximum(m_prev, jnp.max(sc, axis=-1, keepdims=True))
        pr = jnp.exp2(sc - jnp.concatenate([m_next] * rep, axis=1))
        alpha = jnp.exp2(m_prev - m_next)
        va = jnp.concatenate([vs_ref[pl.ds(k0, tk), :], ones_k], axis=1)
        accl_ref[...] = jnp.concatenate([alpha, alpha], axis=1) * accl_ref[...] + _dot(pr.astype(BF16), va)
        m_ref[...] = m_next

    score_tile(0, sa_ref)

    def slc_pair(i, carry):
        score_tile(2 * i + 1, sb_ref)
        consume_tile(2 * i, sa_ref)
        score_tile(2 * i + 2, sa_ref)
        consume_tile(2 * i + 1, sb_ref)
        return carry

    lax.fori_loop(0, lax.shift_right_logical(last, 1), slc_pair, 0)

    @pl.when((last & 1) == 0)
    def _():
        consume_tile(last, sa_ref)

    @pl.when((last & 1) == 1)
    def _():
        score_tile(last, sb_ref)
        consume_tile(last - 1, sa_ref)
        consume_tile(last, sb_ref)

    o_slc = accl_ref[:, :LANES] / accl_ref[:, LANES:]

    gsig = jax.nn.sigmoid(gate_ref[...])
    for h in range(NSA_HG):
        rs = slice(h * Q_BLOCK, (h + 1) * Q_BLOCK)
        cs = slice(h * HEAD_DIM, (h + 1) * HEAD_DIM)
        oh = part_ref[:, cs] + gsig[:, 3 * h + 1:3 * h + 2] * o_slc[rs]
        z = z_ref[:, cs]
        o_ref[:, cs] = (oh * (z * jax.nn.sigmoid(z))).astype(o_ref.dtype)


def _cmp_staircase():
    nb = Q_BLOCK // CMP_STRIDE
    a = np.zeros((Q_BLOCK, LANES), np.float32)
    a[:, 0] = 1.0
    r = np.arange(Q_BLOCK)[:, None]
    i = np.arange(nb)[None, :]
    a[:, 1:1 + nb] = np.where(r < CMP_STRIDE * i + (CMP_STRIDE - 1), NEG_BIG, 0.0)
    return a


def _window_masks():
    n_w = WINDOW // Q_BLOCK
    r = np.arange(Q_BLOCK)[None, :, None]
    c = np.arange(WINDOW + Q_BLOCK)[None, None, :]
    d = np.arange(n_w + 1)[:, None, None]
    ok = np.where(d < n_w, c <= Q_BLOCK * d + r, (c > r) & (c <= WINDOW + r))
    return np.where(ok, 0.0, NEG_BIG).astype(np.float32)


def _causal_staircases(tk):
    n_q = tk // Q_BLOCK
    r = np.arange(Q_BLOCK)[None, :, None]
    c = np.arange(tk)[None, None, :]
    d = np.arange(n_q + 1)[:, None, None]
    return np.where((c <= Q_BLOCK * d + r) | (d == n_q), 0.0, NEG_BIG).astype(np.float32)


def _overlap_matrix(ncp, nsel_pad, nsel):
    ci = np.arange(ncp)[:, None] * CMP_STRIDE
    sj = np.arange(nsel_pad)[None, :] * SEL_LEN
    ov = (ci < sj + SEL_LEN) & (ci + CMP_LEN > sj) & (np.arange(nsel_pad)[None, :] < nsel) & (np.arange(ncp)[:, None] < ncp - 1)
    return ov.astype(np.float32)


def _nsa(proj, cosf, sinf, selb, pcmp, kvb, tk=512):
    S = proj.shape[0]
    assert S % tk == 0 and S >= WINDOW + Q_BLOCK and WINDOW % Q_BLOCK == 0
    wb = jnp.asarray(_window_masks(), F32)
    n_q = tk // Q_BLOCK
    assert n_q & (n_q - 1) == 0 and tk % SEL_LEN == 0
    cb = jnp.asarray(_causal_staircases(tk), F32)
    qw = NSA_HG * HEAD_DIM
    R = NSA_HG * Q_BLOCK
    kern = functools.partial(_nsa_kernel, tk=tk)
    return pl.pallas_call(
        kern,
        grid=(NSA_KV, S // Q_BLOCK),
        in_specs=[
            pl.BlockSpec((Q_BLOCK, qw), lambda g, n: (n, OFF_Q // qw + g)),
            pl.BlockSpec((Q_BLOCK, LANES), lambda g, n: (n, 0)),
            pl.BlockSpec((Q_BLOCK, LANES), lambda g, n: (n, 0)),
            pl.BlockSpec((Q_BLOCK, LANES), lambda g, n: (n, OFF_G // LANES + g)),
            pl.BlockSpec((Q_BLOCK, qw), lambda g, n: (n, OFF_Z // qw + g)),
            pl.BlockSpec((Q_BLOCK, LANES), lambda g, n: (n, g)),
            pl.BlockSpec((Q_BLOCK, qw), lambda g, n: (n, g)),
            pl.BlockSpec((WINDOW // Q_BLOCK + 1, Q_BLOCK, WINDOW + Q_BLOCK), lambda g, n: (0, 0, 0)),
            pl.BlockSpec((n_q + 1, Q_BLOCK, tk), lambda g, n: (0, 0, 0)),
            pl.BlockSpec((S, HEAD_DIM), lambda g, n: (0, 0 * NSA_KV + g)),
            pl.BlockSpec((S, HEAD_DIM), lambda g, n: (0, 1 * NSA_KV + g)),
            pl.BlockSpec((S, HEAD_DIM), lambda g, n: (0, 2 * NSA_KV + g)),
            pl.BlockSpec((S, HEAD_DIM), lambda g, n: (0, 3 * NSA_KV + g)),
        ],
        out_specs=pl.BlockSpec((Q_BLOCK, qw), lambda g, n: (n, g)),
        out_shape=jax.ShapeDtypeStruct((S, NSA_WIDTH), BF16),
        scratch_shapes=[
            pltpu.VMEM((R, LANES), F32),
            pltpu.VMEM((R, HEAD_DIM + LANES), F32),
            pltpu.VMEM((R, tk), F32),
            pltpu.VMEM((R, tk), F32),
            pltpu.VMEM((Q_BLOCK, qw), F32),
        ],
        compiler_params=_cparams(("arbitrary", "arbitrary")),
        name="nsa",
    )(proj, cosf, sinf, proj, proj, selb, pcmp, wb, cb, kvb, kvb, kvb, kvb)


def _merge_kernel(ohg_ref, onsa_ref, g0_ref, g1_ref, g2_ref, g3_ref, whg_ref, wnsa_ref, o_ref):
    y_hg = _dot(ohg_ref[...], whg_ref[...])
    y_nsa = _dot(onsa_ref[...], wnsa_ref[...])
    half = D_MODEL // 2
    gh = (g0_ref, g1_ref)
    gn = (g2_ref, g3_ref)
    for c in range(2):
        cs = slice(c * half, (c + 1) * half)
        m = jax.nn.sigmoid(gh[c][...]) * y_hg[:, cs] + jax.nn.sigmoid(gn[c][...]) * y_nsa[:, cs]
        o_ref[:, cs] = m.astype(o_ref.dtype)


def _merge(o_hg, o_nsa, proj, w_hg, w_nsa, tm=256):
    S = o_hg.shape[0]
    tm = min(tm, S)
    half = D_MODEL // 2
    gb = OFF_MG // half
    return pl.pallas_call(
        _merge_kernel,
        grid=(S // tm,),
        in_specs=[
            pl.BlockSpec((tm, HG_WIDTH), lambda i: (i, 0)),
            pl.BlockSpec((tm, NSA_WIDTH), lambda i: (i, 0)),
            pl.BlockSpec((tm, half), lambda i: (i, gb + 0)),
            pl.BlockSpec((tm, half), lambda i: (i, gb + 1)),
            pl.BlockSpec((tm, half), lambda i: (i, gb + 2)),
            pl.BlockSpec((tm, half), lambda i: (i, gb + 3)),
            pl.BlockSpec((HG_WIDTH, D_MODEL), lambda i: (0, 0)),
            pl.BlockSpec((NSA_WIDTH, D_MODEL), lambda i: (0, 0)),
        ],
        out_specs=pl.BlockSpec((tm, D_MODEL), lambda i: (i, 0)),
        out_shape=jax.ShapeDtypeStruct((S, D_MODEL), BF16),
        compiler_params=_cparams(("arbitrary",)),
        name="merge",
    )(o_hg, o_nsa, proj, proj, proj, proj, w_hg, w_nsa)


def _out_kernel(x_ref, m_ref, w_ref, nw_ref, o_ref):
    h = x_ref[...] + _dot(m_ref[...], w_ref[...])
    ms = jnp.mean(h * h, axis=-1, keepdims=True)
    o_ref[...] = h * lax.rsqrt(ms + EPS) * nw_ref[...]


def _out(x2, merged, w_out, final_w, tm=256):
    S, D = x2.shape
    tm = min(tm, S)
    return pl.pallas_call(
        _out_kernel,
        grid=(S // tm,),
        in_specs=[
            pl.BlockSpec((tm, D), lambda i: (i, 0)),
            pl.BlockSpec((tm, D), lambda i: (i, 0)),
            pl.BlockSpec((D, D), lambda i: (0, 0)),
            pl.BlockSpec((1, D), lambda i: (0, 0)),
        ],
        out_specs=pl.BlockSpec((tm, D), lambda i: (i, 0)),
        out_shape=jax.ShapeDtypeStruct((S, D), F32),
        compiler_params=_cparams(("arbitrary",)),
        name="out_proj",
    )(x2, merged, w_out, final_w.reshape(1, D))


def _layer(x2, norm_w, w_in, lb_logits, hg_norm_w, cmp_k_pos, cmp_k_w1, cmp_k_b1, cmp_k_w2,
           cmp_v_pos, cmp_v_w1, cmp_v_b1, cmp_v_w2, w_branch_hg, w_branch_nsa, w_out, final_w):
    S = x2.shape[0]
    proj = _norm_proj(x2, norm_w, _w_prep(w_in.T))
    cosf, sinf = _rope_tables(S)
    segs, kvb = _kv_prep(proj, cosf, sinf)
    pe = jnp.stack([cmp_k_pos.reshape(1, -1), cmp_v_pos.reshape(1, -1)])
    w1 = jnp.stack([cmp_k_w1, cmp_v_w1]).astype(BF16)
    b1 = jnp.stack([cmp_k_b1.reshape(1, -1), cmp_v_b1.reshape(1, -1)])
    w2 = jnp.stack([cmp_k_w2, cmp_v_w2]).astype(BF16)
    kcvc = _compress(segs, pe, w1, b1, w2)
    o_hg = _hgrn(proj, lb_logits, hg_norm_w)
    selb, pcmp = _nsa_select(proj, cosf, sinf, kcvc)
    o_nsa = _nsa(proj, cosf, sinf, selb, pcmp, kvb)
    merged = _merge(o_hg, o_nsa, proj, w_branch_hg.astype(BF16), w_branch_nsa.astype(BF16))
    return _out(x2, merged, w_out.astype(BF16), final_w)


def kernel(x, norm_w, w_in, hg_lb_logits, hg_norm_w, cmp_k_pos, cmp_k_w1, cmp_k_b1, cmp_k_w2, cmp_v_pos, cmp_v_w1, cmp_v_b1, cmp_v_w2, w_branch_hg, w_branch_nsa, w_out, final_norm_w):
    B, S, D = x.shape
    assert B == 1 and D == D_MODEL and norm_w.shape[0] == 1
    y = _layer(x[0], norm_w[0], w_in[0], hg_lb_logits, hg_norm_w[0], cmp_k_pos[0], cmp_k_w1[0], cmp_k_b1[0],
               cmp_k_w2[0], cmp_v_pos[0], cmp_v_w1[0], cmp_v_b1[0], cmp_v_w2[0], w_branch_hg[0],
               w_branch_nsa[0], w_out[0], final_norm_w)
    return y[None]
```

```python
import functools

import numpy as np
import jax
import jax.numpy as jnp
from jax import lax
from jax.experimental import pallas as pl
from jax.experimental.pallas import tpu as pltpu

F32 = jnp.float32
BF16 = jnp.bfloat16

D_MODEL = 2048
HG_HEADS = 8
HG_DK = 128
HG_DV = 128
HG_FDIM = HG_HEADS * HG_DK
HG_WIDTH = HG_HEADS * HG_DV
HG_MM_LEVELS = 3
NSA_HEADS = 16
NSA_KV = 4
NSA_HG = NSA_HEADS // NSA_KV
HEAD_DIM = 128
NSA_WIDTH = NSA_HEADS * HEAD_DIM
NSA_KVW = NSA_KV * HEAD_DIM
CMP_LEN = 32
CMP_STRIDE = 16
CMP_HIDDEN = 512
SEL_LEN = 64
SEL_TOPK = 16
WINDOW = 512
Q_BLOCK = 256
ROPE_THETA = 500000.0
ROT_DIM = HEAD_DIM // 4
ROT_HALF = ROT_DIM // 2
EPS = 1e-6
LOG2E = float(np.log2(np.e))

LANES = 128
NEG_BIG = -1e30
VMEM_LIMIT = 56 * 1024 * 1024

OFF_HG = 0
OFF_Q = 4 * HG_FDIM
OFF_KV = OFF_Q + NSA_WIDTH
OFF_MG = OFF_KV + 6 * NSA_KVW
OFF_Z = OFF_MG + 2 * D_MODEL
OFF_G = OFF_Z + NSA_WIDTH
PROJ_COLS = OFF_G + NSA_KV * LANES
IN_GATE_OFF = OFF_MG
IN_Z_OFF = IN_GATE_OFF + 3 * NSA_HEADS
IN_MG_OFF = IN_Z_OFF + NSA_WIDTH


def _dot(a, b):
    return jnp.dot(a, b, preferred_element_type=F32)


def _dot_nt(a, b):
    return lax.dot_general(a, b, (((1,), (1,)), ((), ())), preferred_element_type=F32)


def _halves(dot, a, b):
    h = a.shape[0] // 2
    return jnp.concatenate([dot(a[:h], b), dot(a[h:], b)], axis=0)


def _split_bf16(a):
    hi = a.astype(BF16)
    lo = (a - hi.astype(F32)).astype(BF16)
    return hi, lo


def _cparams(sem):
    return pltpu.CompilerParams(dimension_semantics=sem, vmem_limit_bytes=VMEM_LIMIT)


def _w_prep_kernel(wt_ref, wg_ref, o_ref, *, n_direct):
    j = pl.program_id(0)

    @pl.when(j < n_direct)
    def _():
        o_ref[...] = wt_ref[...].astype(BF16)

    @pl.when(j >= n_direct)
    def _():
        o_ref[...] = wg_ref[...].astype(BF16)


def _w_prep(wt, tn=512):
    D = wt.shape[1]
    n_main, n_mg, n_z = OFF_MG // tn, 2 * D_MODEL // tn, NSA_WIDTH // tn
    n_direct = n_main + n_mg + n_z
    assert OFF_MG % tn == 0 and D_MODEL % tn == 0 and NSA_KV * LANES == tn
    gates = wt[IN_GATE_OFF:IN_Z_OFF].reshape(NSA_KV, 3 * NSA_HG, D)
    gates = jnp.pad(gates, ((0, 0), (0, LANES - 3 * NSA_HG), (0, 0))).reshape(NSA_KV * LANES, D)

    sub = 8
    assert IN_MG_OFF % sub == 0 and IN_Z_OFF % sub == 0 and tn % sub == 0

    def src_row(j):
        in_mg = IN_MG_OFF // sub + (j - n_main) * (tn // sub)
        in_z = IN_Z_OFF // sub + (jnp.minimum(j, n_direct - 1) - n_main - n_mg) * (tn // sub)
        return jnp.where(j < n_main, j * (tn // sub), jnp.where(j < n_main + n_mg, in_mg, in_z)) * sub

    return pl.pallas_call(
        functools.partial(_w_prep_kernel, n_direct=n_direct),
        grid=(n_direct + 1,),
        in_specs=[
            pl.BlockSpec((pl.Element(tn), pl.Element(D)), lambda j: (src_row(j), 0)),
            pl.BlockSpec((tn, D), lambda j: (0, 0)),
        ],
        out_specs=pl.BlockSpec((tn, D), lambda j: (j, 0)),
        out_shape=jax.ShapeDtypeStruct(((n_direct + 1) * tn, D), BF16),
        compiler_params=_cparams(("arbitrary",)),
        name="w_prep",
    )(wt, gates)


def _norm_proj_kernel(x_ref, nw_ref, w_ref, o_ref, xn_ref):
    @pl.when(pl.program_id(1) == 0)
    def _():
        x = x_ref[...]
        ms = jnp.mean(x * x, axis=-1, keepdims=True)
        xn_ref[...] = (x * lax.rsqrt(ms + EPS) * nw_ref[...]).astype(BF16)

    o_ref[...] = _dot_nt(xn_ref[...], w_ref[...])


def _norm_proj(x2, norm_w, wtb, tm=1024, tn=512):
    S, D = x2.shape
    N = wtb.shape[0]
    tm = min(tm, S)
    return pl.pallas_call(
        _norm_proj_kernel,
        grid=(S // tm, N // tn),
        in_specs=[
            pl.BlockSpec((tm, D), lambda i, j: (i, 0)),
            pl.BlockSpec((1, D), lambda i, j: (0, 0)),
            pl.BlockSpec((tn, D), lambda i, j: (j, 0)),
        ],
        out_specs=pl.BlockSpec((tm, tn), lambda i, j: (i, j)),
        out_shape=jax.ShapeDtypeStruct((S, N), F32),
        scratch_shapes=[pltpu.VMEM((tm, D), BF16)],
        compiler_params=_cparams(("arbitrary", "arbitrary")),
        name="norm_proj",
    )(x2, norm_w.reshape(1, D), wtb)


def _rope(x, cosf, sinf):
    lane = lax.broadcasted_iota(jnp.int32, x.shape, 1)
    rot = jnp.where(lane < ROT_HALF, pltpu.roll(x, LANES - ROT_HALF, 1), pltpu.roll(x, ROT_HALF, 1))
    return x * cosf + rot * sinf


def _rope_tables(S):
    pos = np.arange(S, dtype=np.float64)
    inv = ROPE_THETA ** (-np.arange(0, ROT_DIM, 2, dtype=np.float64) / ROT_DIM)
    ang = pos[:, None] * inv[None, :]
    cos, sin = np.cos(ang), np.sin(ang)
    rest = LANES - ROT_DIM
    cosf = np.concatenate([cos, cos, np.ones((S, rest))], axis=1)
    sinf = np.concatenate([-sin, sin, np.zeros((S, rest))], axis=1)
    return jnp.asarray(cosf, F32), jnp.asarray(sinf, F32)


def _kv_prep_kernel(kv_ref, cos_ref, sin_ref, cmp_ref, kvb_ref, row_ref):
    cosf = cos_ref[...]
    sinf = sin_ref[...]
    W = NSA_KVW
    n_seg = row_ref.shape[0] // CMP_STRIDE

    def to_segments(a, g, rows):
        row_ref[...] = rows
        for l in range(CMP_STRIDE):
            cmp_ref[a, g, :, l * HEAD_DIM:(l + 1) * HEAD_DIM] = row_ref[pl.ds(l, n_seg, stride=CMP_STRIDE), :]

    for g in range(NSA_KV):
        kc = kv_ref[:, 0 * W + g * LANES:0 * W + (g + 1) * LANES]
        to_segments(0, g, _rope(kc, cosf, sinf))
        to_segments(1, g, kv_ref[:, 1 * W + g * LANES:1 * W + (g + 1) * LANES])
        ks = kv_ref[:, 2 * W + g * LANES:2 * W + (g + 1) * LANES]
        kvb_ref[:, 0 * W + g * LANES:0 * W + (g + 1) * LANES] = _rope(ks, cosf, sinf).astype(BF16)
        kvb_ref[:, 1 * W + g * LANES:1 * W + (g + 1) * LANES] = kv_ref[:, 3 * W + g * LANES:3 * W + (g + 1) * LANES].astype(BF16)
        kw = kv_ref[:, 4 * W + g * LANES:4 * W + (g + 1) * LANES]
        kvb_ref[:, 2 * W + g * LANES:2 * W + (g + 1) * LANES] = _rope(kw, cosf, sinf).astype(BF16)
        kvb_ref[:, 3 * W + g * LANES:3 * W + (g + 1) * LANES] = kv_ref[:, 5 * W + g * LANES:5 * W + (g + 1) * LANES].astype(BF16)


def _kv_prep(proj, cosf, sinf, tm=512):
    S = proj.shape[0]
    tm = min(tm, S)
    kvw = 6 * NSA_KVW
    return pl.pallas_call(
        _kv_prep_kernel,
        grid=(S // tm,),
        in_specs=[
            pl.BlockSpec((tm, kvw), lambda i: (i, OFF_KV // kvw)),
            pl.BlockSpec((tm, LANES), lambda i: (i, 0)),
            pl.BlockSpec((tm, LANES), lambda i: (i, 0)),
        ],
        out_specs=[
            pl.BlockSpec((2, NSA_KV, tm // CMP_STRIDE, CMP_STRIDE * HEAD_DIM), lambda i: (0, 0, i, 0)),
            pl.BlockSpec((tm, 4 * NSA_KVW), lambda i: (i, 0)),
        ],
        out_shape=[
            jax.ShapeDtypeStruct((2, NSA_KV, S // CMP_STRIDE, CMP_STRIDE * HEAD_DIM), F32),
            jax.ShapeDtypeStruct((S, 4 * NSA_KVW), BF16),
        ],
        scratch_shapes=[pltpu.VMEM((tm, HEAD_DIM), F32)],
        compiler_params=_cparams(("arbitrary",)),
        name="kv_prep",
    )(proj, cosf, sinf)


def _compress_kernel(seg_ref, pe_ref, w1_ref, b1_ref, w2_ref, o_ref):
    half = (CMP_LEN // 2) * HEAD_DIM
    seg = seg_ref[0, 0]
    n_seg = seg.shape[0]
    pe = pe_ref[0]
    a = (seg + pe[:, :half]).astype(BF16)
    b = (seg + pe[:, half:]).astype(BF16)
    u = _dot(a, w1_ref[0, :half, :])
    v = _dot(b, w1_ref[0, half:, :])
    v_next = pltpu.roll(v, n_seg - 1, 0)
    pre = u + v_next + b1_ref[0]
    h = 0.5 * pre * (1.0 + jnp.tanh(np.sqrt(2.0 / np.pi).astype(np.float32) * (pre + 0.044715 * (pre * pre * pre))))
    o_ref[0, 0] = _dot(h.astype(BF16), w2_ref[0])


def _compress(segs, pe, w1, b1, w2):
    _, G, n_seg, segw = segs.shape
    return pl.pallas_call(
        _compress_kernel,
        grid=(2, G),
        in_specs=[
            pl.BlockSpec((1, 1, n_seg, segw), lambda a, g: (a, g, 0, 0)),
            pl.BlockSpec((1, 1, 2 * segw), lambda a, g: (a, 0, 0)),
            pl.BlockSpec((1, 2 * segw, CMP_HIDDEN), lambda a, g: (a, 0, 0)),
            pl.BlockSpec((1, 1, CMP_HIDDEN), lambda a, g: (a, 0, 0)),
            pl.BlockSpec((1, CMP_HIDDEN, HEAD_DIM), lambda a, g: (a, 0, 0)),
        ],
        out_specs=pl.BlockSpec((1, 1, n_seg, HEAD_DIM), lambda a, g: (a, g, 0, 0)),
        out_shape=jax.ShapeDtypeStruct((2, G, n_seg, HEAD_DIM), F32),
        compiler_params=_cparams(("arbitrary", "arbitrary")),
        name="compress",
    )(segs, pe, w1, b1, w2)


def _hgrn_consts(ch):
    nl = int(np.log2(ch))
    assert 1 << nl == ch
    t = np.arange(ch)[:, None]
    r = np.arange(ch)[None, :]
    mats = [r <= t]
    masks = [np.eye(ch, dtype=bool)]
    for l in range(nl):
        half = 1 << l
        blk = t // (2 * half)
        ref = blk * 2 * half + half - 1
        up = ((t >> l) & 1) == 1
        mats.append(np.where(up, (r > ref) & (r <= t), (r > t) & (r <= ref)))
        masks.append(up & (((r >> l) & 1) == 0) & (blk == r // (2 * half)))
    mc = np.concatenate(mats[:1 + HG_MM_LEVELS], axis=0).astype(np.float32)
    lm = np.stack(masks, axis=0).astype(np.float32)
    return nl, mc, lm


def _hgrn_kernel(q_ref, f_ref, i_ref, z_ref, lbl_ref, nw_ref, mc_ref, lm_ref, o_ref, st_ref, e_ref, *, ch, nl):
    @pl.when(pl.program_id(0) == 0)
    def _():
        st_ref[...] = jnp.zeros_like(st_ref)

    lg = lbl_ref[...]
    ex = jnp.exp(lg - jnp.max(lg, axis=0, keepdims=True))
    lb = ex[0:1] / jnp.sum(ex, axis=0, keepdims=True)
    f = lb + (1.0 - lb) * jax.nn.sigmoid(f_ref[...])
    g_hi, g_lo = _split_bf16(jnp.log(f))
    mc = mc_ref[...]
    e_ref[...] = _dot(mc, g_hi) + _dot(mc, g_lo)
    row = lax.broadcasted_iota(jnp.int32, (ch, 1), 0)
    nw = nw_ref[...]

    heads = [slice(h * HG_DK, (h + 1) * HG_DK) for h in range(HG_HEADS)]
    qs = [q_ref[:, sl] for sl in heads]
    ks = [1.0 - f[:, sl] for sl in heads]
    scs = [_dot_nt(q.astype(BF16), k.astype(BF16)) * lm_ref[0] for q, k in zip(qs, ks)]
    bs = [e_ref[0:ch, sl] for sl in heads]
    for l in range(nl):
        up = ((row >> l) & 1) == 1
        half = 1 << l
        for h, sl in enumerate(heads):
            if l < HG_MM_LEVELS:
                el = e_ref[(l + 1) * ch:(l + 2) * ch, sl]
                xl = jnp.where(up, qs[h], ks[h]) * jnp.exp(el)
            else:
                parts = []
                for r0 in range(0, ch, 2 * half):
                    mid = r0 + half
                    edge = bs[h][mid - 1:mid, :]
                    parts.append(ks[h][r0:mid] * jnp.exp(edge - bs[h][r0:mid]))
                    parts.append(qs[h][mid:mid + half] * jnp.exp(bs[h][mid:mid + half] - edge))
                xl = jnp.concatenate(parts, axis=0)
            xl = xl.astype(BF16)
            scs[h] = scs[h] + _dot_nt(xl, xl) * lm_ref[l + 1]

    vbs = [i_ref[:, sl].astype(BF16) for sl in heads]
    sts = [st_ref[h] for h in range(HG_HEADS)]
    inters = [_dot_nt((qs[h] * jnp.exp(bs[h])).astype(BF16), sts[h].astype(BF16)) for h in range(HG_HEADS)]
    for h, sl in enumerate(heads):
        ke = (ks[h] * jnp.exp(bs[h][ch - 1:ch, :] - bs[h])).astype(BF16)
        upd = lax.dot_general(vbs[h], ke, (((0,), (0,)), ((), ())), preferred_element_type=F32)
        st_ref[h] = sts[h] * jnp.exp(bs[h][ch - 1:ch, :]) + upd
    outs = [inters[h] + _dot(scs[h].astype(BF16), vbs[h]) for h in range(HG_HEADS)]
    for h, sl in enumerate(heads):
        o = outs[h]
        ms = jnp.mean(o * o, axis=-1, keepdims=True)
        z = z_ref[:, sl]
        o_ref[:, sl] = (o * lax.rsqrt(ms + EPS) * nw * (z * jax.nn.sigmoid(z))).astype(o_ref.dtype)


def _hgrn(proj, lb_logits, hg_norm_w, ch=128):
    S = proj.shape[0]
    nl, mc, lm = _hgrn_consts(ch)
    kern = functools.partial(_hgrn_kernel, ch=ch, nl=nl)
    nlb = lb_logits.shape[0]
    return pl.pallas_call(
        kern,
        grid=(S // ch,),
        in_specs=[
            pl.BlockSpec((ch, HG_FDIM), lambda c: (c, 0)),
            pl.BlockSpec((ch, HG_FDIM), lambda c: (c, 1)),
            pl.BlockSpec((ch, HG_WIDTH), lambda c: (c, 2)),
            pl.BlockSpec((ch, HG_WIDTH), lambda c: (c, 3)),
            pl.BlockSpec((nlb, HG_FDIM), lambda c: (0, 0)),
            pl.BlockSpec((1, HG_DV), lambda c: (0, 0)),
            pl.BlockSpec(mc.shape, lambda c: (0, 0)),
            pl.BlockSpec(lm.shape, lambda c: (0, 0, 0)),
        ],
        out_specs=pl.BlockSpec((ch, HG_WIDTH), lambda c: (c, 0)),
        out_shape=jax.ShapeDtypeStruct((S, HG_WIDTH), BF16),
        scratch_shapes=[
            pltpu.VMEM((HG_HEADS, HG_DV, HG_DK), F32),
            pltpu.VMEM((mc.shape[0], HG_FDIM), F32),
        ],
        compiler_params=_cparams(("arbitrary",)),
        name="hgrn",
    )(proj, proj, proj, proj, lb_logits, hg_norm_w.reshape(1, HG_DV), jnp.asarray(mc, BF16), jnp.asarray(lm, F32))


def _stack_heads(q_ref, rows, cosf, sinf):
    qscale = (HEAD_DIM ** -0.5) * LOG2E
    return jnp.concatenate(
        [(_rope(q_ref[rows, h * HEAD_DIM:(h + 1) * HEAD_DIM], cosf, sinf) * qscale).astype(BF16)
         for h in range(NSA_HG)], axis=0)


def _per_head(a):
    return jnp.concatenate([a] * NSA_HG, axis=0)


def _nsa_select_kernel(q_ref, cos_ref, sin_ref, gate_ref, kc_ref, vc_ref, ov_ref, ca_ref, selb_ref, pc_ref, *, nsub):
    n = pl.program_id(1)
    R = NSA_HG * Q_BLOCK
    ncp = kc_ref.shape[2]
    nb = Q_BLOCK // CMP_STRIDE
    subs = range(nsub)
    rows = [slice(i * Q_BLOCK, (i + 1) * Q_BLOCK) for i in subs]
    q0s = [(n * nsub + i) * Q_BLOCK for i in subs]

    crow = lax.broadcasted_iota(jnp.int32, (ncp, LANES), 0)
    clane = lax.broadcasted_iota(jnp.int32, (ncp, LANES), 1)
    kc = kc_ref[0, 0].astype(BF16)
    vca = jnp.concatenate([vc_ref[0, 0].astype(BF16), jnp.ones((ncp, LANES), BF16)], axis=1)
    ca4 = _per_head(ca_ref[...])
    ov = ov_ref[...]

    ss = []
    for i in subs:
        q4 = _stack_heads(q_ref, rows[i], cos_ref[rows[i], :], sin_ref[rows[i], :])
        c_first = (n * nsub + i) * nb - (CMP_LEN // CMP_STRIDE - 1)
        flags = jnp.where(clane == 0, jnp.where(crow >= c_first + nb, NEG_BIG, 0.0),
                          jnp.where(crow == c_first + clane - 1, 1.0, 0.0)).astype(BF16)
        ss.append(_halves(_dot_nt, jnp.concatenate([q4, ca4], axis=1), jnp.concatenate([kc, flags], axis=1)))
    es = [jnp.exp2(s - jnp.max(s, axis=-1, keepdims=True)) for s in ss]
    pvs = [_halves(_dot, e.astype(BF16), vca) for e in es]

    imps = []
    for i in subs:
        t_row = q0s[i] + (lax.broadcasted_iota(jnp.int32, (R, LANES), 0) & (Q_BLOCK - 1))
        inv = jnp.where(t_row >= CMP_LEN - 1, 1.0, 0.0) / pvs[i][:, LANES:]
        o_cmp = pvs[i][:, :LANES] * inv
        gate = jax.nn.sigmoid(gate_ref[rows[i], :])
        for h in range(NSA_HG):
            pc_ref[rows[i], h * HEAD_DIM:(h + 1) * HEAD_DIM] = (
                gate[:, 3 * h:3 * h + 1] * o_cmp[h * Q_BLOCK:(h + 1) * Q_BLOCK])
        p = es[i] * jnp.concatenate([inv] * (ncp // LANES), axis=1)
        p_sum = p[0:Q_BLOCK]
        for h in range(1, NSA_HG):
            p_sum = p_sum + p[h * Q_BLOCK:(h + 1) * Q_BLOCK]
        ps_hi, ps_lo = _split_bf16(p_sum)
        imps.append(_dot(ps_hi, ov) + _dot(ps_lo, ov))

    nsel_pad = ov.shape[1]
    jj = lax.broadcasted_iota(jnp.int32, (nsel_pad, Q_BLOCK), 0)
    jjf = jj.astype(F32)
    lane_q = lax.broadcasted_iota(jnp.int32, (nsel_pad, Q_BLOCK), 1)
    TAKEN = -2.0
    scores = []
    for i in subs:
        tq = q0s[i] + lane_q
        jt = tq >> 6
        forced = (jj == 0) | (jj == jt) | (jj == jt - 1)
        scores.append(jnp.where(jj * SEL_LEN <= tq, jnp.where(forced, TAKEN, imps[i].T), -1.0))
    for _ in range(SEL_TOPK - 3):
        for i in subs:
            mx = jnp.max(scores[i], axis=0, keepdims=True)
            first = jnp.min(jnp.where(scores[i] == mx, jjf, float(nsel_pad)), axis=0, keepdims=True)
            first = jnp.where(mx >= 0.0, first, -1.0)
            scores[i] = jnp.where(jjf == first, TAKEN, scores[i])
    for i in subs:
        selb_ref[rows[i], :] = jnp.where(scores[i] == TAKEN, 0.0, NEG_BIG).T.astype(BF16)


def _nsa_select(proj, cosf, sinf, kcvc, nsub=4):
    S = proj.shape[0]
    ncp = kcvc.shape[2]
    nsel = S // SEL_LEN
    qs = min(nsub * Q_BLOCK, S)
    nsub = qs // Q_BLOCK
    assert nsel <= LANES and ncp % LANES == 0 and S % qs == 0
    assert CMP_LEN == 2 * CMP_STRIDE and Q_BLOCK % CMP_STRIDE == 0 and Q_BLOCK // CMP_STRIDE < LANES
    ov = jnp.asarray(_overlap_matrix(ncp, LANES, nsel), BF16)
    ca = jnp.asarray(_cmp_staircase(), BF16)
    qw = NSA_HG * HEAD_DIM
    return pl.pallas_call(
        functools.partial(_nsa_select_kernel, nsub=nsub),
        grid=(NSA_KV, S // qs),
        in_specs=[
            pl.BlockSpec((qs, qw), lambda g, n: (n, OFF_Q // qw + g)),
            pl.BlockSpec((qs, LANES), lambda g, n: (n, 0)),
            pl.BlockSpec((qs, LANES), lambda g, n: (n, 0)),
            pl.BlockSpec((qs, LANES), lambda g, n: (n, OFF_G // LANES + g)),
            pl.BlockSpec((1, 1, ncp, HEAD_DIM), lambda g, n: (0, g, 0, 0)),
            pl.BlockSpec((1, 1, ncp, HEAD_DIM), lambda g, n: (1, g, 0, 0)),
            pl.BlockSpec((ncp, LANES), lambda g, n: (0, 0)),
            pl.BlockSpec((Q_BLOCK, LANES), lambda g, n: (0, 0)),
        ],
        out_specs=[
            pl.BlockSpec((qs, LANES), lambda g, n: (n, g)),
            pl.BlockSpec((qs, qw), lambda g, n: (n, g)),
        ],
        out_shape=[
            jax.ShapeDtypeStruct((S, NSA_KV * LANES), BF16),
            jax.ShapeDtypeStruct((S, NSA_WIDTH), F32),
        ],
        compiler_params=_cparams(("arbitrary", "arbitrary")),
        name="nsa_select",
    )(proj, cosf, sinf, proj, kcvc, kcvc, ov, ca)


def _nsa_kernel(q_ref, cos_ref, sin_ref, gate_ref, z_ref, selb_ref, pc_ref, wb_ref, cb_ref,
                ks_ref, vs_ref, kw_ref, vw_ref, o_ref, m_ref, accl_ref, sa_ref, sb_ref, part_ref, *, tk):
    n = pl.program_id(1)
    q0 = n * Q_BLOCK
    R = NSA_HG * Q_BLOCK
    q4 = _stack_heads(q_ref, slice(None), cos_ref[...], sin_ref[...])
    per_head = _per_head

    def tpos(shape):
        return q0 + (lax.broadcasted_iota(jnp.int32, shape, 0) & (Q_BLOCK - 1))

    wspan = WINDOW + Q_BLOCK
    w0 = pl.multiple_of(jnp.maximum(q0 - WINDOW, 0), Q_BLOCK)
    sw = (_halves(_dot_nt, q4, kw_ref[pl.ds(w0, wspan), :])
          + per_head(wb_ref[jnp.minimum(n, WINDOW // Q_BLOCK)]))
    ew = jnp.exp2(sw - jnp.max(sw, axis=-1, keepdims=True))
    vwa = jnp.concatenate([vw_ref[pl.ds(w0, wspan), :], jnp.ones((wspan, LANES), BF16)], axis=1)
    pvw = _halves(_dot, ew.astype(BF16), vwa)
    o_win = pvw[:, :LANES] / pvw[:, LANES:]

    gate = jax.nn.sigmoid(gate_ref[...])
    for h in range(NSA_HG):
        cs = slice(h * HEAD_DIM, (h + 1) * HEAD_DIM)
        part_ref[:, cs] = pc_ref[:, cs] + gate[:, 3 * h + 2:3 * h + 3] * o_win[h * Q_BLOCK:(h + 1) * Q_BLOCK]

    sel_bias = per_head(selb_ref[...])

    m_ref[...] = jnp.full(m_ref.shape, NEG_BIG, F32)
    accl_ref[...] = jnp.zeros(accl_ref.shape, F32)
    qa = jnp.concatenate([q4, sel_bias], axis=1)
    key_blk = lax.broadcasted_iota(jnp.int32, (tk, LANES), 0) >> 6
    blk_lane = lax.broadcasted_iota(jnp.int32, (tk, LANES), 1)
    ones_k = jnp.ones((tk, LANES), BF16)
    rep = tk // LANES

    n_q = tk // Q_BLOCK
    last = lax.shift_right_logical(n, n_q.bit_length() - 1)
    diag = n - last * n_q

    def score_tile(kt, dst_ref):
        k0 = pl.multiple_of(kt * tk, tk)
        onehot = jnp.where(key_blk + kt * (tk // SEL_LEN) == blk_lane, 1.0, 0.0).astype(BF16)
        dst_ref[...] = _dot_nt(qa, jnp.concatenate([ks_ref[pl.ds(k0, tk), :], onehot], axis=1))

    def consume_tile(kt, src_ref, causal=False):
        k0 = pl.multiple_of(kt * tk, tk)
        sc = src_ref[...]
        if causal:
            sc = sc + per_head(cb_ref[diag])
        m_prev = m_ref[...]
        m_next = jnp.maximum(m_prev, jnp.max(sc, axis=-1, keepdims=True))
        pr = jnp.exp2(sc - jnp.concatenate([m_next] * rep, axis=1))
        alpha = jnp.exp2(m_prev - m_next)
        va = jnp.concatenate([vs_ref[pl.ds(k0, tk), :], ones_k], axis=1)
        accl_ref[...] = jnp.concatenate([alpha, alpha], axis=1) * accl_ref[...] + _dot(pr.astype(BF16), va)
        m_ref[...] = m_next

    score_tile(0, sa_ref)

    def slc_pair(i, carry):
        score_tile(2 * i + 1, sb_ref)
        consume_tile(2 * i, sa_ref)
        score_tile(2 * i + 2, sa_ref)
        consume_tile(2 * i + 1, sb_ref)
        return carry

    lax.fori_loop(0, lax.shift_right_logical(last, 1), slc_pair, 0)

    @pl.when((last & 1) == 0)
    def _():
        consume_tile(last, sa_ref, causal=True)

    @pl.when((last & 1) == 1)
    def _():
        score_tile(last, sb_ref)
        consume_tile(last - 1, sa_ref)
        consume_tile(last, sb_ref, causal=True)

    o_slc = accl_ref[:, :LANES] / accl_ref[:, LANES:]

    gsig = jax.nn.sigmoid(gate_ref[...])
    for h in range(NSA_HG):
        rs = slice(h * Q_BLOCK, (h + 1) * Q_BLOCK)
        cs = slice(h * HEAD_DIM, (h + 1) * HEAD_DIM)
        oh = part_ref[:, cs] + gsig[:, 3 * h + 1:3 * h + 2] * o_slc[rs]
        z = z_ref[:, cs]
        o_ref[:, cs] = (oh * (z * jax.nn.sigmoid(z))).astype(o_ref.dtype)


def _cmp_staircase():
    nb = Q_BLOCK // CMP_STRIDE
    a = np.zeros((Q_BLOCK, LANES), np.float32)
    a[:, 0] = 1.0
    r = np.arange(Q_BLOCK)[:, None]
    i = np.arange(nb)[None, :]
    a[:, 1:1 + nb] = np.where(r < CMP_STRIDE * i + (CMP_STRIDE - 1), NEG_BIG, 0.0)
    return a


def _window_masks():
    n_w = WINDOW // Q_BLOCK
    r = np.arange(Q_BLOCK)[None, :, None]
    c = np.arange(WINDOW + Q_BLOCK)[None, None, :]
    d = np.arange(n_w + 1)[:, None, None]
    ok = np.where(d < n_w, c <= Q_BLOCK * d + r, (c > r) & (c <= WINDOW + r))
    return np.where(ok, 0.0, NEG_BIG).astype(np.float32)


def _causal_staircases(tk):
    n_q = tk // Q_BLOCK
    r = np.arange(Q_BLOCK)[None, :, None]
    c = np.arange(tk)[None, None, :]
    d = np.arange(n_q + 1)[:, None, None]
    return np.where((c <= Q_BLOCK * d + r) | (d == n_q), 0.0, NEG_BIG).astype(np.float32)


def _overlap_matrix(ncp, nsel_pad, nsel):
    ci = np.arange(ncp)[:, None] * CMP_STRIDE
    sj = np.arange(nsel_pad)[None, :] * SEL_LEN
    ov = (ci < sj + SEL_LEN) & (ci + CMP_LEN > sj) & (np.arange(nsel_pad)[None, :] < nsel) & (np.arange(ncp)[:, None] < ncp - 1)
    return ov.astype(np.float32)


def _nsa(proj, cosf, sinf, selb, pcmp, kvb, tk=512):
    S = proj.shape[0]
    assert S % tk == 0 and S >= WINDOW + Q_BLOCK and WINDOW % Q_BLOCK == 0
    wb = jnp.asarray(_window_masks(), F32)
    n_q = tk // Q_BLOCK
    assert n_q & (n_q - 1) == 0 and tk % SEL_LEN == 0
    cb = jnp.asarray(_causal_staircases(tk), F32)
    qw = NSA_HG * HEAD_DIM
    R = NSA_HG * Q_BLOCK
    kern = functools.partial(_nsa_kernel, tk=tk)
    return pl.pallas_call(
        kern,
        grid=(NSA_KV, S // Q_BLOCK),
        in_specs=[
            pl.BlockSpec((Q_BLOCK, qw), lambda g, n: (n, OFF_Q // qw + g)),
            pl.BlockSpec((Q_BLOCK, LANES), lambda g, n: (n, 0)),
            pl.BlockSpec((Q_BLOCK, LANES), lambda g, n: (n, 0)),
            pl.BlockSpec((Q_BLOCK, LANES), lambda g, n: (n, OFF_G // LANES + g)),
            pl.BlockSpec((Q_BLOCK, qw), lambda g, n: (n, OFF_Z // qw + g)),
            pl.BlockSpec((Q_BLOCK, LANES), lambda g, n: (n, g)),
            pl.BlockSpec((Q_BLOCK, qw), lambda g, n: (n, g)),
            pl.BlockSpec((WINDOW // Q_BLOCK + 1, Q_BLOCK, WINDOW + Q_BLOCK), lambda g, n: (0, 0, 0)),
            pl.BlockSpec((n_q + 1, Q_BLOCK, tk), lambda g, n: (0, 0, 0)),
            pl.BlockSpec((S, HEAD_DIM), lambda g, n: (0, 0 * NSA_KV + g)),
            pl.BlockSpec((S, HEAD_DIM), lambda g, n: (0, 1 * NSA_KV + g)),
            pl.BlockSpec((S, HEAD_DIM), lambda g, n: (0, 2 * NSA_KV + g)),
            pl.BlockSpec((S, HEAD_DIM), lambda g, n: (0, 3 * NSA_KV + g)),
        ],
        out_specs=pl.BlockSpec((Q_BLOCK, qw), lambda g, n: (n, g)),
        out_shape=jax.ShapeDtypeStruct((S, NSA_WIDTH), BF16),
        scratch_shapes=[
            pltpu.VMEM((R, LANES), F32),
            pltpu.VMEM((R, HEAD_DIM + LANES), F32),
            pltpu.VMEM((R, tk), F32),
            pltpu.VMEM((R, tk), F32),
            pltpu.VMEM((Q_BLOCK, qw), F32),
        ],
        compiler_params=_cparams(("arbitrary", "arbitrary")),
        name="nsa",
    )(proj, cosf, sinf, proj, proj, selb, pcmp, wb, cb, kvb, kvb, kvb, kvb)


def _merge_kernel(ohg_ref, onsa_ref, g0_ref, g1_ref, g2_ref, g3_ref, whg_ref, wnsa_ref, o_ref):
    y_hg = _dot(ohg_ref[...], whg_ref[...])
    y_nsa = _dot(onsa_ref[...], wnsa_ref[...])
    half = D_MODEL // 2
    gh = (g0_ref, g1_ref)
    gn = (g2_ref, g3_ref)
    for c in range(2):
        cs = slice(c * half, (c + 1) * half)
        m = jax.nn.sigmoid(gh[c][...]) * y_hg[:, cs] + jax.nn.sigmoid(gn[c][...]) * y_nsa[:, cs]
        o_ref[:, cs] = m.astype(o_ref.dtype)


def _merge(o_hg, o_nsa, proj, w_hg, w_nsa, tm=256):
    S = o_hg.shape[0]
    tm = min(tm, S)
    half = D_MODEL // 2
    gb = OFF_MG // half
    return pl.pallas_call(
        _merge_kernel,
        grid=(S // tm,),
        in_specs=[
            pl.BlockSpec((tm, HG_WIDTH), lambda i: (i, 0)),
            pl.BlockSpec((tm, NSA_WIDTH), lambda i: (i, 0)),
            pl.BlockSpec((tm, half), lambda i: (i, gb + 0)),
            pl.BlockSpec((tm, half), lambda i: (i, gb + 1)),
            pl.BlockSpec((tm, half), lambda i: (i, gb + 2)),
            pl.BlockSpec((tm, half), lambda i: (i, gb + 3)),
            pl.BlockSpec((HG_WIDTH, D_MODEL), lambda i: (0, 0)),
            pl.BlockSpec((NSA_WIDTH, D_MODEL), lambda i: (0, 0)),
        ],
        out_specs=pl.BlockSpec((tm, D_MODEL), lambda i: (i, 0)),
        out_shape=jax.ShapeDtypeStruct((S, D_MODEL), BF16),
        compiler_params=_cparams(("arbitrary",)),
        name="merge",
    )(o_hg, o_nsa, proj, proj, proj, proj, w_hg, w_nsa)


def _out_kernel(x_ref, m_ref, w_ref, nw_ref, o_ref):
    h = x_ref[...] + _dot(m_ref[...], w_ref[...])
    ms = jnp.mean(h * h, axis=-1, keepdims=True)
    o_ref[...] = h * lax.rsqrt(ms + EPS) * nw_ref[...]


def _out(x2, merged, w_out, final_w, tm=256):
    S, D = x2.shape
    tm = min(tm, S)
    return pl.pallas_call(
        _out_kernel,
        grid=(S // tm,),
        in_specs=[
            pl.BlockSpec((tm, D), lambda i: (i, 0)),
            pl.BlockSpec((tm, D), lambda i: (i, 0)),
            pl.BlockSpec((D, D), lambda i: (0, 0)),
            pl.BlockSpec((1, D), lambda i: (0, 0)),
        ],
        out_specs=pl.BlockSpec((tm, D), lambda i: (i, 0)),
        out_shape=jax.ShapeDtypeStruct((S, D), F32),
        compiler_params=_cparams(("arbitrary",)),
        name="out_proj",
    )(x2, merged, w_out, final_w.reshape(1, D))


def _layer(x2, norm_w, w_in, lb_logits, hg_norm_w, cmp_k_pos, cmp_k_w1, cmp_k_b1, cmp_k_w2,
           cmp_v_pos, cmp_v_w1, cmp_v_b1, cmp_v_w2, w_branch_hg, w_branch_nsa, w_out, final_w):
    S = x2.shape[0]
    proj = _norm_proj(x2, norm_w, _w_prep(w_in.T))
    cosf, sinf = _rope_tables(S)
    segs, kvb = _kv_prep(proj, cosf, sinf)
    pe = jnp.stack([cmp_k_pos.reshape(1, -1), cmp_v_pos.reshape(1, -1)])
    w1 = jnp.stack([cmp_k_w1, cmp_v_w1]).astype(BF16)
    b1 = jnp.stack([cmp_k_b1.reshape(1, -1), cmp_v_b1.reshape(1, -1)])
    w2 = jnp.stack([cmp_k_w2, cmp_v_w2]).astype(BF16)
    kcvc = _compress(segs, pe, w1, b1, w2)
    o_hg = _hgrn(proj, lb_logits, hg_norm_w)
    selb, pcmp = _nsa_select(proj, cosf, sinf, kcvc)
    o_nsa = _nsa(proj, cosf, sinf, selb, pcmp, kvb)
    merged = _merge(o_hg, o_nsa, proj, w_branch_hg.astype(BF16), w_branch_nsa.astype(BF16))
    return _out(x2, merged, w_out.astype(BF16), final_w)


def kernel(x, norm_w, w_in, hg_lb_logits, hg_norm_w, cmp_k_pos, cmp_k_w1, cmp_k_b1, cmp_k_w2, cmp_v_pos, cmp_v_w1, cmp_v_b1, cmp_v_w2, w_branch_hg, w_branch_nsa, w_out, final_norm_w):
    B, S, D = x.shape
    assert B == 1 and D == D_MODEL and norm_w.shape[0] == 1
    y = _layer(x[0], norm_w[0], w_in[0], hg_lb_logits, hg_norm_w[0], cmp_k_pos[0], cmp_k_w1[0], cmp_k_b1[0],
               cmp_k_w2[0], cmp_v_pos[0], cmp_v_w1[0], cmp_v_b1[0], cmp_v_w2[0], w_branch_hg[0],
               w_branch_nsa[0], w_out[0], final_norm_w)
    return y[None]
```

```python
import functools

import numpy as np
import jax
import jax.numpy as jnp
from jax import lax
from jax.experimental import pallas as pl
from jax.experimental.pallas import tpu as pltpu

F32 = jnp.float32
BF16 = jnp.bfloat16

D_MODEL = 2048
HG_HEADS = 8
HG_DK = 128
HG_DV = 128
HG_FDIM = HG_HEADS * HG_DK
HG_WIDTH = HG_HEADS * HG_DV
HG_MM_LEVELS = 3
NSA_HEADS = 16
NSA_KV = 4
NSA_HG = NSA_HEADS // NSA_KV
HEAD_DIM = 128
NSA_WIDTH = NSA_HEADS * HEAD_DIM
NSA_KVW = NSA_KV * HEAD_DIM
CMP_LEN = 32
CMP_STRIDE = 16
CMP_HIDDEN = 512
SEL_LEN = 64
SEL_TOPK = 16
WINDOW = 512
Q_BLOCK = 256
ROPE_THETA = 500000.0
ROT_DIM = HEAD_DIM // 4
ROT_HALF = ROT_DIM // 2
EPS = 1e-6
LOG2E = float(np.log2(np.e))

LANES = 128
NEG_BIG = -1e30
VMEM_LIMIT = 56 * 1024 * 1024

OFF_HG = 0
OFF_Q = 4 * HG_FDIM
OFF_KV = OFF_Q + NSA_WIDTH
OFF_MG = OFF_KV + 6 * NSA_KVW
OFF_Z = OFF_MG + 2 * D_MODEL
OFF_G = OFF_Z + NSA_WIDTH
PROJ_COLS = OFF_G + NSA_KV * LANES
IN_GATE_OFF = OFF_MG
IN_Z_OFF = IN_GATE_OFF + 3 * NSA_HEADS
IN_MG_OFF = IN_Z_OFF + NSA_WIDTH


def _dot(a, b):
    return jnp.dot(a, b, preferred_element_type=F32)


def _dot_nt(a, b):
    return lax.dot_general(a, b, (((1,), (1,)), ((), ())), preferred_element_type=F32)


def _halves(dot, a, b):
    h = a.shape[0] // 2
    return jnp.concatenate([dot(a[:h], b), dot(a[h:], b)], axis=0)


def _split_bf16(a):
    hi = a.astype(BF16)
    lo = (a - hi.astype(F32)).astype(BF16)
    return hi, lo


def _cparams(sem):
    return pltpu.CompilerParams(dimension_semantics=sem, vmem_limit_bytes=VMEM_LIMIT)


def _w_prep_kernel(wt_ref, wg_ref, o_ref, *, n_direct):
    j = pl.program_id(0)

    @pl.when(j < n_direct)
    def _():
        o_ref[...] = wt_ref[...].astype(BF16)

    @pl.when(j >= n_direct)
    def _():
        o_ref[...] = wg_ref[...].astype(BF16)


def _w_prep(wt, tn=512):
    D = wt.shape[1]
    n_main, n_mg, n_z = OFF_MG // tn, 2 * D_MODEL // tn, NSA_WIDTH // tn
    n_direct = n_main + n_mg + n_z
    assert OFF_MG % tn == 0 and D_MODEL % tn == 0 and NSA_KV * LANES == tn
    gates = wt[IN_GATE_OFF:IN_Z_OFF].reshape(NSA_KV, 3 * NSA_HG, D)
    gates = jnp.pad(gates, ((0, 0), (0, LANES - 3 * NSA_HG), (0, 0))).reshape(NSA_KV * LANES, D)
    n_extra = 2
    gates = jnp.pad(gates, ((0, (n_extra - 1) * tn), (0, 0)))

    sub = 8
    assert IN_MG_OFF % sub == 0 and IN_Z_OFF % sub == 0 and tn % sub == 0

    def src_row(j):
        in_mg = IN_MG_OFF // sub + (j - n_main) * (tn // sub)
        in_z = IN_Z_OFF // sub + (jnp.minimum(j, n_direct - 1) - n_main - n_mg) * (tn // sub)
        return jnp.where(j < n_main, j * (tn // sub), jnp.where(j < n_main + n_mg, in_mg, in_z)) * sub

    return pl.pallas_call(
        functools.partial(_w_prep_kernel, n_direct=n_direct),
        grid=(n_direct + n_extra,),
        in_specs=[
            pl.BlockSpec((pl.Element(tn), pl.Element(D)), lambda j: (src_row(j), 0)),
            pl.BlockSpec((tn, D), lambda j: (jnp.maximum(j - n_direct, 0), 0)),
        ],
        out_specs=pl.BlockSpec((tn, D), lambda j: (j, 0)),
        out_shape=jax.ShapeDtypeStruct(((n_direct + n_extra) * tn, D), BF16),
        compiler_params=_cparams(("arbitrary",)),
        name="w_prep",
    )(wt, gates)


def _norm_proj_kernel(x_ref, nw_ref, w_ref, o_ref, xn_ref):
    @pl.when(pl.program_id(1) == 0)
    def _():
        x = x_ref[...]
        ms = jnp.mean(x * x, axis=-1, keepdims=True)
        xn_ref[...] = (x * lax.rsqrt(ms + EPS) * nw_ref[...]).astype(BF16)

    o_ref[...] = _dot_nt(xn_ref[...], w_ref[...])


def _norm_proj(x2, norm_w, wtb, tm=1024, tn=1024):
    S, D = x2.shape
    N = wtb.shape[0]
    tm = min(tm, S)
    return pl.pallas_call(
        _norm_proj_kernel,
        grid=(S // tm, N // tn),
        in_specs=[
            pl.BlockSpec((tm, D), lambda i, j: (i, 0)),
            pl.BlockSpec((1, D), lambda i, j: (0, 0)),
            pl.BlockSpec((tn, D), lambda i, j: (j, 0)),
        ],
        out_specs=pl.BlockSpec((tm, tn), lambda i, j: (i, j)),
        out_shape=jax.ShapeDtypeStruct((S, N), F32),
        scratch_shapes=[pltpu.VMEM((tm, D), BF16)],
        compiler_params=_cparams(("arbitrary", "arbitrary")),
        name="norm_proj",
    )(x2, norm_w.reshape(1, D), wtb)


def _rope(x, cosf, sinf):
    lane = lax.broadcasted_iota(jnp.int32, x.shape, 1)
    rot = jnp.where(lane < ROT_HALF, pltpu.roll(x, LANES - ROT_HALF, 1), pltpu.roll(x, ROT_HALF, 1))
    return x * cosf + rot * sinf


def _rope_tables(S):
    pos = np.arange(S, dtype=np.float64)
    inv = ROPE_THETA ** (-np.arange(0, ROT_DIM, 2, dtype=np.float64) / ROT_DIM)
    ang = pos[:, None] * inv[None, :]
    cos, sin = np.cos(ang), np.sin(ang)
    rest = LANES - ROT_DIM
    cosf = np.concatenate([cos, cos, np.ones((S, rest))], axis=1)
    sinf = np.concatenate([-sin, sin, np.zeros((S, rest))], axis=1)
    return jnp.asarray(cosf, F32), jnp.asarray(sinf, F32)


def _kv_prep_kernel(kv_ref, cos_ref, sin_ref, cmp_ref, kvb_ref, row_ref):
    cosf = cos_ref[...]
    sinf = sin_ref[...]
    W = NSA_KVW
    n_seg = row_ref.shape[0] // CMP_STRIDE

    def to_segments(a, g, rows):
        row_ref[...] = rows
        for l in range(CMP_STRIDE):
            cmp_ref[a, g, :, l * HEAD_DIM:(l + 1) * HEAD_DIM] = row_ref[pl.ds(l, n_seg, stride=CMP_STRIDE), :]

    for g in range(NSA_KV):
        kc = kv_ref[:, 0 * W + g * LANES:0 * W + (g + 1) * LANES]
        to_segments(0, g, _rope(kc, cosf, sinf))
        to_segments(1, g, kv_ref[:, 1 * W + g * LANES:1 * W + (g + 1) * LANES])
        ks = kv_ref[:, 2 * W + g * LANES:2 * W + (g + 1) * LANES]
        kvb_ref[:, 0 * W + g * LANES:0 * W + (g + 1) * LANES] = _rope(ks, cosf, sinf).astype(BF16)
        kvb_ref[:, 1 * W + g * LANES:1 * W + (g + 1) * LANES] = kv_ref[:, 3 * W + g * LANES:3 * W + (g + 1) * LANES].astype(BF16)
        kw = kv_ref[:, 4 * W + g * LANES:4 * W + (g + 1) * LANES]
        kvb_ref[:, 2 * W + g * LANES:2 * W + (g + 1) * LANES] = _rope(kw, cosf, sinf).astype(BF16)
        kvb_ref[:, 3 * W + g * LANES:3 * W + (g + 1) * LANES] = kv_ref[:, 5 * W + g * LANES:5 * W + (g + 1) * LANES].astype(BF16)


def _kv_prep(proj, cosf, sinf, tm=512):
    S = proj.shape[0]
    tm = min(tm, S)
    kvw = 6 * NSA_KVW
    return pl.pallas_call(
        _kv_prep_kernel,
        grid=(S // tm,),
        in_specs=[
            pl.BlockSpec((tm, kvw), lambda i: (i, OFF_KV // kvw)),
            pl.BlockSpec((tm, LANES), lambda i: (i, 0)),
            pl.BlockSpec((tm, LANES), lambda i: (i, 0)),
        ],
        out_specs=[
            pl.BlockSpec((2, NSA_KV, tm // CMP_STRIDE, CMP_STRIDE * HEAD_DIM), lambda i: (0, 0, i, 0)),
            pl.BlockSpec((tm, 4 * NSA_KVW), lambda i: (i, 0)),
        ],
        out_shape=[
            jax.ShapeDtypeStruct((2, NSA_KV, S // CMP_STRIDE, CMP_STRIDE * HEAD_DIM), F32),
            jax.ShapeDtypeStruct((S, 4 * NSA_KVW), BF16),
        ],
        scratch_shapes=[pltpu.VMEM((tm, HEAD_DIM), F32)],
        compiler_params=_cparams(("arbitrary",)),
        name="kv_prep",
    )(proj, cosf, sinf)


def _compress_kernel(seg_ref, pe_ref, w1_ref, b1_ref, w2_ref, o_ref):
    half = (CMP_LEN // 2) * HEAD_DIM
    seg = seg_ref[0, 0]
    n_seg = seg.shape[0]
    pe = pe_ref[0]
    a = (seg + pe[:, :half]).astype(BF16)
    b = (seg + pe[:, half:]).astype(BF16)
    u = _dot(a, w1_ref[0, :half, :])
    v = _dot(b, w1_ref[0, half:, :])
    v_next = pltpu.roll(v, n_seg - 1, 0)
    pre = u + v_next + b1_ref[0]
    h = 0.5 * pre * (1.0 + jnp.tanh(np.sqrt(2.0 / np.pi).astype(np.float32) * (pre + 0.044715 * (pre * pre * pre))))
    o_ref[0, 0] = _dot(h.astype(BF16), w2_ref[0])


def _compress(segs, pe, w1, b1, w2):
    _, G, n_seg, segw = segs.shape
    return pl.pallas_call(
        _compress_kernel,
        grid=(2, G),
        in_specs=[
            pl.BlockSpec((1, 1, n_seg, segw), lambda a, g: (a, g, 0, 0)),
            pl.BlockSpec((1, 1, 2 * segw), lambda a, g: (a, 0, 0)),
            pl.BlockSpec((1, 2 * segw, CMP_HIDDEN), lambda a, g: (a, 0, 0)),
            pl.BlockSpec((1, 1, CMP_HIDDEN), lambda a, g: (a, 0, 0)),
            pl.BlockSpec((1, CMP_HIDDEN, HEAD_DIM), lambda a, g: (a, 0, 0)),
        ],
        out_specs=pl.BlockSpec((1, 1, n_seg, HEAD_DIM), lambda a, g: (a, g, 0, 0)),
        out_shape=jax.ShapeDtypeStruct((2, G, n_seg, HEAD_DIM), F32),
        compiler_params=_cparams(("arbitrary", "arbitrary")),
        name="compress",
    )(segs, pe, w1, b1, w2)


def _hgrn_consts(ch):
    nl = int(np.log2(ch))
    assert 1 << nl == ch
    t = np.arange(ch)[:, None]
    r = np.arange(ch)[None, :]
    mats = [r <= t]
    masks = [np.eye(ch, dtype=bool)]
    for l in range(nl):
        half = 1 << l
        blk = t // (2 * half)
        ref = blk * 2 * half + half - 1
        up = ((t >> l) & 1) == 1
        mats.append(np.where(up, (r > ref) & (r <= t), (r > t) & (r <= ref)))
        masks.append(up & (((r >> l) & 1) == 0) & (blk == r // (2 * half)))
    mc = np.concatenate(mats[:1 + HG_MM_LEVELS], axis=0).astype(np.float32)
    lm = np.stack(masks, axis=0).astype(np.float32)
    return nl, mc, lm


def _hgrn_kernel(q_ref, f_ref, i_ref, z_ref, lbl_ref, nw_ref, mc_ref, lm_ref, o_ref, st_ref, e_ref, *, ch, nl):
    @pl.when(pl.program_id(0) == 0)
    def _():
        st_ref[...] = jnp.zeros_like(st_ref)

    lg = lbl_ref[...]
    ex = jnp.exp(lg - jnp.max(lg, axis=0, keepdims=True))
    lb = ex[0:1] / jnp.sum(ex, axis=0, keepdims=True)
    f = lb + (1.0 - lb) * jax.nn.sigmoid(f_ref[...])
    g_hi, g_lo = _split_bf16(jnp.log(f))
    mc = mc_ref[...]
    e_ref[...] = _dot(mc, g_hi) + _dot(mc, g_lo)
    row = lax.broadcasted_iota(jnp.int32, (ch, 1), 0)
    nw = nw_ref[...]

    heads = [slice(h * HG_DK, (h + 1) * HG_DK) for h in range(HG_HEADS)]
    qs = [q_ref[:, sl] for sl in heads]
    ks = [1.0 - f[:, sl] for sl in heads]
    scs = [_dot_nt(q.astype(BF16), k.astype(BF16)) * lm_ref[0] for q, k in zip(qs, ks)]
    bs = [e_ref[0:ch, sl] for sl in heads]
    for l in range(nl):
        up = ((row >> l) & 1) == 1
        half = 1 << l
        for h, sl in enumerate(heads):
            if l < HG_MM_LEVELS:
                el = e_ref[(l + 1) * ch:(l + 2) * ch, sl]
                xl = jnp.where(up, qs[h], ks[h]) * jnp.exp(el)
            else:
                parts = []
                for r0 in range(0, ch, 2 * half):
                    mid = r0 + half
                    edge = bs[h][mid - 1:mid, :]
                    parts.append(ks[h][r0:mid] * jnp.exp(edge - bs[h][r0:mid]))
                    parts.append(qs[h][mid:mid + half] * jnp.exp(bs[h][mid:mid + half] - edge))
                xl = jnp.concatenate(parts, axis=0)
            xl = xl.astype(BF16)
            scs[h] = scs[h] + _dot_nt(xl, xl) * lm_ref[l + 1]

    vbs = [i_ref[:, sl].astype(BF16) for sl in heads]
    sts = [st_ref[h] for h in range(HG_HEADS)]
    inters = [_dot_nt((qs[h] * jnp.exp(bs[h])).astype(BF16), sts[h].astype(BF16)) for h in range(HG_HEADS)]
    for h, sl in enumerate(heads):
        ke = (ks[h] * jnp.exp(bs[h][ch - 1:ch, :] - bs[h])).astype(BF16)
        upd = lax.dot_general(vbs[h], ke, (((0,), (0,)), ((), ())), preferred_element_type=F32)
        st_ref[h] = sts[h] * jnp.exp(bs[h][ch - 1:ch, :]) + upd
    outs = [inters[h] + _dot(scs[h].astype(BF16), vbs[h]) for h in range(HG_HEADS)]
    for h, sl in enumerate(heads):
        o = outs[h]
        ms = jnp.mean(o * o, axis=-1, keepdims=True)
        z = z_ref[:, sl]
        o_ref[:, sl] = (o * lax.rsqrt(ms + EPS) * nw * (z * jax.nn.sigmoid(z))).astype(o_ref.dtype)


def _hgrn(proj, lb_logits, hg_norm_w, ch=128):
    S = proj.shape[0]
    nl, mc, lm = _hgrn_consts(ch)
    kern = functools.partial(_hgrn_kernel, ch=ch, nl=nl)
    nlb = lb_logits.shape[0]
    return pl.pallas_call(
        kern,
        grid=(S // ch,),
        in_specs=[
            pl.BlockSpec((ch, HG_FDIM), lambda c: (c, 0)),
            pl.BlockSpec((ch, HG_FDIM), lambda c: (c, 1)),
            pl.BlockSpec((ch, HG_WIDTH), lambda c: (c, 2)),
            pl.BlockSpec((ch, HG_WIDTH), lambda c: (c, 3)),
            pl.BlockSpec((nlb, HG_FDIM), lambda c: (0, 0)),
            pl.BlockSpec((1, HG_DV), lambda c: (0, 0)),
            pl.BlockSpec(mc.shape, lambda c: (0, 0)),
            pl.BlockSpec(lm.shape, lambda c: (0, 0, 0)),
        ],
        out_specs=pl.BlockSpec((ch, HG_WIDTH), lambda c: (c, 0)),
        out_shape=jax.ShapeDtypeStruct((S, HG_WIDTH), BF16),
        scratch_shapes=[
            pltpu.VMEM((HG_HEADS, HG_DV, HG_DK), F32),
            pltpu.VMEM((mc.shape[0], HG_FDIM), F32),
        ],
        compiler_params=_cparams(("arbitrary",)),
        name="hgrn",
    )(proj, proj, proj, proj, lb_logits, hg_norm_w.reshape(1, HG_DV), jnp.asarray(mc, BF16), jnp.asarray(lm, F32))


def _stack_heads(q_ref, rows, cosf, sinf):
    qscale = (HEAD_DIM ** -0.5) * LOG2E
    return jnp.concatenate(
        [(_rope(q_ref[rows, h * HEAD_DIM:(h + 1) * HEAD_DIM], cosf, sinf) * qscale).astype(BF16)
         for h in range(NSA_HG)], axis=0)


def _per_head(a):
    return jnp.concatenate([a] * NSA_HG, axis=0)


def _nsa_select_kernel(q_ref, cos_ref, sin_ref, gate_ref, kc_ref, vc_ref, ov_ref, ca_ref, selb_ref, pc_ref, *, nsub):
    n = pl.program_id(1)
    R = NSA_HG * Q_BLOCK
    ncp = kc_ref.shape[2]
    nb = Q_BLOCK // CMP_STRIDE
    subs = range(nsub)
    rows = [slice(i * Q_BLOCK, (i + 1) * Q_BLOCK) for i in subs]
    q0s = [(n * nsub + i) * Q_BLOCK for i in subs]

    crow = lax.broadcasted_iota(jnp.int32, (ncp, LANES), 0)
    clane = lax.broadcasted_iota(jnp.int32, (ncp, LANES), 1)
    kc = kc_ref[0, 0].astype(BF16)
    vca = jnp.concatenate([vc_ref[0, 0].astype(BF16), jnp.ones((ncp, LANES), BF16)], axis=1)
    ca4 = _per_head(ca_ref[...])
    ov = ov_ref[...]

    ss = []
    for i in subs:
        q4 = _stack_heads(q_ref, rows[i], cos_ref[rows[i], :], sin_ref[rows[i], :])
        c_first = (n * nsub + i) * nb - (CMP_LEN // CMP_STRIDE - 1)
        flags = jnp.where(clane == 0, jnp.where(crow >= c_first + nb, NEG_BIG, 0.0),
                          jnp.where(crow == c_first + clane - 1, 1.0, 0.0)).astype(BF16)
        ss.append(_halves(_dot_nt, jnp.concatenate([q4, ca4], axis=1), jnp.concatenate([kc, flags], axis=1)))
    ebs = [jnp.exp2(s - jnp.max(s, axis=-1, keepdims=True)).astype(BF16) for s in ss]
    pvs = [_halves(_dot, e, vca) for e in ebs]

    imps = []
    for i in subs:
        t_row = q0s[i] + (lax.broadcasted_iota(jnp.int32, (R, LANES), 0) & (Q_BLOCK - 1))
        inv = jnp.where(t_row >= CMP_LEN - 1, 1.0, 0.0) / pvs[i][:, LANES:]
        o_cmp = pvs[i][:, :LANES] * inv
        gate = jax.nn.sigmoid(gate_ref[rows[i], :])
        for h in range(NSA_HG):
            pc_ref[rows[i], h * HEAD_DIM:(h + 1) * HEAD_DIM] = (
                gate[:, 3 * h:3 * h + 1] * o_cmp[h * Q_BLOCK:(h + 1) * Q_BLOCK])
        eo = _dot(ebs[i], ov) * inv
        imp = eo[0:Q_BLOCK]
        for h in range(1, NSA_HG):
            imp = imp + eo[h * Q_BLOCK:(h + 1) * Q_BLOCK]
        imps.append(imp)

    nsel_pad = ov.shape[1]
    jj = lax.broadcasted_iota(jnp.int32, (nsel_pad, Q_BLOCK), 0)
    jjf = jj.astype(F32)
    lane_q = lax.broadcasted_iota(jnp.int32, (nsel_pad, Q_BLOCK), 1)
    TAKEN = -2.0
    scores = []
    for i in subs:
        tq = q0s[i] + lane_q
        jt = tq >> 6
        forced = (jj == 0) | (jj == jt) | (jj == jt - 1)
        scores.append(jnp.where(jj * SEL_LEN <= tq, jnp.where(forced, TAKEN, imps[i].T), -1.0))
    for _ in range(SEL_TOPK - 3):
        for i in subs:
            mx = jnp.max(scores[i], axis=0, keepdims=True)
            first = jnp.min(jnp.where(scores[i] == mx, jjf, float(nsel_pad)), axis=0, keepdims=True)
            first = jnp.where(mx >= 0.0, first, -1.0)
            scores[i] = jnp.where(jjf == first, TAKEN, scores[i])
    for i in subs:
        selb_ref[rows[i], :] = jnp.where(scores[i] == TAKEN, 0.0, NEG_BIG).T.astype(BF16)


def _nsa_select(proj, cosf, sinf, kcvc, nsub=4):
    S = proj.shape[0]
    ncp = kcvc.shape[2]
    nsel = S // SEL_LEN
    qs = min(nsub * Q_BLOCK, S)
    nsub = qs // Q_BLOCK
    assert nsel <= LANES and ncp % LANES == 0 and S % qs == 0
    assert CMP_LEN == 2 * CMP_STRIDE and Q_BLOCK % CMP_STRIDE == 0 and Q_BLOCK // CMP_STRIDE < LANES
    ov = jnp.asarray(_overlap_matrix(ncp, LANES, nsel), BF16)
    ca = jnp.asarray(_cmp_staircase(), BF16)
    qw = NSA_HG * HEAD_DIM
    return pl.pallas_call(
        functools.partial(_nsa_select_kernel, nsub=nsub),
        grid=(NSA_KV, S // qs),
        in_specs=[
            pl.BlockSpec((qs, qw), lambda g, n: (n, OFF_Q // qw + g)),
            pl.BlockSpec((qs, LANES), lambda g, n: (n, 0)),
            pl.BlockSpec((qs, LANES), lambda g, n: (n, 0)),
            pl.BlockSpec((qs, LANES), lambda g, n: (n, OFF_G // LANES + g)),
            pl.BlockSpec((1, 1, ncp, HEAD_DIM), lambda g, n: (0, g, 0, 0)),
            pl.BlockSpec((1, 1, ncp, HEAD_DIM), lambda g, n: (1, g, 0, 0)),
            pl.BlockSpec((ncp, LANES), lambda g, n: (0, 0)),
            pl.BlockSpec((Q_BLOCK, LANES), lambda g, n: (0, 0)),
        ],
        out_specs=[
            pl.BlockSpec((qs, LANES), lambda g, n: (n, g)),
            pl.BlockSpec((qs, qw), lambda g, n: (n, g)),
        ],
        out_shape=[
            jax.ShapeDtypeStruct((S, NSA_KV * LANES), BF16),
            jax.ShapeDtypeStruct((S, NSA_WIDTH), F32),
        ],
        compiler_params=_cparams(("arbitrary", "arbitrary")),
        name="nsa_select",
    )(proj, cosf, sinf, proj, kcvc, kcvc, ov, ca)


def _nsa_kernel(q_ref, cos_ref, sin_ref, gate_ref, z_ref, selb_ref, pc_ref, wb_ref, cb_ref,
                ks_ref, vs_ref, kw_ref, vw_ref, o_ref, m_ref, accl_ref, sa_ref, sb_ref, part_ref, *, tk):
    n = pl.program_id(1)
    q0 = n * Q_BLOCK
    R = NSA_HG * Q_BLOCK
    q4 = _stack_heads(q_ref, slice(None), cos_ref[...], sin_ref[...])
    per_head = _per_head

    def tpos(shape):
        return q0 + (lax.broadcasted_iota(jnp.int32, shape, 0) & (Q_BLOCK - 1))

    wspan = WINDOW + Q_BLOCK
    w0 = pl.multiple_of(jnp.maximum(q0 - WINDOW, 0), Q_BLOCK)
    sw = (_halves(_dot_nt, q4, kw_ref[pl.ds(w0, wspan), :])
          + per_head(wb_ref[jnp.minimum(n, WINDOW // Q_BLOCK)]))
    ew = jnp.exp2(sw - jnp.max(sw, axis=-1, keepdims=True))
    vwa = jnp.concatenate([vw_ref[pl.ds(w0, wspan), :], jnp.ones((wspan, LANES), BF16)], axis=1)
    pvw = _halves(_dot, ew.astype(BF16), vwa)
    o_win = pvw[:, :LANES] / pvw[:, LANES:]

    gate = jax.nn.sigmoid(gate_ref[...])
    for h in range(NSA_HG):
        cs = slice(h * HEAD_DIM, (h + 1) * HEAD_DIM)
        part_ref[:, cs] = pc_ref[:, cs] + gate[:, 3 * h + 2:3 * h + 3] * o_win[h * Q_BLOCK:(h + 1) * Q_BLOCK]

    sel_bias = per_head(selb_ref[...])

    m_ref[...] = jnp.full(m_ref.shape, NEG_BIG, F32)
    accl_ref[...] = jnp.zeros(accl_ref.shape, F32)
    qa = jnp.concatenate([q4, sel_bias], axis=1)
    key_blk = lax.broadcasted_iota(jnp.int32, (tk, LANES), 0) >> 6
    blk_lane = lax.broadcasted_iota(jnp.int32, (tk, LANES), 1)
    ones_k = jnp.ones((tk, LANES), BF16)
    rep = tk // LANES

    n_q = tk // Q_BLOCK
    last = lax.shift_right_logical(n, n_q.bit_length() - 1)
    diag = n - last * n_q

    def score_tile(kt, dst_ref):
        k0 = pl.multiple_of(kt * tk, tk)
        onehot = jnp.where(key_blk + kt * (tk // SEL_LEN) == blk_lane, 1.0, 0.0).astype(BF16)
        dst_ref[...] = _dot_nt(qa, jnp.concatenate([ks_ref[pl.ds(k0, tk), :], onehot], axis=1))

    def consume_tile(kt, src_ref, causal=False):
        k0 = pl.multiple_of(kt * tk, tk)
        sc = src_ref[...]
        if causal:
            sc = sc + per_head(cb_ref[diag])
        m_prev = m_ref[...]
        m_next = jnp.maximum(m_prev, jnp.max(sc, axis=-1, keepdims=True))
        pr = jnp.exp2(sc - jnp.concatenate([m_next] * rep, axis=1))
        alpha = jnp.exp2(m_prev - m_next)
        va = jnp.concatenate([vs_ref[pl.ds(k0, tk), :], ones_k], axis=1)
        accl_ref[...] = jnp.concatenate([alpha, alpha], axis=1) * accl_ref[...] + _dot(pr.astype(BF16), va)
        m_ref[...] = m_next

    score_tile(0, sa_ref)

    def slc_pair(i, carry):
        score_tile(2 * i + 1, sb_ref)
        consume_tile(2 * i, sa_ref)
        score_tile(2 * i + 2, sa_ref)
        consume_tile(2 * i + 1, sb_ref)
        return carry

    lax.fori_loop(0, lax.shift_right_logical(last, 1), slc_pair, 0)

    @pl.when((last & 1) == 0)
    def _():
        consume_tile(last, sa_ref, causal=True)

    @pl.when((last & 1) == 1)
    def _():
        score_tile(last, sb_ref)
        consume_tile(last - 1, sa_ref)
        consume_tile(last, sb_ref, causal=True)

    o_slc = accl_ref[:, :LANES] / accl_ref[:, LANES:]

    gsig = jax.nn.sigmoid(gate_ref[...])
    for h in range(NSA_HG):
        rs = slice(h * Q_BLOCK, (h + 1) * Q_BLOCK)
        cs = slice(h * HEAD_DIM, (h + 1) * HEAD_DIM)
        oh = part_ref[:, cs] + gsig[:, 3 * h + 1:3 * h + 2] * o_slc[rs]
        z = z_ref[:, cs]
        o_ref[:, cs] = (oh * (z * jax.nn.sigmoid(z))).astype(o_ref.dtype)


def _cmp_staircase():
    nb = Q_BLOCK // CMP_STRIDE
    a = np.zeros((Q_BLOCK, LANES), np.float32)
    a[:, 0] = 1.0
    r = np.arange(Q_BLOCK)[:, None]
    i = np.arange(nb)[None, :]
    a[:, 1:1 + nb] = np.where(r < CMP_STRIDE * i + (CMP_STRIDE - 1), NEG_BIG, 0.0)
    return a


def _window_masks():
    n_w = WINDOW // Q_BLOCK
    r = np.arange(Q_BLOCK)[None, :, None]
    c = np.arange(WINDOW + Q_BLOCK)[None, None, :]
    d = np.arange(n_w + 1)[:, None, None]
    ok = np.where(d < n_w, c <= Q_BLOCK * d + r, (c > r) & (c <= WINDOW + r))
    return np.where(ok, 0.0, NEG_BIG).astype(np.float32)


def _causal_staircases(tk):
    n_q = tk // Q_BLOCK
    r = np.arange(Q_BLOCK)[None, :, None]
    c = np.arange(tk)[None, None, :]
    d = np.arange(n_q + 1)[:, None, None]
    return np.where((c <= Q_BLOCK * d + r) | (d == n_q), 0.0, NEG_BIG).astype(np.float32)


def _overlap_matrix(ncp, nsel_pad, nsel):
    ci = np.arange(ncp)[:, None] * CMP_STRIDE
    sj = np.arange(nsel_pad)[None, :] * SEL_LEN
    ov = (ci < sj + SEL_LEN) & (ci + CMP_LEN > sj) & (np.arange(nsel_pad)[None, :] < nsel) & (np.arange(ncp)[:, None] < ncp - 1)
    return ov.astype(np.float32)


def _nsa(proj, cosf, sinf, selb, pcmp, kvb, tk=512):
    S = proj.shape[0]
    assert S % tk == 0 and S >= WINDOW + Q_BLOCK and WINDOW % Q_BLOCK == 0
    wb = jnp.asarray(_window_masks(), F32)
    n_q = tk // Q_BLOCK
    assert n_q & (n_q - 1) == 0 and tk % SEL_LEN == 0
    cb = jnp.asarray(_causal_staircases(tk), F32)
    qw = NSA_HG * HEAD_DIM
    R = NSA_HG * Q_BLOCK
    kern = functools.partial(_nsa_kernel, tk=tk)
    return pl.pallas_call(
        kern,
        grid=(NSA_KV, S // Q_BLOCK),
        in_specs=[
            pl.BlockSpec((Q_BLOCK, qw), lambda g, n: (n, OFF_Q // qw + g)),
            pl.BlockSpec((Q_BLOCK, LANES), lambda g, n: (n, 0)),
            pl.BlockSpec((Q_BLOCK, LANES), lambda g, n: (n, 0)),
            pl.BlockSpec((Q_BLOCK, LANES), lambda g, n: (n, OFF_G // LANES + g)),
            pl.BlockSpec((Q_BLOCK, qw), lambda g, n: (n, OFF_Z // qw + g)),
            pl.BlockSpec((Q_BLOCK, LANES), lambda g, n: (n, g)),
            pl.BlockSpec((Q_BLOCK, qw), lambda g, n: (n, g)),
            pl.BlockSpec((WINDOW // Q_BLOCK + 1, Q_BLOCK, WINDOW + Q_BLOCK), lambda g, n: (0, 0, 0)),
            pl.BlockSpec((n_q + 1, Q_BLOCK, tk), lambda g, n: (0, 0, 0)),
            pl.BlockSpec((S, HEAD_DIM), lambda g, n: (0, 0 * NSA_KV + g)),
            pl.BlockSpec((S, HEAD_DIM), lambda g, n: (0, 1 * NSA_KV + g)),
            pl.BlockSpec((S, HEAD_DIM), lambda g, n: (0, 2 * NSA_KV + g)),
            pl.BlockSpec((S, HEAD_DIM), lambda g, n: (0, 3 * NSA_KV + g)),
        ],
        out_specs=pl.BlockSpec((Q_BLOCK, qw), lambda g, n: (n, g)),
        out_shape=jax.ShapeDtypeStruct((S, NSA_WIDTH), BF16),
        scratch_shapes=[
            pltpu.VMEM((R, LANES), F32),
            pltpu.VMEM((R, HEAD_DIM + LANES), F32),
            pltpu.VMEM((R, tk), F32),
            pltpu.VMEM((R, tk), F32),
            pltpu.VMEM((Q_BLOCK, qw), F32),
        ],
        compiler_params=_cparams(("arbitrary", "arbitrary")),
        name="nsa",
    )(proj, cosf, sinf, proj, proj, selb, pcmp, wb, cb, kvb, kvb, kvb, kvb)


def _merge_kernel(ohg_ref, onsa_ref, g0_ref, g1_ref, g2_ref, g3_ref, whg_ref, wnsa_ref, o_ref):
    y_hg = _dot(ohg_ref[...], whg_ref[...])
    y_nsa = _dot(onsa_ref[...], wnsa_ref[...])
    half = D_MODEL // 2
    gh = (g0_ref, g1_ref)
    gn = (g2_ref, g3_ref)
    for c in range(2):
        cs = slice(c * half, (c + 1) * half)
        m = jax.nn.sigmoid(gh[c][...]) * y_hg[:, cs] + jax.nn.sigmoid(gn[c][...]) * y_nsa[:, cs]
        o_ref[:, cs] = m.astype(o_ref.dtype)


def _merge(o_hg, o_nsa, proj, w_hg, w_nsa, tm=256):
    S = o_hg.shape[0]
    tm = min(tm, S)
    half = D_MODEL // 2
    gb = OFF_MG // half
    return pl.pallas_call(
        _merge_kernel,
        grid=(S // tm,),
        in_specs=[
            pl.BlockSpec((tm, HG_WIDTH), lambda i: (i, 0)),
            pl.BlockSpec((tm, NSA_WIDTH), lambda i: (i, 0)),
            pl.BlockSpec((tm, half), lambda i: (i, gb + 0)),
            pl.BlockSpec((tm, half), lambda i: (i, gb + 1)),
            pl.BlockSpec((tm, half), lambda i: (i, gb + 2)),
            pl.BlockSpec((tm, half), lambda i: (i, gb + 3)),
            pl.BlockSpec((HG_WIDTH, D_MODEL), lambda i: (0, 0)),
            pl.BlockSpec((NSA_WIDTH, D_MODEL), lambda i: (0, 0)),
        ],
        out_specs=pl.BlockSpec((tm, D_MODEL), lambda i: (i, 0)),
        out_shape=jax.ShapeDtypeStruct((S, D_MODEL), BF16),
        compiler_params=_cparams(("arbitrary",)),
        name="merge",
    )(o_hg, o_nsa, proj, proj, proj, proj, w_hg, w_nsa)


def _out_kernel(x_ref, m_ref, w_ref, nw_ref, o_ref):
    h = x_ref[...] + _dot(m_ref[...], w_ref[...])
    ms = jnp.mean(h * h, axis=-1, keepdims=True)
    o_ref[...] = h * lax.rsqrt(ms + EPS) * nw_ref[...]


def _out(x2, merged, w_out, final_w, tm=256):
    S, D = x2.shape
    tm = min(tm, S)
    return pl.pallas_call(
        _out_kernel,
        grid=(S // tm,),
        in_specs=[
            pl.BlockSpec((tm, D), lambda i: (i, 0)),
            pl.BlockSpec((tm, D), lambda i: (i, 0)),
            pl.BlockSpec((D, D), lambda i: (0, 0)),
            pl.BlockSpec((1, D), lambda i: (0, 0)),
        ],
        out_specs=pl.BlockSpec((tm, D), lambda i: (i, 0)),
        out_shape=jax.ShapeDtypeStruct((S, D), F32),
        compiler_params=_cparams(("arbitrary",)),
        name="out_proj",
    )(x2, merged, w_out, final_w.reshape(1, D))


def _layer(x2, norm_w, w_in, lb_logits, hg_norm_w, cmp_k_pos, cmp_k_w1, cmp_k_b1, cmp_k_w2,
           cmp_v_pos, cmp_v_w1, cmp_v_b1, cmp_v_w2, w_branch_hg, w_branch_nsa, w_out, final_w):
    S = x2.shape[0]
    proj = _norm_proj(x2, norm_w, _w_prep(w_in.T))
    cosf, sinf = _rope_tables(S)
    segs, kvb = _kv_prep(proj, cosf, sinf)
    pe = jnp.stack([cmp_k_pos.reshape(1, -1), cmp_v_pos.reshape(1, -1)])
    w1 = jnp.stack([cmp_k_w1, cmp_v_w1]).astype(BF16)
    b1 = jnp.stack([cmp_k_b1.reshape(1, -1), cmp_v_b1.reshape(1, -1)])
    w2 = jnp.stack([cmp_k_w2, cmp_v_w2]).astype(BF16)
    kcvc = _compress(segs, pe, w1, b1, w2)
    o_hg = _hgrn(proj, lb_logits, hg_norm_w)
    selb, pcmp = _nsa_select(proj, cosf, sinf, kcvc)
    o_nsa = _nsa(proj, cosf, sinf, selb, pcmp, kvb)
    merged = _merge(o_hg, o_nsa, proj, w_branch_hg.astype(BF16), w_branch_nsa.astype(BF16))
    return _out(x2, merged, w_out.astype(BF16), final_w)


def kernel(x, norm_w, w_in, hg_lb_logits, hg_norm_w, cmp_k_pos, cmp_k_w1, cmp_k_b1, cmp_k_w2, cmp_v_pos, cmp_v_w1, cmp_v_b1, cmp_v_w2, w_branch_hg, w_branch_nsa, w_out, final_norm_w):
    B, S, D = x.shape
    assert B == 1 and D == D_MODEL and norm_w.shape[0] == 1
    y = _layer(x[0], norm_w[0], w_in[0], hg_lb_logits, hg_norm_w[0], cmp_k_pos[0], cmp_k_w1[0], cmp_k_b1[0],
               cmp_k_w2[0], cmp_v_pos[0], cmp_v_w1[0], cmp_v_b1[0], cmp_v_w2[0], w_branch_hg[0],
               w_branch_nsa[0], w_out[0], final_norm_w)
    return y[None]
```

```python
import functools

import numpy as np
import jax
import jax.numpy as jnp
from jax import lax
from jax.experimental import pallas as pl
from jax.experimental.pallas import tpu as pltpu

F32 = jnp.float32
BF16 = jnp.bfloat16

D_MODEL = 2048
HG_HEADS = 8
HG_DK = 128
HG_DV = 128
HG_FDIM = HG_HEADS * HG_DK
HG_WIDTH = HG_HEADS * HG_DV
HG_MM_LEVELS = 3
NSA_HEADS = 16
NSA_KV = 4
NSA_HG = NSA_HEADS // NSA_KV
HEAD_DIM = 128
NSA_WIDTH = NSA_HEADS * HEAD_DIM
NSA_KVW = NSA_KV * HEAD_DIM
CMP_LEN = 32
CMP_STRIDE = 16
CMP_HIDDEN = 512
SEL_LEN = 64
SEL_TOPK = 16
WINDOW = 512
Q_BLOCK = 256
ROPE_THETA = 500000.0
ROT_DIM = HEAD_DIM // 4
ROT_HALF = ROT_DIM // 2
EPS = 1e-6
LOG2E = float(np.log2(np.e))

LANES = 128
NEG_BIG = -1e30
VMEM_LIMIT = 56 * 1024 * 1024

OFF_HG = 0
OFF_Q = 4 * HG_FDIM
OFF_KV = OFF_Q + NSA_WIDTH
OFF_MG = OFF_KV + 6 * NSA_KVW
OFF_Z = OFF_MG + 2 * D_MODEL
OFF_G = OFF_Z + NSA_WIDTH
PROJ_COLS = OFF_G + NSA_KV * LANES
IN_GATE_OFF = OFF_MG
IN_Z_OFF = IN_GATE_OFF + 3 * NSA_HEADS
IN_MG_OFF = IN_Z_OFF + NSA_WIDTH


def _dot(a, b):
    return jnp.dot(a, b, preferred_element_type=F32)


def _dot_nt(a, b):
    return lax.dot_general(a, b, (((1,), (1,)), ((), ())), preferred_element_type=F32)


def _halves(dot, a, b):
    h = a.shape[0] // 2
    return jnp.concatenate([dot(a[:h], b), dot(a[h:], b)], axis=0)


def _split_bf16(a):
    hi = a.astype(BF16)
    lo = (a - hi.astype(F32)).astype(BF16)
    return hi, lo


def _cparams(sem):
    return pltpu.CompilerParams(dimension_semantics=sem, vmem_limit_bytes=VMEM_LIMIT)


def _w_prep_kernel(wt_ref, wg_ref, o_ref, *, n_direct):
    j = pl.program_id(0)

    @pl.when(j < n_direct)
    def _():
        o_ref[...] = wt_ref[...].astype(BF16)

    @pl.when(j >= n_direct)
    def _():
        o_ref[...] = wg_ref[...].astype(BF16)


def _w_prep(wt, tn=512):
    D = wt.shape[1]
    n_main, n_mg, n_z = OFF_MG // tn, 2 * D_MODEL // tn, NSA_WIDTH // tn
    n_direct = n_main + n_mg + n_z
    assert OFF_MG % tn == 0 and D_MODEL % tn == 0 and NSA_KV * LANES == tn
    gates = wt[IN_GATE_OFF:IN_Z_OFF].reshape(NSA_KV, 3 * NSA_HG, D)
    gates = jnp.pad(gates, ((0, 0), (0, LANES - 3 * NSA_HG), (0, 0))).reshape(NSA_KV * LANES, D)
    n_extra = 2
    gates = jnp.pad(gates, ((0, (n_extra - 1) * tn), (0, 0)))

    sub = 8
    assert IN_MG_OFF % sub == 0 and IN_Z_OFF % sub == 0 and tn % sub == 0

    def src_row(j):
        in_mg = IN_MG_OFF // sub + (j - n_main) * (tn // sub)
        in_z = IN_Z_OFF // sub + (jnp.minimum(j, n_direct - 1) - n_main - n_mg) * (tn // sub)
        return jnp.where(j < n_main, j * (tn // sub), jnp.where(j < n_main + n_mg, in_mg, in_z)) * sub

    return pl.pallas_call(
        functools.partial(_w_prep_kernel, n_direct=n_direct),
        grid=(n_direct + n_extra,),
        in_specs=[
            pl.BlockSpec((pl.Element(tn), pl.Element(D)), lambda j: (src_row(j), 0)),
            pl.BlockSpec((tn, D), lambda j: (jnp.maximum(j - n_direct, 0), 0)),
        ],
        out_specs=pl.BlockSpec((tn, D), lambda j: (j, 0)),
        out_shape=jax.ShapeDtypeStruct(((n_direct + n_extra) * tn, D), BF16),
        compiler_params=_cparams(("arbitrary",)),
        name="w_prep",
    )(wt, gates)


def _norm_proj_kernel(x_ref, nw_ref, w_ref, o_ref, xn_ref):
    @pl.when(pl.program_id(1) == 0)
    def _():
        x = x_ref[...]
        ms = jnp.mean(x * x, axis=-1, keepdims=True)
        xn_ref[...] = (x * lax.rsqrt(ms + EPS) * nw_ref[...]).astype(BF16)

    o_ref[...] = _dot_nt(xn_ref[...], w_ref[...])


def _norm_proj(x2, norm_w, wtb, tm=1024, tn=1024):
    S, D = x2.shape
    N = wtb.shape[0]
    tm = min(tm, S)
    return pl.pallas_call(
        _norm_proj_kernel,
        grid=(S // tm, N // tn),
        in_specs=[
            pl.BlockSpec((tm, D), lambda i, j: (i, 0)),
            pl.BlockSpec((1, D), lambda i, j: (0, 0)),
            pl.BlockSpec((tn, D), lambda i, j: (j, 0)),
        ],
        out_specs=pl.BlockSpec((tm, tn), lambda i, j: (i, j)),
        out_shape=jax.ShapeDtypeStruct((S, N), F32),
        scratch_shapes=[pltpu.VMEM((tm, D), BF16)],
        compiler_params=_cparams(("arbitrary", "arbitrary")),
        name="norm_proj",
    )(x2, norm_w.reshape(1, D), wtb)


def _rope(x, cosf, sinf):
    lane = lax.broadcasted_iota(jnp.int32, x.shape, 1)
    rot = jnp.where(lane < ROT_HALF, pltpu.roll(x, LANES - ROT_HALF, 1), pltpu.roll(x, ROT_HALF, 1))
    return x * cosf + rot * sinf


def _rope_tables(S):
    pos = np.arange(S, dtype=np.float64)
    inv = ROPE_THETA ** (-np.arange(0, ROT_DIM, 2, dtype=np.float64) / ROT_DIM)
    ang = pos[:, None] * inv[None, :]
    cos, sin = np.cos(ang), np.sin(ang)
    rest = LANES - ROT_DIM
    cosf = np.concatenate([cos, cos, np.ones((S, rest))], axis=1)
    sinf = np.concatenate([-sin, sin, np.zeros((S, rest))], axis=1)
    return jnp.asarray(cosf, F32), jnp.asarray(sinf, F32)


def _kv_prep_kernel(kv_ref, cos_ref, sin_ref, cmp_ref, kvb_ref, row_ref):
    cosf = cos_ref[...]
    sinf = sin_ref[...]
    W = NSA_KVW
    n_seg = row_ref.shape[0] // CMP_STRIDE

    def to_segments(a, g, rows):
        row_ref[...] = rows
        for l in range(CMP_STRIDE):
            cmp_ref[a, g, :, l * HEAD_DIM:(l + 1) * HEAD_DIM] = row_ref[pl.ds(l, n_seg, stride=CMP_STRIDE), :]

    for g in range(NSA_KV):
        kc = kv_ref[:, 0 * W + g * LANES:0 * W + (g + 1) * LANES]
        to_segments(0, g, _rope(kc, cosf, sinf))
        to_segments(1, g, kv_ref[:, 1 * W + g * LANES:1 * W + (g + 1) * LANES])
        ks = kv_ref[:, 2 * W + g * LANES:2 * W + (g + 1) * LANES]
        kvb_ref[:, 0 * W + g * LANES:0 * W + (g + 1) * LANES] = _rope(ks, cosf, sinf).astype(BF16)
        kvb_ref[:, 1 * W + g * LANES:1 * W + (g + 1) * LANES] = kv_ref[:, 3 * W + g * LANES:3 * W + (g + 1) * LANES].astype(BF16)
        kw = kv_ref[:, 4 * W + g * LANES:4 * W + (g + 1) * LANES]
        kvb_ref[:, 2 * W + g * LANES:2 * W + (g + 1) * LANES] = _rope(kw, cosf, sinf).astype(BF16)
        kvb_ref[:, 3 * W + g * LANES:3 * W + (g + 1) * LANES] = kv_ref[:, 5 * W + g * LANES:5 * W + (g + 1) * LANES].astype(BF16)


def _kv_prep(proj, cosf, sinf, tm=512):
    S = proj.shape[0]
    tm = min(tm, S)
    kvw = 6 * NSA_KVW
    return pl.pallas_call(
        _kv_prep_kernel,
        grid=(S // tm,),
        in_specs=[
            pl.BlockSpec((tm, kvw), lambda i: (i, OFF_KV // kvw)),
            pl.BlockSpec((tm, LANES), lambda i: (i, 0)),
            pl.BlockSpec((tm, LANES), lambda i: (i, 0)),
        ],
        out_specs=[
            pl.BlockSpec((2, NSA_KV, tm // CMP_STRIDE, CMP_STRIDE * HEAD_DIM), lambda i: (0, 0, i, 0)),
            pl.BlockSpec((tm, 4 * NSA_KVW), lambda i: (i, 0)),
        ],
        out_shape=[
            jax.ShapeDtypeStruct((2, NSA_KV, S // CMP_STRIDE, CMP_STRIDE * HEAD_DIM), F32),
            jax.ShapeDtypeStruct((S, 4 * NSA_KVW), BF16),
        ],
        scratch_shapes=[pltpu.VMEM((tm, HEAD_DIM), F32)],
        compiler_params=_cparams(("arbitrary",)),
        name="kv_prep",
    )(proj, cosf, sinf)


def _compress_kernel(seg_ref, pe_ref, w1_ref, b1_ref, w2_ref, o_ref):
    half = (CMP_LEN // 2) * HEAD_DIM
    seg = seg_ref[0, 0]
    n_seg = seg.shape[0]
    pe = pe_ref[0]
    a = (seg + pe[:, :half]).astype(BF16)
    b = (seg + pe[:, half:]).astype(BF16)
    u = _dot(a, w1_ref[0, :half, :])
    v = _dot(b, w1_ref[0, half:, :])
    v_next = pltpu.roll(v, n_seg - 1, 0)
    pre = u + v_next + b1_ref[0]
    h = 0.5 * pre * (1.0 + jnp.tanh(np.sqrt(2.0 / np.pi).astype(np.float32) * (pre + 0.044715 * (pre * pre * pre))))
    o_ref[0, 0] = _dot(h.astype(BF16), w2_ref[0])


def _compress(segs, pe, w1, b1, w2):
    _, G, n_seg, segw = segs.shape
    return pl.pallas_call(
        _compress_kernel,
        grid=(2, G),
        in_specs=[
            pl.BlockSpec((1, 1, n_seg, segw), lambda a, g: (a, g, 0, 0)),
            pl.BlockSpec((1, 1, 2 * segw), lambda a, g: (a, 0, 0)),
            pl.BlockSpec((1, 2 * segw, CMP_HIDDEN), lambda a, g: (a, 0, 0)),
            pl.BlockSpec((1, 1, CMP_HIDDEN), lambda a, g: (a, 0, 0)),
            pl.BlockSpec((1, CMP_HIDDEN, HEAD_DIM), lambda a, g: (a, 0, 0)),
        ],
        out_specs=pl.BlockSpec((1, 1, n_seg, HEAD_DIM), lambda a, g: (a, g, 0, 0)),
        out_shape=jax.ShapeDtypeStruct((2, G, n_seg, HEAD_DIM), F32),
        compiler_params=_cparams(("arbitrary", "arbitrary")),
        name="compress",
    )(segs, pe, w1, b1, w2)


def _hgrn_consts(ch):
    nl = int(np.log2(ch))
    assert 1 << nl == ch
    t = np.arange(ch)[:, None]
    r = np.arange(ch)[None, :]
    mats = [r <= t]
    masks = [np.eye(ch, dtype=bool)]
    for l in range(nl):
        half = 1 << l
        blk = t // (2 * half)
        ref = blk * 2 * half + half - 1
        up = ((t >> l) & 1) == 1
        mats.append(np.where(up, (r > ref) & (r <= t), (r > t) & (r <= ref)))
        masks.append(up & (((r >> l) & 1) == 0) & (blk == r // (2 * half)))
    mc = np.concatenate(mats[:1 + HG_MM_LEVELS], axis=0).astype(np.float32)
    lm = np.stack(masks, axis=0).astype(np.float32)
    return nl, mc, lm


def _hgrn_kernel(q_ref, f_ref, i_ref, z_ref, lbl_ref, nw_ref, mc_ref, lm_ref, o_ref, st_ref, e_ref, *, ch, nl):
    @pl.when(pl.program_id(0) == 0)
    def _():
        st_ref[...] = jnp.zeros_like(st_ref)

    lg = lbl_ref[...]
    ex = jnp.exp(lg - jnp.max(lg, axis=0, keepdims=True))
    lb = ex[0:1] / jnp.sum(ex, axis=0, keepdims=True)
    f = lb + (1.0 - lb) * jax.nn.sigmoid(f_ref[...])
    g_hi, g_lo = _split_bf16(jnp.log(f))
    mc = mc_ref[...]
    e_ref[...] = _dot(mc, g_hi) + _dot(mc, g_lo)
    row = lax.broadcasted_iota(jnp.int32, (ch, 1), 0)
    nw = nw_ref[...]

    heads = [slice(h * HG_DK, (h + 1) * HG_DK) for h in range(HG_HEADS)]
    qs = [q_ref[:, sl] for sl in heads]
    ks = [1.0 - f[:, sl] for sl in heads]
    scs = [_dot_nt(q.astype(BF16), k.astype(BF16)) * lm_ref[0] for q, k in zip(qs, ks)]
    bs = [e_ref[0:ch, sl] for sl in heads]
    for l in range(nl):
        up = ((row >> l) & 1) == 1
        half = 1 << l
        for h, sl in enumerate(heads):
            if l < HG_MM_LEVELS:
                el = e_ref[(l + 1) * ch:(l + 2) * ch, sl]
                xl = jnp.where(up, qs[h], ks[h]) * jnp.exp(el)
            else:
                parts = []
                for r0 in range(0, ch, 2 * half):
                    mid = r0 + half
                    edge = bs[h][mid - 1:mid, :]
                    parts.append(ks[h][r0:mid] * jnp.exp(edge - bs[h][r0:mid]))
                    parts.append(qs[h][mid:mid + half] * jnp.exp(bs[h][mid:mid + half] - edge))
                xl = jnp.concatenate(parts, axis=0)
            xl = xl.astype(BF16)
            scs[h] = scs[h] + _dot_nt(xl, xl) * lm_ref[l + 1]

    vbs = [i_ref[:, sl].astype(BF16) for sl in heads]
    sts = [st_ref[h] for h in range(HG_HEADS)]
    inters = [_dot_nt((qs[h] * jnp.exp(bs[h])).astype(BF16), sts[h].astype(BF16)) for h in range(HG_HEADS)]
    for h, sl in enumerate(heads):
        ke = (ks[h] * jnp.exp(bs[h][ch - 1:ch, :] - bs[h])).astype(BF16)
        upd = lax.dot_general(vbs[h], ke, (((0,), (0,)), ((), ())), preferred_element_type=F32)
        st_ref[h] = sts[h] * jnp.exp(bs[h][ch - 1:ch, :]) + upd
    outs = [inters[h] + _dot(scs[h].astype(BF16), vbs[h]) for h in range(HG_HEADS)]
    for h, sl in enumerate(heads):
        o = outs[h]
        ms = jnp.mean(o * o, axis=-1, keepdims=True)
        z = z_ref[:, sl]
        o_ref[:, sl] = (o * lax.rsqrt(ms + EPS) * nw * (z * jax.nn.sigmoid(z))).astype(o_ref.dtype)


def _hgrn(proj, lb_logits, hg_norm_w, ch=128):
    S = proj.shape[0]
    nl, mc, lm = _hgrn_consts(ch)
    kern = functools.partial(_hgrn_kernel, ch=ch, nl=nl)
    nlb = lb_logits.shape[0]
    return pl.pallas_call(
        kern,
        grid=(S // ch,),
        in_specs=[
            pl.BlockSpec((ch, HG_FDIM), lambda c: (c, 0)),
            pl.BlockSpec((ch, HG_FDIM), lambda c: (c, 1)),
            pl.BlockSpec((ch, HG_WIDTH), lambda c: (c, 2)),
            pl.BlockSpec((ch, HG_WIDTH), lambda c: (c, 3)),
            pl.BlockSpec((nlb, HG_FDIM), lambda c: (0, 0)),
            pl.BlockSpec((1, HG_DV), lambda c: (0, 0)),
            pl.BlockSpec(mc.shape, lambda c: (0, 0)),
            pl.BlockSpec(lm.shape, lambda c: (0, 0, 0)),
        ],
        out_specs=pl.BlockSpec((ch, HG_WIDTH), lambda c: (c, 0)),
        out_shape=jax.ShapeDtypeStruct((S, HG_WIDTH), BF16),
        scratch_shapes=[
            pltpu.VMEM((HG_HEADS, HG_DV, HG_DK), F32),
            pltpu.VMEM((mc.shape[0], HG_FDIM), F32),
        ],
        compiler_params=_cparams(("arbitrary",)),
        name="hgrn",
    )(proj, proj, proj, proj, lb_logits, hg_norm_w.reshape(1, HG_DV), jnp.asarray(mc, BF16), jnp.asarray(lm, F32))


def _stack_heads(q_ref, rows, cosf, sinf):
    qscale = (HEAD_DIM ** -0.5) * LOG2E
    return jnp.concatenate(
        [(_rope(q_ref[rows, h * HEAD_DIM:(h + 1) * HEAD_DIM], cosf, sinf) * qscale).astype(BF16)
         for h in range(NSA_HG)], axis=0)


def _per_head(a):
    return jnp.concatenate([a] * NSA_HG, axis=0)


def _nsa_select_kernel(q_ref, cos_ref, sin_ref, gate_ref, kc_ref, vc_ref, ov_ref, ca_ref,
                       selb_ref, pc_ref, qb_ref, *, nsub):
    n = pl.program_id(1)
    R = NSA_HG * Q_BLOCK
    ncp = kc_ref.shape[2]
    nb = Q_BLOCK // CMP_STRIDE
    subs = range(nsub)
    rows = [slice(i * Q_BLOCK, (i + 1) * Q_BLOCK) for i in subs]
    q0s = [(n * nsub + i) * Q_BLOCK for i in subs]

    crow = lax.broadcasted_iota(jnp.int32, (ncp, LANES), 0)
    clane = lax.broadcasted_iota(jnp.int32, (ncp, LANES), 1)
    kc = kc_ref[0, 0].astype(BF16)
    vca = jnp.concatenate([vc_ref[0, 0].astype(BF16), jnp.ones((ncp, LANES), BF16)], axis=1)
    ca4 = _per_head(ca_ref[...])
    ov = ov_ref[...]

    ss = []
    for i in subs:
        q4 = _stack_heads(q_ref, rows[i], cos_ref[rows[i], :], sin_ref[rows[i], :])
        for h in range(NSA_HG):
            qb_ref[rows[i], h * HEAD_DIM:(h + 1) * HEAD_DIM] = q4[h * Q_BLOCK:(h + 1) * Q_BLOCK]
        c_first = (n * nsub + i) * nb - (CMP_LEN // CMP_STRIDE - 1)
        flags = jnp.where(clane == 0, jnp.where(crow >= c_first + nb, NEG_BIG, 0.0),
                          jnp.where(crow == c_first + clane - 1, 1.0, 0.0)).astype(BF16)
        ss.append(_halves(_dot_nt, jnp.concatenate([q4, ca4], axis=1), jnp.concatenate([kc, flags], axis=1)))
    ebs = [jnp.exp2(s - jnp.max(s, axis=-1, keepdims=True)).astype(BF16) for s in ss]
    pvs = [_halves(_dot, e, vca) for e in ebs]

    imps = []
    for i in subs:
        t_row = q0s[i] + (lax.broadcasted_iota(jnp.int32, (R, LANES), 0) & (Q_BLOCK - 1))
        inv = jnp.where(t_row >= CMP_LEN - 1, 1.0, 0.0) / pvs[i][:, LANES:]
        o_cmp = pvs[i][:, :LANES] * inv
        gate = jax.nn.sigmoid(gate_ref[rows[i], :])
        for h in range(NSA_HG):
            pc_ref[rows[i], h * HEAD_DIM:(h + 1) * HEAD_DIM] = (
                gate[:, 3 * h:3 * h + 1] * o_cmp[h * Q_BLOCK:(h + 1) * Q_BLOCK])
        eo = _dot(ebs[i], ov) * inv
        imp = eo[0:Q_BLOCK]
        for h in range(1, NSA_HG):
            imp = imp + eo[h * Q_BLOCK:(h + 1) * Q_BLOCK]
        imps.append(imp)

    nsel_pad = ov.shape[1]
    jj = lax.broadcasted_iota(jnp.int32, (nsel_pad, Q_BLOCK), 0)
    jjf = jj.astype(F32)
    lane_q = lax.broadcasted_iota(jnp.int32, (nsel_pad, Q_BLOCK), 1)
    TAKEN = -2.0
    scores = []
    for i in subs:
        tq = q0s[i] + lane_q
        jt = tq >> 6
        forced = (jj == 0) | (jj == jt) | (jj == jt - 1)
        scores.append(jnp.where(jj * SEL_LEN <= tq, jnp.where(forced, TAKEN, imps[i].T), -1.0))
    for _ in range(SEL_TOPK - 3):
        for i in subs:
            mx = jnp.max(scores[i], axis=0, keepdims=True)
            first = jnp.min(jnp.where(scores[i] == mx, jjf, float(nsel_pad)), axis=0, keepdims=True)
            first = jnp.where(mx >= 0.0, first, -1.0)
            scores[i] = jnp.where(jjf == first, TAKEN, scores[i])
    for i in subs:
        selb_ref[rows[i], :] = jnp.where(scores[i] == TAKEN, 0.0, NEG_BIG).T.astype(BF16)


def _nsa_select(proj, cosf, sinf, kcvc, nsub=4):
    S = proj.shape[0]
    ncp = kcvc.shape[2]
    nsel = S // SEL_LEN
    qs = min(nsub * Q_BLOCK, S)
    nsub = qs // Q_BLOCK
    assert nsel <= LANES and ncp % LANES == 0 and S % qs == 0
    assert CMP_LEN == 2 * CMP_STRIDE and Q_BLOCK % CMP_STRIDE == 0 and Q_BLOCK // CMP_STRIDE < LANES
    ov = jnp.asarray(_overlap_matrix(ncp, LANES, nsel), BF16)
    ca = jnp.asarray(_cmp_staircase(), BF16)
    qw = NSA_HG * HEAD_DIM
    return pl.pallas_call(
        functools.partial(_nsa_select_kernel, nsub=nsub),
        grid=(NSA_KV, S // qs),
        in_specs=[
            pl.BlockSpec((qs, qw), lambda g, n: (n, OFF_Q // qw + g)),
            pl.BlockSpec((qs, LANES), lambda g, n: (n, 0)),
            pl.BlockSpec((qs, LANES), lambda g, n: (n, 0)),
            pl.BlockSpec((qs, LANES), lambda g, n: (n, OFF_G // LANES + g)),
            pl.BlockSpec((1, 1, ncp, HEAD_DIM), lambda g, n: (0, g, 0, 0)),
            pl.BlockSpec((1, 1, ncp, HEAD_DIM), lambda g, n: (1, g, 0, 0)),
            pl.BlockSpec((ncp, LANES), lambda g, n: (0, 0)),
            pl.BlockSpec((Q_BLOCK, LANES), lambda g, n: (0, 0)),
        ],
        out_specs=[
            pl.BlockSpec((qs, LANES), lambda g, n: (n, g)),
            pl.BlockSpec((qs, qw), lambda g, n: (n, g)),
            pl.BlockSpec((qs, qw), lambda g, n: (n, g)),
        ],
        out_shape=[
            jax.ShapeDtypeStruct((S, NSA_KV * LANES), BF16),
            jax.ShapeDtypeStruct((S, NSA_WIDTH), F32),
            jax.ShapeDtypeStruct((S, NSA_WIDTH), BF16),
        ],
        compiler_params=_cparams(("arbitrary", "arbitrary")),
        name="nsa_select",
    )(proj, cosf, sinf, proj, kcvc, kcvc, ov, ca)


def _nsa_kernel(q_ref, gate_ref, z_ref, selb_ref, pc_ref, wb_ref, cb_ref,
                ks_ref, vs_ref, kw_ref, vw_ref, o_ref, m_ref, accl_ref, sa_ref, sb_ref, part_ref, *, tk):
    n = pl.program_id(1)
    q0 = n * Q_BLOCK
    R = NSA_HG * Q_BLOCK
    q4 = jnp.concatenate([q_ref[:, h * HEAD_DIM:(h + 1) * HEAD_DIM] for h in range(NSA_HG)], axis=0)
    per_head = _per_head

    wspan = WINDOW + Q_BLOCK
    w0 = pl.multiple_of(jnp.maximum(q0 - WINDOW, 0), Q_BLOCK)
    sw = (_halves(_dot_nt, q4, kw_ref[pl.ds(w0, wspan), :])
          + per_head(wb_ref[jnp.minimum(n, WINDOW // Q_BLOCK)]))
    ew = jnp.exp2(sw - jnp.max(sw, axis=-1, keepdims=True))
    vwa = jnp.concatenate([vw_ref[pl.ds(w0, wspan), :], jnp.ones((wspan, LANES), BF16)], axis=1)
    pvw = _halves(_dot, ew.astype(BF16), vwa)
    o_win = pvw[:, :LANES] / pvw[:, LANES:]

    gate = jax.nn.sigmoid(gate_ref[...])
    for h in range(NSA_HG):
        cs = slice(h * HEAD_DIM, (h + 1) * HEAD_DIM)
        part_ref[:, cs] = pc_ref[:, cs] + gate[:, 3 * h + 2:3 * h + 3] * o_win[h * Q_BLOCK:(h + 1) * Q_BLOCK]

    sel_bias = per_head(selb_ref[...])

    m_ref[...] = jnp.full(m_ref.shape, NEG_BIG, F32)
    accl_ref[...] = jnp.zeros(accl_ref.shape, F32)
    qa = jnp.concatenate([q4, sel_bias], axis=1)
    key_blk = lax.broadcasted_iota(jnp.int32, (tk, LANES), 0) >> 6
    blk_lane = lax.broadcasted_iota(jnp.int32, (tk, LANES), 1)
    ones_k = jnp.ones((tk, LANES), BF16)
    rep = tk // LANES

    n_q = tk // Q_BLOCK
    last = lax.shift_right_logical(n, n_q.bit_length() - 1)
    diag = n - last * n_q

    def score_tile(kt, dst_ref):
        k0 = pl.multiple_of(kt * tk, tk)
        onehot = jnp.where(key_blk + kt * (tk // SEL_LEN) == blk_lane, 1.0, 0.0).astype(BF16)
        dst_ref[...] = _dot_nt(qa, jnp.concatenate([ks_ref[pl.ds(k0, tk), :], onehot], axis=1))

    def consume_tile(kt, src_ref, causal=False):
        k0 = pl.multiple_of(kt * tk, tk)
        sc = src_ref[...]
        if causal:
            sc = sc + per_head(cb_ref[diag])
        m_prev = m_ref[...]
        m_next = jnp.maximum(m_prev, jnp.max(sc, axis=-1, keepdims=True))
        pr = jnp.exp2(sc - jnp.concatenate([m_next] * rep, axis=1))
        alpha = jnp.exp2(m_prev - m_next)
        va = jnp.concatenate([vs_ref[pl.ds(k0, tk), :], ones_k], axis=1)
        accl_ref[...] = jnp.concatenate([alpha, alpha], axis=1) * accl_ref[...] + _dot(pr.astype(BF16), va)
        m_ref[...] = m_next

    score_tile(0, sa_ref)

    def slc_pair(i, carry):
        score_tile(2 * i + 1, sb_ref)
        consume_tile(2 * i, sa_ref)
        score_tile(2 * i + 2, sa_ref)
        consume_tile(2 * i + 1, sb_ref)
        return carry

    lax.fori_loop(0, lax.shift_right_logical(last, 1), slc_pair, 0)

    @pl.when((last & 1) == 0)
    def _():
        consume_tile(last, sa_ref, causal=True)

    @pl.when((last & 1) == 1)
    def _():
        score_tile(last, sb_ref)
        consume_tile(last - 1, sa_ref)
        consume_tile(last, sb_ref, causal=True)

    o_slc = accl_ref[:, :LANES] / accl_ref[:, LANES:]

    gsig = jax.nn.sigmoid(gate_ref[...])
    for h in range(NSA_HG):
        rs = slice(h * Q_BLOCK, (h + 1) * Q_BLOCK)
        cs = slice(h * HEAD_DIM, (h + 1) * HEAD_DIM)
        oh = part_ref[:, cs] + gsig[:, 3 * h + 1:3 * h + 2] * o_slc[rs]
        z = z_ref[:, cs]
        o_ref[:, cs] = (oh * (z * jax.nn.sigmoid(z))).astype(o_ref.dtype)


def _cmp_staircase():
    nb = Q_BLOCK // CMP_STRIDE
    a = np.zeros((Q_BLOCK, LANES), np.float32)
    a[:, 0] = 1.0
    r = np.arange(Q_BLOCK)[:, None]
    i = np.arange(nb)[None, :]
    a[:, 1:1 + nb] = np.where(r < CMP_STRIDE * i + (CMP_STRIDE - 1), NEG_BIG, 0.0)
    return a


def _window_masks():
    n_w = WINDOW // Q_BLOCK
    r = np.arange(Q_BLOCK)[None, :, None]
    c = np.arange(WINDOW + Q_BLOCK)[None, None, :]
    d = np.arange(n_w + 1)[:, None, None]
    ok = np.where(d < n_w, c <= Q_BLOCK * d + r, (c > r) & (c <= WINDOW + r))
    return np.where(ok, 0.0, NEG_BIG).astype(np.float32)


def _causal_staircases(tk):
    n_q = tk // Q_BLOCK
    r = np.arange(Q_BLOCK)[None, :, None]
    c = np.arange(tk)[None, None, :]
    d = np.arange(n_q + 1)[:, None, None]
    return np.where((c <= Q_BLOCK * d + r) | (d == n_q), 0.0, NEG_BIG).astype(np.float32)


def _overlap_matrix(ncp, nsel_pad, nsel):
    ci = np.arange(ncp)[:, None] * CMP_STRIDE
    sj = np.arange(nsel_pad)[None, :] * SEL_LEN
    ov = (ci < sj + SEL_LEN) & (ci + CMP_LEN > sj) & (np.arange(nsel_pad)[None, :] < nsel) & (np.arange(ncp)[:, None] < ncp - 1)
    return ov.astype(np.float32)


def _nsa(proj, qb, selb, pcmp, kvb, tk=512):
    S = proj.shape[0]
    assert S % tk == 0 and S >= WINDOW + Q_BLOCK and WINDOW % Q_BLOCK == 0
    wb = jnp.asarray(_window_masks(), F32)
    n_q = tk // Q_BLOCK
    assert n_q & (n_q - 1) == 0 and tk % SEL_LEN == 0
    cb = jnp.asarray(_causal_staircases(tk), F32)
    qw = NSA_HG * HEAD_DIM
    R = NSA_HG * Q_BLOCK
    kern = functools.partial(_nsa_kernel, tk=tk)
    return pl.pallas_call(
        kern,
        grid=(NSA_KV, S // Q_BLOCK),
        in_specs=[
            pl.BlockSpec((Q_BLOCK, qw), lambda g, n: (n, g)),
            pl.BlockSpec((Q_BLOCK, LANES), lambda g, n: (n, OFF_G // LANES + g)),
            pl.BlockSpec((Q_BLOCK, qw), lambda g, n: (n, OFF_Z // qw + g)),
            pl.BlockSpec((Q_BLOCK, LANES), lambda g, n: (n, g)),
            pl.BlockSpec((Q_BLOCK, qw), lambda g, n: (n, g)),
            pl.BlockSpec((WINDOW // Q_BLOCK + 1, Q_BLOCK, WINDOW + Q_BLOCK), lambda g, n: (0, 0, 0)),
            pl.BlockSpec((n_q + 1, Q_BLOCK, tk), lambda g, n: (0, 0, 0)),
            pl.BlockSpec((S, HEAD_DIM), lambda g, n: (0, 0 * NSA_KV + g)),
            pl.BlockSpec((S, HEAD_DIM), lambda g, n: (0, 1 * NSA_KV + g)),
            pl.BlockSpec((S, HEAD_DIM), lambda g, n: (0, 2 * NSA_KV + g)),
            pl.BlockSpec((S, HEAD_DIM), lambda g, n: (0, 3 * NSA_KV + g)),
        ],
        out_specs=pl.BlockSpec((Q_BLOCK, qw), lambda g, n: (n, g)),
        out_shape=jax.ShapeDtypeStruct((S, NSA_WIDTH), BF16),
        scratch_shapes=[
            pltpu.VMEM((R, LANES), F32),
            pltpu.VMEM((R, HEAD_DIM + LANES), F32),
            pltpu.VMEM((R, tk), F32),
            pltpu.VMEM((R, tk), F32),
            pltpu.VMEM((Q_BLOCK, qw), F32),
        ],
        compiler_params=_cparams(("arbitrary", "arbitrary")),
        name="nsa",
    )(qb, proj, proj, selb, pcmp, wb, cb, kvb, kvb, kvb, kvb)


def _merge_kernel(ohg_ref, onsa_ref, g0_ref, g1_ref, g2_ref, g3_ref, whg_ref, wnsa_ref, o_ref):
    y_hg = _dot(ohg_ref[...], whg_ref[...])
    y_nsa = _dot(onsa_ref[...], wnsa_ref[...])
    half = D_MODEL // 2
    gh = (g0_ref, g1_ref)
    gn = (g2_ref, g3_ref)
    for c in range(2):
        cs = slice(c * half, (c + 1) * half)
        m = jax.nn.sigmoid(gh[c][...]) * y_hg[:, cs] + jax.nn.sigmoid(gn[c][...]) * y_nsa[:, cs]
        o_ref[:, cs] = m.astype(o_ref.dtype)


def _merge(o_hg, o_nsa, proj, w_hg, w_nsa, tm=256):
    S = o_hg.shape[0]
    tm = min(tm, S)
    half = D_MODEL // 2
    gb = OFF_MG // half
    return pl.pallas_call(
        _merge_kernel,
        grid=(S // tm,),
        in_specs=[
            pl.BlockSpec((tm, HG_WIDTH), lambda i: (i, 0)),
            pl.BlockSpec((tm, NSA_WIDTH), lambda i: (i, 0)),
            pl.BlockSpec((tm, half), lambda i: (i, gb + 0)),
            pl.BlockSpec((tm, half), lambda i: (i, gb + 1)),
            pl.BlockSpec((tm, half), lambda i: (i, gb + 2)),
            pl.BlockSpec((tm, half), lambda i: (i, gb + 3)),
            pl.BlockSpec((HG_WIDTH, D_MODEL), lambda i: (0, 0)),
            pl.BlockSpec((NSA_WIDTH, D_MODEL), lambda i: (0, 0)),
        ],
        out_specs=pl.BlockSpec((tm, D_MODEL), lambda i: (i, 0)),
        out_shape=jax.ShapeDtypeStruct((S, D_MODEL), BF16),
        compiler_params=_cparams(("arbitrary",)),
        name="merge",
    )(o_hg, o_nsa, proj, proj, proj, proj, w_hg, w_nsa)


def _out_kernel(x_ref, m_ref, w_ref, nw_ref, o_ref):
    h = x_ref[...] + _dot(m_ref[...], w_ref[...])
    ms = jnp.mean(h * h, axis=-1, keepdims=True)
    o_ref[...] = h * lax.rsqrt(ms + EPS) * nw_ref[...]


def _out(x2, merged, w_out, final_w, tm=256):
    S, D = x2.shape
    tm = min(tm, S)
    return pl.pallas_call(
        _out_kernel,
        grid=(S // tm,),
        in_specs=[
            pl.BlockSpec((tm, D), lambda i: (i, 0)),
            pl.BlockSpec((tm, D), lambda i: (i, 0)),
            pl.BlockSpec((D, D), lambda i: (0, 0)),
            pl.BlockSpec((1, D), lambda i: (0, 0)),
        ],
        out_specs=pl.BlockSpec((tm, D), lambda i: (i, 0)),
        out_shape=jax.ShapeDtypeStruct((S, D), F32),
        compiler_params=_cparams(("arbitrary",)),
        name="out_proj",
    )(x2, merged, w_out, final_w.reshape(1, D))


def _layer(x2, norm_w, w_in, lb_logits, hg_norm_w, cmp_k_pos, cmp_k_w1, cmp_k_b1, cmp_k_w2,
           cmp_v_pos, cmp_v_w1, cmp_v_b1, cmp_v_w2, w_branch_hg, w_branch_nsa, w_out, final_w):
    S = x2.shape[0]
    proj = _norm_proj(x2, norm_w, _w_prep(w_in.T))
    cosf, sinf = _rope_tables(S)
    segs, kvb = _kv_prep(proj, cosf, sinf)
    pe = jnp.stack([cmp_k_pos.reshape(1, -1), cmp_v_pos.reshape(1, -1)])
    w1 = jnp.stack([cmp_k_w1, cmp_v_w1]).astype(BF16)
    b1 = jnp.stack([cmp_k_b1.reshape(1, -1), cmp_v_b1.reshape(1, -1)])
    w2 = jnp.stack([cmp_k_w2, cmp_v_w2]).astype(BF16)
    kcvc = _compress(segs, pe, w1, b1, w2)
    o_hg = _hgrn(proj, lb_logits, hg_norm_w)
    selb, pcmp, qb = _nsa_select(proj, cosf, sinf, kcvc)
    o_nsa = _nsa(proj, qb, selb, pcmp, kvb)
    merged = _merge(o_hg, o_nsa, proj, w_branch_hg.astype(BF16), w_branch_nsa.astype(BF16))
    return _out(x2, merged, w_out.astype(BF16), final_w)


def kernel(x, norm_w, w_in, hg_lb_logits, hg_norm_w, cmp_k_pos, cmp_k_w1, cmp_k_b1, cmp_k_w2, cmp_v_pos, cmp_v_w1, cmp_v_b1, cmp_v_w2, w_branch_hg, w_branch_nsa, w_out, final_norm_w):
    B, S, D = x.shape
    assert B == 1 and D == D_MODEL and norm_w.shape[0] == 1
    y = _layer(x[0], norm_w[0], w_in[0], hg_lb_logits, hg_norm_w[0], cmp_k_pos[0], cmp_k_w1[0], cmp_k_b1[0],
               cmp_k_w2[0], cmp_v_pos[0], cmp_v_w1[0], cmp_v_b1[0], cmp_v_w2[0], w_branch_hg[0],
               w_branch_nsa[0], w_out[0], final_norm_w)
    return y[None]
```

```python
import functools

import numpy as np
import jax
import jax.numpy as jnp
from jax import lax
from jax.experimental import pallas as pl
from jax.experimental.pallas import tpu as pltpu

F32 = jnp.float32
BF16 = jnp.bfloat16

D_MODEL = 2048
HG_HEADS = 8
HG_DK = 128
HG_DV = 128
HG_FDIM = HG_HEADS * HG_DK
HG_WIDTH = HG_HEADS * HG_DV
HG_MM_LEVELS = 3
NSA_HEADS = 16
NSA_KV = 4
NSA_HG = NSA_HEADS // NSA_KV
HEAD_DIM = 128
NSA_WIDTH = NSA_HEADS * HEAD_DIM
NSA_KVW = NSA_KV * HEAD_DIM
CMP_LEN = 32
CMP_STRIDE = 16
CMP_HIDDEN = 512
SEL_LEN = 64
SEL_TOPK = 16
WINDOW = 512
Q_BLOCK = 256
ROPE_THETA = 500000.0
ROT_DIM = HEAD_DIM // 4
ROT_HALF = ROT_DIM // 2
EPS = 1e-6
LOG2E = float(np.log2(np.e))

LANES = 128
NEG_BIG = -1e30
VMEM_LIMIT = 56 * 1024 * 1024

OFF_HG = 0
OFF_Q = 4 * HG_FDIM
OFF_KV = OFF_Q + NSA_WIDTH
OFF_MG = OFF_KV + 6 * NSA_KVW
OFF_Z = OFF_MG + 2 * D_MODEL
OFF_G = OFF_Z + NSA_WIDTH
PROJ_COLS = OFF_G + NSA_KV * LANES
IN_GATE_OFF = OFF_MG
IN_Z_OFF = IN_GATE_OFF + 3 * NSA_HEADS
IN_MG_OFF = IN_Z_OFF + NSA_WIDTH


def _dot(a, b):
    return jnp.dot(a, b, preferred_element_type=F32)


def _dot_nt(a, b):
    return lax.dot_general(a, b, (((1,), (1,)), ((), ())), preferred_element_type=F32)


def _halves(dot, a, b):
    h = a.shape[0] // 2
    return jnp.concatenate([dot(a[:h], b), dot(a[h:], b)], axis=0)


def _split_bf16(a):
    hi = a.astype(BF16)
    lo = (a - hi.astype(F32)).astype(BF16)
    return hi, lo


def _cparams(sem):
    return pltpu.CompilerParams(dimension_semantics=sem, vmem_limit_bytes=VMEM_LIMIT)


def _w_prep_kernel(wt_ref, wg_ref, o_ref, *, n_direct):
    j = pl.program_id(0)

    @pl.when(j < n_direct)
    def _():
        o_ref[...] = wt_ref[...].astype(BF16)

    @pl.when(j >= n_direct)
    def _():
        o_ref[...] = wg_ref[...].astype(BF16)


def _w_prep(wt, tn=512):
    D = wt.shape[1]
    n_main, n_mg, n_z = OFF_MG // tn, 2 * D_MODEL // tn, NSA_WIDTH // tn
    n_direct = n_main + n_mg + n_z
    assert OFF_MG % tn == 0 and D_MODEL % tn == 0 and NSA_KV * LANES == tn
    gates = wt[IN_GATE_OFF:IN_Z_OFF].reshape(NSA_KV, 3 * NSA_HG, D)
    gates = jnp.pad(gates, ((0, 0), (0, LANES - 3 * NSA_HG), (0, 0))).reshape(NSA_KV * LANES, D)
    n_extra = 2
    gates = jnp.pad(gates, ((0, (n_extra - 1) * tn), (0, 0)))

    sub = 8
    assert IN_MG_OFF % sub == 0 and IN_Z_OFF % sub == 0 and tn % sub == 0

    def src_row(j):
        in_mg = IN_MG_OFF // sub + (j - n_main) * (tn // sub)
        in_z = IN_Z_OFF // sub + (jnp.minimum(j, n_direct - 1) - n_main - n_mg) * (tn // sub)
        return jnp.where(j < n_main, j * (tn // sub), jnp.where(j < n_main + n_mg, in_mg, in_z)) * sub

    return pl.pallas_call(
        functools.partial(_w_prep_kernel, n_direct=n_direct),
        grid=(n_direct + n_extra,),
        in_specs=[
            pl.BlockSpec((pl.Element(tn), pl.Element(D)), lambda j: (src_row(j), 0)),
            pl.BlockSpec((tn, D), lambda j: (jnp.maximum(j - n_direct, 0), 0)),
        ],
        out_specs=pl.BlockSpec((tn, D), lambda j: (j, 0)),
        out_shape=jax.ShapeDtypeStruct(((n_direct + n_extra) * tn, D), BF16),
        compiler_params=_cparams(("arbitrary",)),
        name="w_prep",
    )(wt, gates)


def _norm_proj_kernel(x_ref, nw_ref, w_ref, o_ref, xn_ref):
    @pl.when(pl.program_id(1) == 0)
    def _():
        x = x_ref[...]
        ms = jnp.mean(x * x, axis=-1, keepdims=True)
        xn_ref[...] = (x * lax.rsqrt(ms + EPS) * nw_ref[...]).astype(BF16)

    o_ref[...] = _dot_nt(xn_ref[...], w_ref[...])


def _norm_proj(x2, norm_w, wtb, tm=1024, tn=1024):
    S, D = x2.shape
    N = wtb.shape[0]
    tm = min(tm, S)
    return pl.pallas_call(
        _norm_proj_kernel,
        grid=(S // tm, N // tn),
        in_specs=[
            pl.BlockSpec((tm, D), lambda i, j: (i, 0)),
            pl.BlockSpec((1, D), lambda i, j: (0, 0)),
            pl.BlockSpec((tn, D), lambda i, j: (j, 0)),
        ],
        out_specs=pl.BlockSpec((tm, tn), lambda i, j: (i, j)),
        out_shape=jax.ShapeDtypeStruct((S, N), F32),
        scratch_shapes=[pltpu.VMEM((tm, D), BF16)],
        compiler_params=_cparams(("arbitrary", "arbitrary")),
        name="norm_proj",
    )(x2, norm_w.reshape(1, D), wtb)


def _rope(x, cosf, sinf):
    lane = lax.broadcasted_iota(jnp.int32, x.shape, 1)
    rot = jnp.where(lane < ROT_HALF, pltpu.roll(x, LANES - ROT_HALF, 1), pltpu.roll(x, ROT_HALF, 1))
    return x * cosf + rot * sinf


def _rope_tables(S):
    pos = np.arange(S, dtype=np.float64)
    inv = ROPE_THETA ** (-np.arange(0, ROT_DIM, 2, dtype=np.float64) / ROT_DIM)
    ang = pos[:, None] * inv[None, :]
    cos, sin = np.cos(ang), np.sin(ang)
    rest = LANES - ROT_DIM
    cosf = np.concatenate([cos, cos, np.ones((S, rest))], axis=1)
    sinf = np.concatenate([-sin, sin, np.zeros((S, rest))], axis=1)
    return jnp.asarray(cosf, F32), jnp.asarray(sinf, F32)


def _kv_prep_kernel(kv_ref, cos_ref, sin_ref, cmp_ref, kvb_ref, row_ref):
    cosf = cos_ref[...]
    sinf = sin_ref[...]
    W = NSA_KVW
    n_seg = row_ref.shape[0] // CMP_STRIDE

    def to_segments(a, g, rows):
        row_ref[...] = rows
        for l in range(CMP_STRIDE):
            cmp_ref[a, g, :, l * HEAD_DIM:(l + 1) * HEAD_DIM] = row_ref[pl.ds(l, n_seg, stride=CMP_STRIDE), :]

    for g in range(NSA_KV):
        kc = kv_ref[:, 0 * W + g * LANES:0 * W + (g + 1) * LANES]
        to_segments(0, g, _rope(kc, cosf, sinf))
        to_segments(1, g, kv_ref[:, 1 * W + g * LANES:1 * W + (g + 1) * LANES])
        ks = kv_ref[:, 2 * W + g * LANES:2 * W + (g + 1) * LANES]
        kvb_ref[:, 0 * W + g * LANES:0 * W + (g + 1) * LANES] = _rope(ks, cosf, sinf).astype(BF16)
        kvb_ref[:, 1 * W + g * LANES:1 * W + (g + 1) * LANES] = kv_ref[:, 3 * W + g * LANES:3 * W + (g + 1) * LANES].astype(BF16)
        kw = kv_ref[:, 4 * W + g * LANES:4 * W + (g + 1) * LANES]
        kvb_ref[:, 2 * W + g * LANES:2 * W + (g + 1) * LANES] = _rope(kw, cosf, sinf).astype(BF16)
        kvb_ref[:, 3 * W + g * LANES:3 * W + (g + 1) * LANES] = kv_ref[:, 5 * W + g * LANES:5 * W + (g + 1) * LANES].astype(BF16)


def _kv_prep(proj, cosf, sinf, tm=512):
    S = proj.shape[0]
    tm = min(tm, S)
    kvw = 6 * NSA_KVW
    return pl.pallas_call(
        _kv_prep_kernel,
        grid=(S // tm,),
        in_specs=[
            pl.BlockSpec((tm, kvw), lambda i: (i, OFF_KV // kvw)),
            pl.BlockSpec((tm, LANES), lambda i: (i, 0)),
            pl.BlockSpec((tm, LANES), lambda i: (i, 0)),
        ],
        out_specs=[
            pl.BlockSpec((2, NSA_KV, tm // CMP_STRIDE, CMP_STRIDE * HEAD_DIM), lambda i: (0, 0, i, 0)),
            pl.BlockSpec((tm, 4 * NSA_KVW), lambda i: (i, 0)),
        ],
        out_shape=[
            jax.ShapeDtypeStruct((2, NSA_KV, S // CMP_STRIDE, CMP_STRIDE * HEAD_DIM), F32),
            jax.ShapeDtypeStruct((S, 4 * NSA_KVW), BF16),
        ],
        scratch_shapes=[pltpu.VMEM((tm, HEAD_DIM), F32)],
        compiler_params=_cparams(("arbitrary",)),
        name="kv_prep",
    )(proj, cosf, sinf)


def _compress_kernel(seg_ref, pe_ref, w1_ref, b1_ref, w2_ref, o_ref):
    half = (CMP_LEN // 2) * HEAD_DIM
    seg = seg_ref[0, 0]
    n_seg = seg.shape[0]
    pe = pe_ref[0]
    a = (seg + pe[:, :half]).astype(BF16)
    b = (seg + pe[:, half:]).astype(BF16)
    u = _dot(a, w1_ref[0, :half, :])
    v = _dot(b, w1_ref[0, half:, :])
    v_next = pltpu.roll(v, n_seg - 1, 0)
    pre = u + v_next + b1_ref[0]
    h = 0.5 * pre * (1.0 + jnp.tanh(np.sqrt(2.0 / np.pi).astype(np.float32) * (pre + 0.044715 * (pre * pre * pre))))
    o_ref[0, 0] = _dot(h.astype(BF16), w2_ref[0])


def _compress(segs, pe, w1, b1, w2):
    _, G, n_seg, segw = segs.shape
    return pl.pallas_call(
        _compress_kernel,
        grid=(2, G),
        in_specs=[
            pl.BlockSpec((1, 1, n_seg, segw), lambda a, g: (a, g, 0, 0)),
            pl.BlockSpec((1, 1, 2 * segw), lambda a, g: (a, 0, 0)),
            pl.BlockSpec((1, 2 * segw, CMP_HIDDEN), lambda a, g: (a, 0, 0)),
            pl.BlockSpec((1, 1, CMP_HIDDEN), lambda a, g: (a, 0, 0)),
            pl.BlockSpec((1, CMP_HIDDEN, HEAD_DIM), lambda a, g: (a, 0, 0)),
        ],
        out_specs=pl.BlockSpec((1, 1, n_seg, HEAD_DIM), lambda a, g: (a, g, 0, 0)),
        out_shape=jax.ShapeDtypeStruct((2, G, n_seg, HEAD_DIM), F32),
        compiler_params=_cparams(("arbitrary", "arbitrary")),
        name="compress",
    )(segs, pe, w1, b1, w2)


def _hgrn_consts(ch):
    nl = int(np.log2(ch))
    assert 1 << nl == ch
    t = np.arange(ch)[:, None]
    r = np.arange(ch)[None, :]
    mats = [r <= t]
    masks = [np.eye(ch, dtype=bool)]
    for l in range(nl):
        half = 1 << l
        blk = t // (2 * half)
        ref = blk * 2 * half + half - 1
        up = ((t >> l) & 1) == 1
        mats.append(np.where(up, (r > ref) & (r <= t), (r > t) & (r <= ref)))
        masks.append(up & (((r >> l) & 1) == 0) & (blk == r // (2 * half)))
    mc = np.concatenate(mats[:1 + HG_MM_LEVELS], axis=0).astype(np.float32)
    lm = np.stack(masks, axis=0).astype(np.float32)
    return nl, mc, lm


def _hgrn_kernel(q_ref, f_ref, i_ref, z_ref, lbl_ref, nw_ref, mc_ref, lm_ref, o_ref, st_ref, e_ref, *, ch, nl):
    @pl.when(pl.program_id(0) == 0)
    def _():
        st_ref[...] = jnp.zeros_like(st_ref)

    lg = lbl_ref[...]
    ex = jnp.exp(lg - jnp.max(lg, axis=0, keepdims=True))
    lb = ex[0:1] / jnp.sum(ex, axis=0, keepdims=True)
    f = lb + (1.0 - lb) * jax.nn.sigmoid(f_ref[...])
    g_hi, g_lo = _split_bf16(jnp.log(f))
    mc = mc_ref[...]
    e_ref[...] = _dot(mc, g_hi) + _dot(mc, g_lo)
    row = lax.broadcasted_iota(jnp.int32, (ch, 1), 0)
    nw = nw_ref[...]

    heads = [slice(h * HG_DK, (h + 1) * HG_DK) for h in range(HG_HEADS)]
    qs = [q_ref[:, sl] for sl in heads]
    ks = [1.0 - f[:, sl] for sl in heads]
    scs = [_dot_nt(q.astype(BF16), k.astype(BF16)) * lm_ref[0] for q, k in zip(qs, ks)]
    bs = [e_ref[0:ch, sl] for sl in heads]
    for l in range(nl):
        up = ((row >> l) & 1) == 1
        half = 1 << l
        for h, sl in enumerate(heads):
            if l < HG_MM_LEVELS:
                el = e_ref[(l + 1) * ch:(l + 2) * ch, sl]
                xl = jnp.where(up, qs[h], ks[h]) * jnp.exp(el)
            else:
                parts = []
                for r0 in range(0, ch, 2 * half):
                    mid = r0 + half
                    edge = bs[h][mid - 1:mid, :]
                    parts.append(ks[h][r0:mid] * jnp.exp(edge - bs[h][r0:mid]))
                    parts.append(qs[h][mid:mid + half] * jnp.exp(bs[h][mid:mid + half] - edge))
                xl = jnp.concatenate(parts, axis=0)
            xl = xl.astype(BF16)
            scs[h] = scs[h] + _dot_nt(xl, xl) * lm_ref[l + 1]

    vbs = [i_ref[:, sl].astype(BF16) for sl in heads]
    sts = [st_ref[h] for h in range(HG_HEADS)]
    inters = [_dot_nt((qs[h] * jnp.exp(bs[h])).astype(BF16), sts[h].astype(BF16)) for h in range(HG_HEADS)]
    for h, sl in enumerate(heads):
        ke = (ks[h] * jnp.exp(bs[h][ch - 1:ch, :] - bs[h])).astype(BF16)
        upd = lax.dot_general(vbs[h], ke, (((0,), (0,)), ((), ())), preferred_element_type=F32)
        st_ref[h] = sts[h] * jnp.exp(bs[h][ch - 1:ch, :]) + upd
    outs = [inters[h] + _dot(scs[h].astype(BF16), vbs[h]) for h in range(HG_HEADS)]
    for h, sl in enumerate(heads):
        o = outs[h]
        ms = jnp.mean(o * o, axis=-1, keepdims=True)
        z = z_ref[:, sl]
        o_ref[:, sl] = (o * lax.rsqrt(ms + EPS) * nw * (z * jax.nn.sigmoid(z))).astype(o_ref.dtype)


def _hgrn(proj, lb_logits, hg_norm_w, ch=128):
    S = proj.shape[0]
    nl, mc, lm = _hgrn_consts(ch)
    kern = functools.partial(_hgrn_kernel, ch=ch, nl=nl)
    nlb = lb_logits.shape[0]
    return pl.pallas_call(
        kern,
        grid=(S // ch,),
        in_specs=[
            pl.BlockSpec((ch, HG_FDIM), lambda c: (c, 0)),
            pl.BlockSpec((ch, HG_FDIM), lambda c: (c, 1)),
            pl.BlockSpec((ch, HG_WIDTH), lambda c: (c, 2)),
            pl.BlockSpec((ch, HG_WIDTH), lambda c: (c, 3)),
            pl.BlockSpec((nlb, HG_FDIM), lambda c: (0, 0)),
            pl.BlockSpec((1, HG_DV), lambda c: (0, 0)),
            pl.BlockSpec(mc.shape, lambda c: (0, 0)),
            pl.BlockSpec(lm.shape, lambda c: (0, 0, 0)),
        ],
        out_specs=pl.BlockSpec((ch, HG_WIDTH), lambda c: (c, 0)),
        out_shape=jax.ShapeDtypeStruct((S, HG_WIDTH), BF16),
        scratch_shapes=[
            pltpu.VMEM((HG_HEADS, HG_DV, HG_DK), F32),
            pltpu.VMEM((mc.shape[0], HG_FDIM), F32),
        ],
        compiler_params=_cparams(("arbitrary",)),
        name="hgrn",
    )(proj, proj, proj, proj, lb_logits, hg_norm_w.reshape(1, HG_DV), jnp.asarray(mc, BF16), jnp.asarray(lm, F32))


def _stack_heads(q_ref, rows, cosf, sinf):
    qscale = (HEAD_DIM ** -0.5) * LOG2E
    return jnp.concatenate(
        [(_rope(q_ref[rows, h * HEAD_DIM:(h + 1) * HEAD_DIM], cosf, sinf) * qscale).astype(BF16)
         for h in range(NSA_HG)], axis=0)


def _per_head(a):
    return jnp.concatenate([a] * NSA_HG, axis=0)


def _nsa_select_kernel(q_ref, cos_ref, sin_ref, gate_ref, kc_ref, vc_ref, ov_ref, ca_ref,
                       selb_ref, pc_ref, qb_ref, *, nsub):
    n = pl.program_id(1)
    R = NSA_HG * Q_BLOCK
    ncp = kc_ref.shape[2]
    nsel_pad = ov_ref.shape[1]
    nb = Q_BLOCK // CMP_STRIDE
    nsb = Q_BLOCK // SEL_LEN
    subs = range(nsub)
    rows = [slice(i * Q_BLOCK, (i + 1) * Q_BLOCK) for i in subs]
    q0s = [(n * nsub + i) * Q_BLOCK for i in subs]

    def body(ncols):
        nrows = ncols // nb * nsb
        crow = lax.broadcasted_iota(jnp.int32, (ncols, LANES), 0)
        clane = lax.broadcasted_iota(jnp.int32, (ncols, LANES), 1)
        kc = kc_ref[0, 0, :ncols, :].astype(BF16)
        vca = jnp.concatenate([vc_ref[0, 0, :ncols, :].astype(BF16), jnp.ones((ncols, LANES), BF16)], axis=1)
        ca4 = _per_head(ca_ref[...])
        ov = ov_ref[:ncols, :]

        ss = []
        for i in subs:
            q4 = _stack_heads(q_ref, rows[i], cos_ref[rows[i], :], sin_ref[rows[i], :])
            for h in range(NSA_HG):
                qb_ref[rows[i], h * HEAD_DIM:(h + 1) * HEAD_DIM] = q4[h * Q_BLOCK:(h + 1) * Q_BLOCK]
            c_first = (n * nsub + i) * nb - (CMP_LEN // CMP_STRIDE - 1)
            flags = jnp.where(clane == 0, jnp.where(crow >= c_first + nb, NEG_BIG, 0.0),
                              jnp.where(crow == c_first + clane - 1, 1.0, 0.0)).astype(BF16)
            ss.append(_halves(_dot_nt, jnp.concatenate([q4, ca4], axis=1), jnp.concatenate([kc, flags], axis=1)))
        ebs = [jnp.exp2(s - jnp.max(s, axis=-1, keepdims=True)).astype(BF16) for s in ss]
        pvs = [_halves(_dot, e, vca) for e in ebs]

        imps = []
        for i in subs:
            t_row = q0s[i] + (lax.broadcasted_iota(jnp.int32, (R, LANES), 0) & (Q_BLOCK - 1))
            inv = jnp.where(t_row >= CMP_LEN - 1, 1.0, 0.0) / pvs[i][:, LANES:]
            o_cmp = pvs[i][:, :LANES] * inv
            gate = jax.nn.sigmoid(gate_ref[rows[i], :])
            for h in range(NSA_HG):
                pc_ref[rows[i], h * HEAD_DIM:(h + 1) * HEAD_DIM] = (
                    gate[:, 3 * h:3 * h + 1] * o_cmp[h * Q_BLOCK:(h + 1) * Q_BLOCK])
            eo = _dot(ebs[i], ov) * inv
            imp = eo[0:Q_BLOCK]
            for h in range(1, NSA_HG):
                imp = imp + eo[h * Q_BLOCK:(h + 1) * Q_BLOCK]
            imps.append(imp)

        jj = lax.broadcasted_iota(jnp.int32, (nrows, Q_BLOCK), 0)
        jjf = jj.astype(F32)
        lane_q = lax.broadcasted_iota(jnp.int32, (nrows, Q_BLOCK), 1)
        TAKEN = -2.0
        scores = []
        for i in subs:
            tq = q0s[i] + lane_q
            jt = tq >> 6
            forced = (jj == 0) | (jj == jt) | (jj == jt - 1)
            scores.append(jnp.where(jj * SEL_LEN <= tq, jnp.where(forced, TAKEN, imps[i].T[:nrows]), -1.0))
        for _ in range(SEL_TOPK - 3):
            for i in subs:
                mx = jnp.max(scores[i], axis=0, keepdims=True)
                first = jnp.min(jnp.where(scores[i] == mx, jjf, float(nsel_pad)), axis=0, keepdims=True)
                first = jnp.where(mx >= 0.0, first, -1.0)
                scores[i] = jnp.where(jjf == first, TAKEN, scores[i])
        for i in subs:
            bias = jnp.where(scores[i] == TAKEN, 0.0, NEG_BIG)
            if nrows < nsel_pad:
                bias = jnp.concatenate([bias, jnp.full((nsel_pad - nrows, Q_BLOCK), NEG_BIG, F32)], axis=0)
            selb_ref[rows[i], :] = bias.T.astype(BF16)

    needed = (n + 1) * nsub * nb
    widths = list(range(LANES, ncp + 1, LANES))
    for w in widths:
        lo = w - LANES
        cond = needed > lo if w == widths[-1] else (needed > lo) & (needed <= w)
        pl.when(cond)(functools.partial(body, w))


def _nsa_select(proj, cosf, sinf, kcvc, nsub=4):
    S = proj.shape[0]
    ncp = kcvc.shape[2]
    nsel = S // SEL_LEN
    qs = min(nsub * Q_BLOCK, S)
    nsub = qs // Q_BLOCK
    assert nsel <= LANES and ncp % LANES == 0 and S % qs == 0
    assert CMP_LEN == 2 * CMP_STRIDE and Q_BLOCK % CMP_STRIDE == 0 and Q_BLOCK // CMP_STRIDE < LANES
    ov = jnp.asarray(_overlap_matrix(ncp, LANES, nsel), BF16)
    ca = jnp.asarray(_cmp_staircase(), BF16)
    qw = NSA_HG * HEAD_DIM
    return pl.pallas_call(
        functools.partial(_nsa_select_kernel, nsub=nsub),
        grid=(NSA_KV, S // qs),
        in_specs=[
            pl.BlockSpec((qs, qw), lambda g, n: (n, OFF_Q // qw + g)),
            pl.BlockSpec((qs, LANES), lambda g, n: (n, 0)),
            pl.BlockSpec((qs, LANES), lambda g, n: (n, 0)),
            pl.BlockSpec((qs, LANES), lambda g, n: (n, OFF_G // LANES + g)),
            pl.BlockSpec((1, 1, ncp, HEAD_DIM), lambda g, n: (0, g, 0, 0)),
            pl.BlockSpec((1, 1, ncp, HEAD_DIM), lambda g, n: (1, g, 0, 0)),
            pl.BlockSpec((ncp, LANES), lambda g, n: (0, 0)),
            pl.BlockSpec((Q_BLOCK, LANES), lambda g, n: (0, 0)),
        ],
        out_specs=[
            pl.BlockSpec((qs, LANES), lambda g, n: (n, g)),
            pl.BlockSpec((qs, qw), lambda g, n: (n, g)),
            pl.BlockSpec((qs, qw), lambda g, n: (n, g)),
        ],
        out_shape=[
            jax.ShapeDtypeStruct((S, NSA_KV * LANES), BF16),
            jax.ShapeDtypeStruct((S, NSA_WIDTH), F32),
            jax.ShapeDtypeStruct((S, NSA_WIDTH), BF16),
        ],
        compiler_params=_cparams(("arbitrary", "arbitrary")),
        name="nsa_select",
    )(proj, cosf, sinf, proj, kcvc, kcvc, ov, ca)


def _nsa_kernel(q_ref, gate_ref, z_ref, selb_ref, pc_ref, wb_ref, cb_ref,
                ks_ref, vs_ref, kw_ref, vw_ref, o_ref, m_ref, accl_ref, sa_ref, sb_ref, part_ref, *, tk):
    n = pl.program_id(1)
    q0 = n * Q_BLOCK
    R = NSA_HG * Q_BLOCK
    q4 = jnp.concatenate([q_ref[:, h * HEAD_DIM:(h + 1) * HEAD_DIM] for h in range(NSA_HG)], axis=0)
    per_head = _per_head

    wspan = WINDOW + Q_BLOCK
    w0 = pl.multiple_of(jnp.maximum(q0 - WINDOW, 0), Q_BLOCK)
    sw = (_halves(_dot_nt, q4, kw_ref[pl.ds(w0, wspan), :])
          + per_head(wb_ref[jnp.minimum(n, WINDOW // Q_BLOCK)]))
    ew = jnp.exp2(sw - jnp.max(sw, axis=-1, keepdims=True))
    vwa = jnp.concatenate([vw_ref[pl.ds(w0, wspan), :], jnp.ones((wspan, LANES), BF16)], axis=1)
    pvw = _halves(_dot, ew.astype(BF16), vwa)
    o_win = pvw[:, :LANES] / pvw[:, LANES:]

    gate = jax.nn.sigmoid(gate_ref[...])
    for h in range(NSA_HG):
        cs = slice(h * HEAD_DIM, (h + 1) * HEAD_DIM)
        part_ref[:, cs] = pc_ref[:, cs] + gate[:, 3 * h + 2:3 * h + 3] * o_win[h * Q_BLOCK:(h + 1) * Q_BLOCK]

    sel_bias = per_head(selb_ref[...])

    m_ref[...] = jnp.full(m_ref.shape, NEG_BIG, F32)
    accl_ref[...] = jnp.zeros(accl_ref.shape, F32)
    qa = jnp.concatenate([q4, sel_bias], axis=1)
    key_blk = lax.broadcasted_iota(jnp.int32, (tk, LANES), 0) >> 6
    blk_lane = lax.broadcasted_iota(jnp.int32, (tk, LANES), 1)
    ones_k = jnp.ones((tk, LANES), BF16)
    rep = tk // LANES

    n_q = tk // Q_BLOCK
    last = lax.shift_right_logical(n, n_q.bit_length() - 1)
    diag = n - last * n_q

    def score_tile(kt, dst_ref):
        k0 = pl.multiple_of(kt * tk, tk)
        onehot = jnp.where(key_blk + kt * (tk // SEL_LEN) == blk_lane, 1.0, 0.0).astype(BF16)
        dst_ref[...] = _dot_nt(qa, jnp.concatenate([ks_ref[pl.ds(k0, tk), :], onehot], axis=1))

    def consume_tile(kt, src_ref, causal=False):
        k0 = pl.multiple_of(kt * tk, tk)
        sc = src_ref[...]
        if causal:
            sc = sc + per_head(cb_ref[diag])
        m_prev = m_ref[...]
        m_next = jnp.maximum(m_prev, jnp.max(sc, axis=-1, keepdims=True))
        pr = jnp.exp2(sc - jnp.concatenate([m_next] * rep, axis=1))
        alpha = jnp.exp2(m_prev - m_next)
        va = jnp.concatenate([vs_ref[pl.ds(k0, tk), :], ones_k], axis=1)
        accl_ref[...] = jnp.concatenate([alpha, alpha], axis=1) * accl_ref[...] + _dot(pr.astype(BF16), va)
        m_ref[...] = m_next

    score_tile(0, sa_ref)

    def slc_pair(i, carry):
        score_tile(2 * i + 1, sb_ref)
        consume_tile(2 * i, sa_ref)
        score_tile(2 * i + 2, sa_ref)
        consume_tile(2 * i + 1, sb_ref)
        return carry

    lax.fori_loop(0, lax.shift_right_logical(last, 1), slc_pair, 0)

    @pl.when((last & 1) == 0)
    def _():
        consume_tile(last, sa_ref, causal=True)

    @pl.when((last & 1) == 1)
    def _():
        score_tile(last, sb_ref)
        consume_tile(last - 1, sa_ref)
        consume_tile(last, sb_ref, causal=True)

    o_slc = accl_ref[:, :LANES] / accl_ref[:, LANES:]

    gsig = jax.nn.sigmoid(gate_ref[...])
    for h in range(NSA_HG):
        rs = slice(h * Q_BLOCK, (h + 1) * Q_BLOCK)
        cs = slice(h * HEAD_DIM, (h + 1) * HEAD_DIM)
        oh = part_ref[:, cs] + gsig[:, 3 * h + 1:3 * h + 2] * o_slc[rs]
        z = z_ref[:, cs]
        o_ref[:, cs] = (oh * (z * jax.nn.sigmoid(z))).astype(o_ref.dtype)


def _cmp_staircase():
    nb = Q_BLOCK // CMP_STRIDE
    a = np.zeros((Q_BLOCK, LANES), np.float32)
    a[:, 0] = 1.0
    r = np.arange(Q_BLOCK)[:, None]
    i = np.arange(nb)[None, :]
    a[:, 1:1 + nb] = np.where(r < CMP_STRIDE * i + (CMP_STRIDE - 1), NEG_BIG, 0.0)
    return a


def _window_masks():
    n_w = WINDOW // Q_BLOCK
    r = np.arange(Q_BLOCK)[None, :, None]
    c = np.arange(WINDOW + Q_BLOCK)[None, None, :]
    d = np.arange(n_w + 1)[:, None, None]
    ok = np.where(d < n_w, c <= Q_BLOCK * d + r, (c > r) & (c <= WINDOW + r))
    return np.where(ok, 0.0, NEG_BIG).astype(np.float32)


def _causal_staircases(tk):
    n_q = tk // Q_BLOCK
    r = np.arange(Q_BLOCK)[None, :, None]
    c = np.arange(tk)[None, None, :]
    d = np.arange(n_q + 1)[:, None, None]
    return np.where((c <= Q_BLOCK * d + r) | (d == n_q), 0.0, NEG_BIG).astype(np.float32)


def _overlap_matrix(ncp, nsel_pad, nsel):
    ci = np.arange(ncp)[:, None] * CMP_STRIDE
    sj = np.arange(nsel_pad)[None, :] * SEL_LEN
    ov = (ci < sj + SEL_LEN) & (ci + CMP_LEN > sj) & (np.arange(nsel_pad)[None, :] < nsel) & (np.arange(ncp)[:, None] < ncp - 1)
    return ov.astype(np.float32)


def _nsa(proj, qb, selb, pcmp, kvb, tk=512):
    S = proj.shape[0]
    assert S % tk == 0 and S >= WINDOW + Q_BLOCK and WINDOW % Q_BLOCK == 0
    wb = jnp.asarray(_window_masks(), F32)
    n_q = tk // Q_BLOCK
    assert n_q & (n_q - 1) == 0 and tk % SEL_LEN == 0
    cb = jnp.asarray(_causal_staircases(tk), F32)
    qw = NSA_HG * HEAD_DIM
    R = NSA_HG * Q_BLOCK
    kern = functools.partial(_nsa_kernel, tk=tk)
    return pl.pallas_call(
        kern,
        grid=(NSA_KV, S // Q_BLOCK),
        in_specs=[
            pl.BlockSpec((Q_BLOCK, qw), lambda g, n: (n, g)),
            pl.BlockSpec((Q_BLOCK, LANES), lambda g, n: (n, OFF_G // LANES + g)),
            pl.BlockSpec((Q_BLOCK, qw), lambda g, n: (n, OFF_Z // qw + g)),
            pl.BlockSpec((Q_BLOCK, LANES), lambda g, n: (n, g)),
            pl.BlockSpec((Q_BLOCK, qw), lambda g, n: (n, g)),
            pl.BlockSpec((WINDOW // Q_BLOCK + 1, Q_BLOCK, WINDOW + Q_BLOCK), lambda g, n: (0, 0, 0)),
            pl.BlockSpec((n_q + 1, Q_BLOCK, tk), lambda g, n: (0, 0, 0)),
            pl.BlockSpec((S, HEAD_DIM), lambda g, n: (0, 0 * NSA_KV + g)),
            pl.BlockSpec((S, HEAD_DIM), lambda g, n: (0, 1 * NSA_KV + g)),
            pl.BlockSpec((S, HEAD_DIM), lambda g, n: (0, 2 * NSA_KV + g)),
            pl.BlockSpec((S, HEAD_DIM), lambda g, n: (0, 3 * NSA_KV + g)),
        ],
        out_specs=pl.BlockSpec((Q_BLOCK, qw), lambda g, n: (n, g)),
        out_shape=jax.ShapeDtypeStruct((S, NSA_WIDTH), BF16),
        scratch_shapes=[
            pltpu.VMEM((R, LANES), F32),
            pltpu.VMEM((R, HEAD_DIM + LANES), F32),
            pltpu.VMEM((R, tk), F32),
            pltpu.VMEM((R, tk), F32),
            pltpu.VMEM((Q_BLOCK, qw), F32),
        ],
        compiler_params=_cparams(("arbitrary", "arbitrary")),
        name="nsa",
    )(qb, proj, proj, selb, pcmp, wb, cb, kvb, kvb, kvb, kvb)


def _merge_kernel(ohg_ref, onsa_ref, g0_ref, g1_ref, g2_ref, g3_ref, whg_ref, wnsa_ref, o_ref):
    y_hg = _dot(ohg_ref[...], whg_ref[...])
    y_nsa = _dot(onsa_ref[...], wnsa_ref[...])
    half = D_MODEL // 2
    gh = (g0_ref, g1_ref)
    gn = (g2_ref, g3_ref)
    for c in range(2):
        cs = slice(c * half, (c + 1) * half)
        m = jax.nn.sigmoid(gh[c][...]) * y_hg[:, cs] + jax.nn.sigmoid(gn[c][...]) * y_nsa[:, cs]
        o_ref[:, cs] = m.astype(o_ref.dtype)


def _merge(o_hg, o_nsa, proj, w_hg, w_nsa, tm=256):
    S = o_hg.shape[0]
    tm = min(tm, S)
    half = D_MODEL // 2
    gb = OFF_MG // half
    return pl.pallas_call(
        _merge_kernel,
        grid=(S // tm,),
        in_specs=[
            pl.BlockSpec((tm, HG_WIDTH), lambda i: (i, 0)),
            pl.BlockSpec((tm, NSA_WIDTH), lambda i: (i, 0)),
            pl.BlockSpec((tm, half), lambda i: (i, gb + 0)),
            pl.BlockSpec((tm, half), lambda i: (i, gb + 1)),
            pl.BlockSpec((tm, half), lambda i: (i, gb + 2)),
            pl.BlockSpec((tm, half), lambda i: (i, gb + 3)),
            pl.BlockSpec((HG_WIDTH, D_MODEL), lambda i: (0, 0)),
            pl.BlockSpec((NSA_WIDTH, D_MODEL), lambda i: (0, 0)),
        ],
        out_specs=pl.BlockSpec((tm, D_MODEL), lambda i: (i, 0)),
        out_shape=jax.ShapeDtypeStruct((S, D_MODEL), BF16),
        compiler_params=_cparams(("arbitrary",)),
        name="merge",
    )(o_hg, o_nsa, proj, proj, proj, proj, w_hg, w_nsa)


def _out_kernel(x_ref, m_ref, w_ref, nw_ref, o_ref):
    h = x_ref[...] + _dot(m_ref[...], w_ref[...])
    ms = jnp.mean(h * h, axis=-1, keepdims=True)
    o_ref[...] = h * lax.rsqrt(ms + EPS) * nw_ref[...]


def _out(x2, merged, w_out, final_w, tm=256):
    S, D = x2.shape
    tm = min(tm, S)
    return pl.pallas_call(
        _out_kernel,
        grid=(S // tm,),
        in_specs=[
            pl.BlockSpec((tm, D), lambda i: (i, 0)),
            pl.BlockSpec((tm, D), lambda i: (i, 0)),
            pl.BlockSpec((D, D), lambda i: (0, 0)),
            pl.BlockSpec((1, D), lambda i: (0, 0)),
        ],
        out_specs=pl.BlockSpec((tm, D), lambda i: (i, 0)),
        out_shape=jax.ShapeDtypeStruct((S, D), F32),
        compiler_params=_cparams(("arbitrary",)),
        name="out_proj",
    )(x2, merged, w_out, final_w.reshape(1, D))


def _layer(x2, norm_w, w_in, lb_logits, hg_norm_w, cmp_k_pos, cmp_k_w1, cmp_k_b1, cmp_k_w2,
           cmp_v_pos, cmp_v_w1, cmp_v_b1, cmp_v_w2, w_branch_hg, w_branch_nsa, w_out, final_w):
    S = x2.shape[0]
    proj = _norm_proj(x2, norm_w, _w_prep(w_in.T))
    cosf, sinf = _rope_tables(S)
    segs, kvb = _kv_prep(proj, cosf, sinf)
    pe = jnp.stack([cmp_k_pos.reshape(1, -1), cmp_v_pos.reshape(1, -1)])
    w1 = jnp.stack([cmp_k_w1, cmp_v_w1]).astype(BF16)
    b1 = jnp.stack([cmp_k_b1.reshape(1, -1), cmp_v_b1.reshape(1, -1)])
    w2 = jnp.stack([cmp_k_w2, cmp_v_w2]).astype(BF16)
    kcvc = _compress(segs, pe, w1, b1, w2)
    o_hg = _hgrn(proj, lb_logits, hg_norm_w)
    selb, pcmp, qb = _nsa_select(proj, cosf, sinf, kcvc)
    o_nsa = _nsa(proj, qb, selb, pcmp, kvb)
    merged = _merge(o_hg, o_nsa, proj, w_branch_hg.astype(BF16), w_branch_nsa.astype(BF16))
    return _out(x2, merged, w_out.astype(BF16), final_w)


def kernel(x, norm_w, w_in, hg_lb_logits, hg_norm_w, cmp_k_pos, cmp_k_w1, cmp_k_b1, cmp_k_w2, cmp_v_pos, cmp_v_w1, cmp_v_b1, cmp_v_w2, w_branch_hg, w_branch_nsa, w_out, final_norm_w):
    B, S, D = x.shape
    assert B == 1 and D == D_MODEL and norm_w.shape[0] == 1
    y = _layer(x[0], norm_w[0], w_in[0], hg_lb_logits, hg_norm_w[0], cmp_k_pos[0], cmp_k_w1[0], cmp_k_b1[0],
               cmp_k_w2[0], cmp_v_pos[0], cmp_v_w1[0], cmp_v_b1[0], cmp_v_w2[0], w_branch_hg[0],
               w_branch_nsa[0], w_out[0], final_norm_w)
    return y[None]
```

```python
import functools

import numpy as np
import jax
import jax.numpy as jnp
from jax import lax
from jax.experimental import pallas as pl
from jax.experimental.pallas import tpu as pltpu

F32 = jnp.float32
BF16 = jnp.bfloat16

D_MODEL = 2048
HG_HEADS = 8
HG_DK = 128
HG_DV = 128
HG_FDIM = HG_HEADS * HG_DK
HG_WIDTH = HG_HEADS * HG_DV
HG_MM_LEVELS = 3
NSA_HEADS = 16
NSA_KV = 4
NSA_HG = NSA_HEADS // NSA_KV
HEAD_DIM = 128
NSA_WIDTH = NSA_HEADS * HEAD_DIM
NSA_KVW = NSA_KV * HEAD_DIM
CMP_LEN = 32
CMP_STRIDE = 16
CMP_HIDDEN = 512
SEL_LEN = 64
SEL_TOPK = 16
WINDOW = 512
Q_BLOCK = 256
ROPE_THETA = 500000.0
ROT_DIM = HEAD_DIM // 4
ROT_HALF = ROT_DIM // 2
EPS = 1e-6
LOG2E = float(np.log2(np.e))

LANES = 128
NEG_BIG = -1e30
VMEM_LIMIT = 56 * 1024 * 1024

OFF_HG = 0
OFF_Q = 4 * HG_FDIM
OFF_KV = OFF_Q + NSA_WIDTH
OFF_MG = OFF_KV + 6 * NSA_KVW
OFF_Z = OFF_MG + 2 * D_MODEL
OFF_G = OFF_Z + NSA_WIDTH
PROJ_COLS = OFF_G + NSA_KV * LANES
IN_GATE_OFF = OFF_MG
IN_Z_OFF = IN_GATE_OFF + 3 * NSA_HEADS
IN_MG_OFF = IN_Z_OFF + NSA_WIDTH


def _dot(a, b):
    return jnp.dot(a, b, preferred_element_type=F32)


def _dot_nt(a, b):
    return lax.dot_general(a, b, (((1,), (1,)), ((), ())), preferred_element_type=F32)


def _halves(dot, a, b):
    h = a.shape[0] // 2
    return jnp.concatenate([dot(a[:h], b), dot(a[h:], b)], axis=0)


def _split_bf16(a):
    hi = a.astype(BF16)
    lo = (a - hi.astype(F32)).astype(BF16)
    return hi, lo


def _cparams(sem):
    return pltpu.CompilerParams(dimension_semantics=sem, vmem_limit_bytes=VMEM_LIMIT)


def _w_prep_kernel(wt_ref, wg_ref, o_ref, *, n_direct):
    j = pl.program_id(0)

    @pl.when(j < n_direct)
    def _():
        o_ref[...] = wt_ref[...].astype(BF16)

    @pl.when(j >= n_direct)
    def _():
        o_ref[...] = wg_ref[...].astype(BF16)


def _w_prep(wt, tn=512):
    D = wt.shape[1]
    n_main, n_mg, n_z = OFF_MG // tn, 2 * D_MODEL // tn, NSA_WIDTH // tn
    n_direct = n_main + n_mg + n_z
    assert OFF_MG % tn == 0 and D_MODEL % tn == 0 and NSA_KV * LANES == tn
    gates = wt[IN_GATE_OFF:IN_Z_OFF].reshape(NSA_KV, 3 * NSA_HG, D)
    gates = jnp.pad(gates, ((0, 0), (0, LANES - 3 * NSA_HG), (0, 0))).reshape(NSA_KV * LANES, D)
    n_extra = 2
    gates = jnp.pad(gates, ((0, (n_extra - 1) * tn), (0, 0)))

    sub = 8
    assert IN_MG_OFF % sub == 0 and IN_Z_OFF % sub == 0 and tn % sub == 0

    def src_row(j):
        in_mg = IN_MG_OFF // sub + (j - n_main) * (tn // sub)
        in_z = IN_Z_OFF // sub + (jnp.minimum(j, n_direct - 1) - n_main - n_mg) * (tn // sub)
        return jnp.where(j < n_main, j * (tn // sub), jnp.where(j < n_main + n_mg, in_mg, in_z)) * sub

    return pl.pallas_call(
        functools.partial(_w_prep_kernel, n_direct=n_direct),
        grid=(n_direct + n_extra,),
        in_specs=[
            pl.BlockSpec((pl.Element(tn), pl.Element(D)), lambda j: (src_row(j), 0)),
            pl.BlockSpec((tn, D), lambda j: (jnp.maximum(j - n_direct, 0), 0)),
        ],
        out_specs=pl.BlockSpec((tn, D), lambda j: (j, 0)),
        out_shape=jax.ShapeDtypeStruct(((n_direct + n_extra) * tn, D), BF16),
        compiler_params=_cparams(("arbitrary",)),
        name="w_prep",
    )(wt, gates)


def _norm_proj_kernel(x_ref, nw_ref, w_ref, o_ref, xn_ref):
    @pl.when(pl.program_id(1) == 0)
    def _():
        x = x_ref[...]
        ms = jnp.mean(x * x, axis=-1, keepdims=True)
        xn_ref[...] = (x * lax.rsqrt(ms + EPS) * nw_ref[...]).astype(BF16)

    o_ref[...] = _dot_nt(xn_ref[...], w_ref[...])


def _norm_proj(x2, norm_w, wtb, tm=1024, tn=1024):
    S, D = x2.shape
    N = wtb.shape[0]
    tm = min(tm, S)
    return pl.pallas_call(
        _norm_proj_kernel,
        grid=(S // tm, N // tn),
        in_specs=[
            pl.BlockSpec((tm, D), lambda i, j: (i, 0)),
            pl.BlockSpec((1, D), lambda i, j: (0, 0)),
            pl.BlockSpec((tn, D), lambda i, j: (j, 0)),
        ],
        out_specs=pl.BlockSpec((tm, tn), lambda i, j: (i, j)),
        out_shape=jax.ShapeDtypeStruct((S, N), F32),
        scratch_shapes=[pltpu.VMEM((tm, D), BF16)],
        compiler_params=_cparams(("arbitrary", "arbitrary")),
        name="norm_proj",
    )(x2, norm_w.reshape(1, D), wtb)


def _rope(x, cosf, sinf):
    lane = lax.broadcasted_iota(jnp.int32, x.shape, 1)
    rot = jnp.where(lane < ROT_HALF, pltpu.roll(x, LANES - ROT_HALF, 1), pltpu.roll(x, ROT_HALF, 1))
    return x * cosf + rot * sinf


def _rope_tables(S):
    pos = np.arange(S, dtype=np.float64)
    inv = ROPE_THETA ** (-np.arange(0, ROT_DIM, 2, dtype=np.float64) / ROT_DIM)
    ang = pos[:, None] * inv[None, :]
    cos, sin = np.cos(ang), np.sin(ang)
    rest = LANES - ROT_DIM
    cosf = np.concatenate([cos, cos, np.ones((S, rest))], axis=1)
    sinf = np.concatenate([-sin, sin, np.zeros((S, rest))], axis=1)
    return jnp.asarray(cosf, F32), jnp.asarray(sinf, F32)


def _kv_prep_kernel(kv_ref, cos_ref, sin_ref, cmp_ref, kvb_ref, row_ref):
    cosf = cos_ref[...]
    sinf = sin_ref[...]
    W = NSA_KVW
    n_seg = row_ref.shape[0] // CMP_STRIDE

    def to_segments(a, g, rows):
        row_ref[...] = rows
        for l in range(CMP_STRIDE):
            cmp_ref[a, g, :, l * HEAD_DIM:(l + 1) * HEAD_DIM] = row_ref[pl.ds(l, n_seg, stride=CMP_STRIDE), :]

    for g in range(NSA_KV):
        kc = kv_ref[:, 0 * W + g * LANES:0 * W + (g + 1) * LANES]
        to_segments(0, g, _rope(kc, cosf, sinf))
        to_segments(1, g, kv_ref[:, 1 * W + g * LANES:1 * W + (g + 1) * LANES])
        ks = kv_ref[:, 2 * W + g * LANES:2 * W + (g + 1) * LANES]
        kvb_ref[:, 0 * W + g * LANES:0 * W + (g + 1) * LANES] = _rope(ks, cosf, sinf).astype(BF16)
        kvb_ref[:, 1 * W + g * LANES:1 * W + (g + 1) * LANES] = kv_ref[:, 3 * W + g * LANES:3 * W + (g + 1) * LANES].astype(BF16)
        kw = kv_ref[:, 4 * W + g * LANES:4 * W + (g + 1) * LANES]
        kvb_ref[:, 2 * W + g * LANES:2 * W + (g + 1) * LANES] = _rope(kw, cosf, sinf).astype(BF16)
        kvb_ref[:, 3 * W + g * LANES:3 * W + (g + 1) * LANES] = kv_ref[:, 5 * W + g * LANES:5 * W + (g + 1) * LANES].astype(BF16)


def _kv_prep(proj, cosf, sinf, tm=512):
    S = proj.shape[0]
    tm = min(tm, S)
    kvw = 6 * NSA_KVW
    return pl.pallas_call(
        _kv_prep_kernel,
        grid=(S // tm,),
        in_specs=[
            pl.BlockSpec((tm, kvw), lambda i: (i, OFF_KV // kvw)),
            pl.BlockSpec((tm, LANES), lambda i: (i, 0)),
            pl.BlockSpec((tm, LANES), lambda i: (i, 0)),
        ],
        out_specs=[
            pl.BlockSpec((2, NSA_KV, tm // CMP_STRIDE, CMP_STRIDE * HEAD_DIM), lambda i: (0, 0, i, 0)),
            pl.BlockSpec((tm, 4 * NSA_KVW), lambda i: (i, 0)),
        ],
        out_shape=[
            jax.ShapeDtypeStruct((2, NSA_KV, S // CMP_STRIDE, CMP_STRIDE * HEAD_DIM), F32),
            jax.ShapeDtypeStruct((S, 4 * NSA_KVW), BF16),
        ],
        scratch_shapes=[pltpu.VMEM((tm, HEAD_DIM), F32)],
        compiler_params=_cparams(("arbitrary",)),
        name="kv_prep",
    )(proj, cosf, sinf)


def _compress_kernel(seg_ref, pe_ref, w1_ref, b1_ref, w2_ref, o_ref):
    half = (CMP_LEN // 2) * HEAD_DIM
    seg = seg_ref[0, 0]
    n_seg = seg.shape[0]
    pe = pe_ref[0]
    a = (seg + pe[:, :half]).astype(BF16)
    b = (seg + pe[:, half:]).astype(BF16)
    u = _dot(a, w1_ref[0, :half, :])
    v = _dot(b, w1_ref[0, half:, :])
    v_next = pltpu.roll(v, n_seg - 1, 0)
    pre = u + v_next + b1_ref[0]
    h = 0.5 * pre * (1.0 + jnp.tanh(np.sqrt(2.0 / np.pi).astype(np.float32) * (pre + 0.044715 * (pre * pre * pre))))
    o_ref[0, 0] = _dot(h.astype(BF16), w2_ref[0])


def _compress(segs, pe, w1, b1, w2):
    _, G, n_seg, segw = segs.shape
    return pl.pallas_call(
        _compress_kernel,
        grid=(2, G),
        in_specs=[
            pl.BlockSpec((1, 1, n_seg, segw), lambda a, g: (a, g, 0, 0)),
            pl.BlockSpec((1, 1, 2 * segw), lambda a, g: (a, 0, 0)),
            pl.BlockSpec((1, 2 * segw, CMP_HIDDEN), lambda a, g: (a, 0, 0)),
            pl.BlockSpec((1, 1, CMP_HIDDEN), lambda a, g: (a, 0, 0)),
            pl.BlockSpec((1, CMP_HIDDEN, HEAD_DIM), lambda a, g: (a, 0, 0)),
        ],
        out_specs=pl.BlockSpec((1, 1, n_seg, HEAD_DIM), lambda a, g: (a, g, 0, 0)),
        out_shape=jax.ShapeDtypeStruct((2, G, n_seg, HEAD_DIM), F32),
        compiler_params=_cparams(("arbitrary", "arbitrary")),
        name="compress",
    )(segs, pe, w1, b1, w2)


def _hgrn_consts(ch):
    nl = int(np.log2(ch))
    assert 1 << nl == ch
    t = np.arange(ch)[:, None]
    r = np.arange(ch)[None, :]
    mats = [r <= t]
    masks = [np.eye(ch, dtype=bool)]
    for l in range(nl):
        half = 1 << l
        blk = t // (2 * half)
        ref = blk * 2 * half + half - 1
        up = ((t >> l) & 1) == 1
        mats.append(np.where(up, (r > ref) & (r <= t), (r > t) & (r <= ref)))
        masks.append(up & (((r >> l) & 1) == 0) & (blk == r // (2 * half)))
    mc = np.concatenate(mats[:1 + HG_MM_LEVELS], axis=0).astype(np.float32)
    lm = np.stack(masks, axis=0).astype(np.float32)
    return nl, mc, lm


def _hgrn_kernel(q_ref, f_ref, i_ref, z_ref, lbl_ref, nw_ref, mc_ref, lm_ref, o_ref, st_ref, e_ref, *, ch, nl):
    @pl.when(pl.program_id(0) == 0)
    def _():
        st_ref[...] = jnp.zeros_like(st_ref)

    lg = lbl_ref[...]
    ex = jnp.exp(lg - jnp.max(lg, axis=0, keepdims=True))
    lb = ex[0:1] / jnp.sum(ex, axis=0, keepdims=True)
    f = lb + (1.0 - lb) * jax.nn.sigmoid(f_ref[...])
    g_hi, g_lo = _split_bf16(jnp.log(f))
    mc = mc_ref[...]
    e_ref[...] = _dot(mc, g_hi) + _dot(mc, g_lo)
    row = lax.broadcasted_iota(jnp.int32, (ch, 1), 0)
    nw = nw_ref[...]

    heads = [slice(h * HG_DK, (h + 1) * HG_DK) for h in range(HG_HEADS)]
    qs = [q_ref[:, sl] for sl in heads]
    ks = [1.0 - f[:, sl] for sl in heads]
    scs = [_dot_nt(q.astype(BF16), k.astype(BF16)) * lm_ref[0] for q, k in zip(qs, ks)]
    bs = [e_ref[0:ch, sl] for sl in heads]
    for l in range(nl):
        up = ((row >> l) & 1) == 1
        half = 1 << l
        for h, sl in enumerate(heads):
            if l < HG_MM_LEVELS:
                el = e_ref[(l + 1) * ch:(l + 2) * ch, sl]
                xl = jnp.where(up, qs[h], ks[h]) * jnp.exp(el)
            else:
                parts = []
                for r0 in range(0, ch, 2 * half):
                    mid = r0 + half
                    edge = bs[h][mid - 1:mid, :]
                    parts.append(ks[h][r0:mid] * jnp.exp(edge - bs[h][r0:mid]))
                    parts.append(qs[h][mid:mid + half] * jnp.exp(bs[h][mid:mid + half] - edge))
                xl = jnp.concatenate(parts, axis=0)
            xl = xl.astype(BF16)
            scs[h] = scs[h] + _dot_nt(xl, xl) * lm_ref[l + 1]

    vbs = [i_ref[:, sl].astype(BF16) for sl in heads]
    sts = [st_ref[h] for h in range(HG_HEADS)]
    inters = [_dot_nt((qs[h] * jnp.exp(bs[h])).astype(BF16), sts[h].astype(BF16)) for h in range(HG_HEADS)]
    for h, sl in enumerate(heads):
        ke = (ks[h] * jnp.exp(bs[h][ch - 1:ch, :] - bs[h])).astype(BF16)
        upd = lax.dot_general(vbs[h], ke, (((0,), (0,)), ((), ())), preferred_element_type=F32)
        st_ref[h] = sts[h] * jnp.exp(bs[h][ch - 1:ch, :]) + upd
    outs = [inters[h] + _dot(scs[h].astype(BF16), vbs[h]) for h in range(HG_HEADS)]
    for h, sl in enumerate(heads):
        o = outs[h]
        ms = jnp.mean(o * o, axis=-1, keepdims=True)
        z = z_ref[:, sl]
        o_ref[:, sl] = (o * lax.rsqrt(ms + EPS) * nw * (z * jax.nn.sigmoid(z))).astype(o_ref.dtype)


def _hgrn(proj, lb_logits, hg_norm_w, ch=128):
    S = proj.shape[0]
    nl, mc, lm = _hgrn_consts(ch)
    kern = functools.partial(_hgrn_kernel, ch=ch, nl=nl)
    nlb = lb_logits.shape[0]
    return pl.pallas_call(
        kern,
        grid=(S // ch,),
        in_specs=[
            pl.BlockSpec((ch, HG_FDIM), lambda c: (c, 0)),
            pl.BlockSpec((ch, HG_FDIM), lambda c: (c, 1)),
            pl.BlockSpec((ch, HG_WIDTH), lambda c: (c, 2)),
            pl.BlockSpec((ch, HG_WIDTH), lambda c: (c, 3)),
            pl.BlockSpec((nlb, HG_FDIM), lambda c: (0, 0)),
            pl.BlockSpec((1, HG_DV), lambda c: (0, 0)),
            pl.BlockSpec(mc.shape, lambda c: (0, 0)),
            pl.BlockSpec(lm.shape, lambda c: (0, 0, 0)),
        ],
        out_specs=pl.BlockSpec((ch, HG_WIDTH), lambda c: (c, 0)),
        out_shape=jax.ShapeDtypeStruct((S, HG_WIDTH), BF16),
        scratch_shapes=[
            pltpu.VMEM((HG_HEADS, HG_DV, HG_DK), F32),
            pltpu.VMEM((mc.shape[0], HG_FDIM), F32),
        ],
        compiler_params=_cparams(("arbitrary",)),
        name="hgrn",
    )(proj, proj, proj, proj, lb_logits, hg_norm_w.reshape(1, HG_DV), jnp.asarray(mc, BF16), jnp.asarray(lm, F32))


def _stack_heads(q_ref, rows, cosf, sinf):
    qscale = (HEAD_DIM ** -0.5) * LOG2E
    return jnp.concatenate(
        [(_rope(q_ref[rows, h * HEAD_DIM:(h + 1) * HEAD_DIM], cosf, sinf) * qscale).astype(BF16)
         for h in range(NSA_HG)], axis=0)


def _per_head(a):
    return jnp.concatenate([a] * NSA_HG, axis=0)


def _nsa_select_kernel(q_ref, cos_ref, sin_ref, gate_ref, kc_ref, vc_ref, ov_ref, ca_ref,
                       selb_ref, pc_ref, qb_ref, *, nsub):
    n = pl.program_id(1)
    R = NSA_HG * Q_BLOCK
    ncp = kc_ref.shape[2]
    nsel_pad = ov_ref.shape[1]
    nb = Q_BLOCK // CMP_STRIDE
    nsb = Q_BLOCK // SEL_LEN
    subs = range(nsub)
    rows = [slice(i * Q_BLOCK, (i + 1) * Q_BLOCK) for i in subs]
    q0s = [(n * nsub + i) * Q_BLOCK for i in subs]

    def body(ncols):
        nrows = ncols // nb * nsb
        crow = lax.broadcasted_iota(jnp.int32, (ncols, LANES), 0)
        clane = lax.broadcasted_iota(jnp.int32, (ncols, LANES), 1)
        kc = kc_ref[0, 0, :ncols, :].astype(BF16)
        vca = jnp.concatenate([vc_ref[0, 0, :ncols, :].astype(BF16), jnp.ones((ncols, LANES), BF16)], axis=1)
        ca4 = _per_head(ca_ref[...])
        ov = ov_ref[:ncols, :]

        ss = []
        for i in subs:
            q4 = _stack_heads(q_ref, rows[i], cos_ref[rows[i], :], sin_ref[rows[i], :])
            for h in range(NSA_HG):
                qb_ref[rows[i], h * HEAD_DIM:(h + 1) * HEAD_DIM] = q4[h * Q_BLOCK:(h + 1) * Q_BLOCK]
            c_first = (n * nsub + i) * nb - (CMP_LEN // CMP_STRIDE - 1)
            flags = jnp.where(clane == 0, jnp.where(crow >= c_first + nb, NEG_BIG, 0.0),
                              jnp.where(crow == c_first + clane - 1, 1.0, 0.0)).astype(BF16)
            ss.append(_halves(_dot_nt, jnp.concatenate([q4, ca4], axis=1), jnp.concatenate([kc, flags], axis=1)))
        ebs = [jnp.exp2(s - jnp.max(s, axis=-1, keepdims=True)).astype(BF16) for s in ss]
        pvs = [_halves(_dot, e, vca) for e in ebs]

        imps = []
        for i in subs:
            t_row = q0s[i] + (lax.broadcasted_iota(jnp.int32, (R, LANES), 0) & (Q_BLOCK - 1))
            inv = jnp.where(t_row >= CMP_LEN - 1, 1.0, 0.0) / pvs[i][:, LANES:]
            o_cmp = pvs[i][:, :LANES] * inv
            gate = jax.nn.sigmoid(gate_ref[rows[i], :])
            for h in range(NSA_HG):
                pc_ref[rows[i], h * HEAD_DIM:(h + 1) * HEAD_DIM] = (
                    gate[:, 3 * h:3 * h + 1] * o_cmp[h * Q_BLOCK:(h + 1) * Q_BLOCK])
            eo = _dot(ebs[i], ov) * inv
            imp = eo[0:Q_BLOCK]
            for h in range(1, NSA_HG):
                imp = imp + eo[h * Q_BLOCK:(h + 1) * Q_BLOCK]
            imps.append(imp)

        jj = lax.broadcasted_iota(jnp.int32, (nrows, Q_BLOCK), 0)
        jjf = jj.astype(F32)
        lane_q = lax.broadcasted_iota(jnp.int32, (nrows, Q_BLOCK), 1)
        TAKEN = -2.0
        scores = []
        for i in subs:
            tq = q0s[i] + lane_q
            jt = tq >> 6
            forced = (jj == 0) | (jj == jt) | (jj == jt - 1)
            scores.append(jnp.where(jj * SEL_LEN <= tq, jnp.where(forced, TAKEN, imps[i].T[:nrows]), -1.0))
        for _ in range(SEL_TOPK - 3):
            for i in subs:
                mx = jnp.max(scores[i], axis=0, keepdims=True)
                first = jnp.min(jnp.where(scores[i] == mx, jjf, float(nsel_pad)), axis=0, keepdims=True)
                first = jnp.where(mx >= 0.0, first, -1.0)
                scores[i] = jnp.where(jjf == first, TAKEN, scores[i])
        for i in subs:
            bias = jnp.where(scores[i] == TAKEN, 0.0, NEG_BIG)
            if nrows < nsel_pad:
                bias = jnp.concatenate([bias, jnp.full((nsel_pad - nrows, Q_BLOCK), NEG_BIG, F32)], axis=0)
            selb_ref[rows[i], :] = bias.T.astype(BF16)

    needed = (n + 1) * nsub * nb
    widths = list(range(LANES, ncp + 1, LANES))
    for w in widths:
        lo = w - LANES
        cond = needed > lo if w == widths[-1] else (needed > lo) & (needed <= w)
        pl.when(cond)(functools.partial(body, w))


def _nsa_select(proj, cosf, sinf, kcvc, nsub=4):
    S = proj.shape[0]
    ncp = kcvc.shape[2]
    nsel = S // SEL_LEN
    qs = min(nsub * Q_BLOCK, S)
    nsub = qs // Q_BLOCK
    assert nsel <= LANES and ncp % LANES == 0 and S % qs == 0
    assert CMP_LEN == 2 * CMP_STRIDE and Q_BLOCK % CMP_STRIDE == 0 and Q_BLOCK // CMP_STRIDE < LANES
    ov = jnp.asarray(_overlap_matrix(ncp, LANES, nsel), BF16)
    ca = jnp.asarray(_cmp_staircase(), BF16)
    qw = NSA_HG * HEAD_DIM
    return pl.pallas_call(
        functools.partial(_nsa_select_kernel, nsub=nsub),
        grid=(NSA_KV, S // qs),
        in_specs=[
            pl.BlockSpec((qs, qw), lambda g, n: (n, OFF_Q // qw + g)),
            pl.BlockSpec((qs, LANES), lambda g, n: (n, 0)),
            pl.BlockSpec((qs, LANES), lambda g, n: (n, 0)),
            pl.BlockSpec((qs, LANES), lambda g, n: (n, OFF_G // LANES + g)),
            pl.BlockSpec((1, 1, ncp, HEAD_DIM), lambda g, n: (0, g, 0, 0)),
            pl.BlockSpec((1, 1, ncp, HEAD_DIM), lambda g, n: (1, g, 0, 0)),
            pl.BlockSpec((ncp, LANES), lambda g, n: (0, 0)),
            pl.BlockSpec((Q_BLOCK, LANES), lambda g, n: (0, 0)),
        ],
        out_specs=[
            pl.BlockSpec((qs, LANES), lambda g, n: (n, g)),
            pl.BlockSpec((qs, qw), lambda g, n: (n, g)),
            pl.BlockSpec((qs, qw), lambda g, n: (n, g)),
        ],
        out_shape=[
            jax.ShapeDtypeStruct((S, NSA_KV * LANES), BF16),
            jax.ShapeDtypeStruct((S, NSA_WIDTH), F32),
            jax.ShapeDtypeStruct((S, NSA_WIDTH), BF16),
        ],
        compiler_params=_cparams(("arbitrary", "arbitrary")),
        name="nsa_select",
    )(proj, cosf, sinf, proj, kcvc, kcvc, ov, ca)


def _nsa_kernel(q_ref, gate_ref, z_ref, selb_ref, pc_ref, wb_ref, cb_ref,
                ks_ref, vs_ref, kw_ref, vw_ref, o_ref, m_ref, accl_ref, sa_ref, sb_ref, part_ref, *, tk):
    n = pl.program_id(1)
    q0 = n * Q_BLOCK
    R = NSA_HG * Q_BLOCK
    q4 = jnp.concatenate([q_ref[:, h * HEAD_DIM:(h + 1) * HEAD_DIM] for h in range(NSA_HG)], axis=0)
    per_head = _per_head

    wspan = WINDOW + Q_BLOCK
    w0 = pl.multiple_of(jnp.maximum(q0 - WINDOW, 0), Q_BLOCK)
    sw = (_halves(_dot_nt, q4, kw_ref[pl.ds(w0, wspan), :])
          + per_head(wb_ref[jnp.minimum(n, WINDOW // Q_BLOCK)]))
    ew = jnp.exp2(sw - jnp.max(sw, axis=-1, keepdims=True))
    vwa = jnp.concatenate([vw_ref[pl.ds(w0, wspan), :], jnp.ones((wspan, LANES), BF16)], axis=1)
    pvw = _halves(_dot, ew.astype(BF16), vwa)
    o_win = pvw[:, :LANES] / pvw[:, LANES:]

    gate = jax.nn.sigmoid(gate_ref[...])
    for h in range(NSA_HG):
        cs = slice(h * HEAD_DIM, (h + 1) * HEAD_DIM)
        part_ref[:, cs] = pc_ref[:, cs] + gate[:, 3 * h + 2:3 * h + 3] * o_win[h * Q_BLOCK:(h + 1) * Q_BLOCK]

    sel_bias = per_head(selb_ref[...])

    m_ref[...] = jnp.full(m_ref.shape, NEG_BIG, F32)
    accl_ref[...] = jnp.zeros(accl_ref.shape, F32)
    qa = jnp.concatenate([q4, sel_bias], axis=1)

    n_q = tk // Q_BLOCK
    last = lax.shift_right_logical(n, n_q.bit_length() - 1)
    diag = n - last * n_q

    def score_tile(kt, dst_ref, width=tk):
        k0 = pl.multiple_of(kt * tk, tk)
        key_blk = lax.broadcasted_iota(jnp.int32, (width, LANES), 0) >> 6
        blk_lane = lax.broadcasted_iota(jnp.int32, (width, LANES), 1)
        onehot = jnp.where(key_blk + kt * (tk // SEL_LEN) == blk_lane, 1.0, 0.0).astype(BF16)
        dst_ref[:, :width] = _dot_nt(qa, jnp.concatenate([ks_ref[pl.ds(k0, width), :], onehot], axis=1))

    def consume_tile(kt, src_ref, last_diag=None):
        width = tk if last_diag is None else (last_diag + 1) * Q_BLOCK
        k0 = pl.multiple_of(kt * tk, tk)
        sc = src_ref[:, :width]
        if last_diag is not None:
            sc = sc + per_head(cb_ref[last_diag, :, :width])
        m_prev = m_ref[...]
        m_next = jnp.maximum(m_prev, jnp.max(sc, axis=-1, keepdims=True))
        pr = jnp.exp2(sc - jnp.concatenate([m_next] * (width // LANES), axis=1))
        alpha = jnp.exp2(m_prev - m_next)
        va = jnp.concatenate([vs_ref[pl.ds(k0, width), :], jnp.ones((width, LANES), BF16)], axis=1)
        accl_ref[...] = jnp.concatenate([alpha, alpha], axis=1) * accl_ref[...] + _dot(pr.astype(BF16), va)
        m_ref[...] = m_next

    score_tile(0, sa_ref)

    def slc_pair(i, carry):
        score_tile(2 * i + 1, sb_ref)
        consume_tile(2 * i, sa_ref)
        score_tile(2 * i + 2, sa_ref)
        consume_tile(2 * i + 1, sb_ref)
        return carry

    lax.fori_loop(0, lax.shift_right_logical(last, 1), slc_pair, 0)

    def tail_even(d):
        consume_tile(last, sa_ref, d)

    def tail_odd(d):
        score_tile(last, sb_ref, (d + 1) * Q_BLOCK)
        consume_tile(last - 1, sa_ref)
        consume_tile(last, sb_ref, d)

    for d in range(n_q):
        pl.when(((last & 1) == 0) & (diag == d))(functools.partial(tail_even, d))
        pl.when(((last & 1) == 1) & (diag == d))(functools.partial(tail_odd, d))

    o_slc = accl_ref[:, :LANES] / accl_ref[:, LANES:]

    gsig = jax.nn.sigmoid(gate_ref[...])
    for h in range(NSA_HG):
        rs = slice(h * Q_BLOCK, (h + 1) * Q_BLOCK)
        cs = slice(h * HEAD_DIM, (h + 1) * HEAD_DIM)
        oh = part_ref[:, cs] + gsig[:, 3 * h + 1:3 * h + 2] * o_slc[rs]
        z = z_ref[:, cs]
        o_ref[:, cs] = (oh * (z * jax.nn.sigmoid(z))).astype(o_ref.dtype)


def _cmp_staircase():
    nb = Q_BLOCK // CMP_STRIDE
    a = np.zeros((Q_BLOCK, LANES), np.float32)
    a[:, 0] = 1.0
    r = np.arange(Q_BLOCK)[:, None]
    i = np.arange(nb)[None, :]
    a[:, 1:1 + nb] = np.where(r < CMP_STRIDE * i + (CMP_STRIDE - 1), NEG_BIG, 0.0)
    return a


def _window_masks():
    n_w = WINDOW // Q_BLOCK
    r = np.arange(Q_BLOCK)[None, :, None]
    c = np.arange(WINDOW + Q_BLOCK)[None, None, :]
    d = np.arange(n_w + 1)[:, None, None]
    ok = np.where(d < n_w, c <= Q_BLOCK * d + r, (c > r) & (c <= WINDOW + r))
    return np.where(ok, 0.0, NEG_BIG).astype(np.float32)


def _causal_staircases(tk):
    n_q = tk // Q_BLOCK
    r = np.arange(Q_BLOCK)[None, :, None]
    c = np.arange(tk)[None, None, :]
    d = np.arange(n_q + 1)[:, None, None]
    return np.where((c <= Q_BLOCK * d + r) | (d == n_q), 0.0, NEG_BIG).astype(np.float32)


def _overlap_matrix(ncp, nsel_pad, nsel):
    ci = np.arange(ncp)[:, None] * CMP_STRIDE
    sj = np.arange(nsel_pad)[None, :] * SEL_LEN
    ov = (ci < sj + SEL_LEN) & (ci + CMP_LEN > sj) & (np.arange(nsel_pad)[None, :] < nsel) & (np.arange(ncp)[:, None] < ncp - 1)
    return ov.astype(np.float32)


def _nsa(proj, qb, selb, pcmp, kvb, tk=512):
    S = proj.shape[0]
    assert S % tk == 0 and S >= WINDOW + Q_BLOCK and WINDOW % Q_BLOCK == 0
    wb = jnp.asarray(_window_masks(), F32)
    n_q = tk // Q_BLOCK
    assert n_q & (n_q - 1) == 0 and tk % SEL_LEN == 0
    cb = jnp.asarray(_causal_staircases(tk), F32)
    qw = NSA_HG * HEAD_DIM
    R = NSA_HG * Q_BLOCK
    kern = functools.partial(_nsa_kernel, tk=tk)
    return pl.pallas_call(
        kern,
        grid=(NSA_KV, S // Q_BLOCK),
        in_specs=[
            pl.BlockSpec((Q_BLOCK, qw), lambda g, n: (n, g)),
            pl.BlockSpec((Q_BLOCK, LANES), lambda g, n: (n, OFF_G // LANES + g)),
            pl.BlockSpec((Q_BLOCK, qw), lambda g, n: (n, OFF_Z // qw + g)),
            pl.BlockSpec((Q_BLOCK, LANES), lambda g, n: (n, g)),
            pl.BlockSpec((Q_BLOCK, qw), lambda g, n: (n, g)),
            pl.BlockSpec((WINDOW // Q_BLOCK + 1, Q_BLOCK, WINDOW + Q_BLOCK), lambda g, n: (0, 0, 0)),
            pl.BlockSpec((n_q + 1, Q_BLOCK, tk), lambda g, n: (0, 0, 0)),
            pl.BlockSpec((S, HEAD_DIM), lambda g, n: (0, 0 * NSA_KV + g)),
            pl.BlockSpec((S, HEAD_DIM), lambda g, n: (0, 1 * NSA_KV + g)),
            pl.BlockSpec((S, HEAD_DIM), lambda g, n: (0, 2 * NSA_KV + g)),
            pl.BlockSpec((S, HEAD_DIM), lambda g, n: (0, 3 * NSA_KV + g)),
        ],
        out_specs=pl.BlockSpec((Q_BLOCK, qw), lambda g, n: (n, g)),
        out_shape=jax.ShapeDtypeStruct((S, NSA_WIDTH), BF16),
        scratch_shapes=[
            pltpu.VMEM((R, LANES), F32),
            pltpu.VMEM((R, HEAD_DIM + LANES), F32),
            pltpu.VMEM((R, tk), F32),
            pltpu.VMEM((R, tk), F32),
            pltpu.VMEM((Q_BLOCK, qw), F32),
        ],
        compiler_params=_cparams(("arbitrary", "arbitrary")),
        name="nsa",
    )(qb, proj, proj, selb, pcmp, wb, cb, kvb, kvb, kvb, kvb)


def _merge_kernel(ohg_ref, onsa_ref, g0_ref, g1_ref, g2_ref, g3_ref, whg_ref, wnsa_ref, o_ref):
    y_hg = _dot(ohg_ref[...], whg_ref[...])
    y_nsa = _dot(onsa_ref[...], wnsa_ref[...])
    half = D_MODEL // 2
    gh = (g0_ref, g1_ref)
    gn = (g2_ref, g3_ref)
    for c in range(2):
        cs = slice(c * half, (c + 1) * half)
        m = jax.nn.sigmoid(gh[c][...]) * y_hg[:, cs] + jax.nn.sigmoid(gn[c][...]) * y_nsa[:, cs]
        o_ref[:, cs] = m.astype(o_ref.dtype)


def _merge(o_hg, o_nsa, proj, w_hg, w_nsa, tm=512):
    S = o_hg.shape[0]
    tm = min(tm, S)
    half = D_MODEL // 2
    gb = OFF_MG // half
    return pl.pallas_call(
        _merge_kernel,
        grid=(S // tm,),
        in_specs=[
            pl.BlockSpec((tm, HG_WIDTH), lambda i: (i, 0)),
            pl.BlockSpec((tm, NSA_WIDTH), lambda i: (i, 0)),
            pl.BlockSpec((tm, half), lambda i: (i, gb + 0)),
            pl.BlockSpec((tm, half), lambda i: (i, gb + 1)),
            pl.BlockSpec((tm, half), lambda i: (i, gb + 2)),
            pl.BlockSpec((tm, half), lambda i: (i, gb + 3)),
            pl.BlockSpec((HG_WIDTH, D_MODEL), lambda i: (0, 0)),
            pl.BlockSpec((NSA_WIDTH, D_MODEL), lambda i: (0, 0)),
        ],
        out_specs=pl.BlockSpec((tm, D_MODEL), lambda i: (i, 0)),
        out_shape=jax.ShapeDtypeStruct((S, D_MODEL), BF16),
        compiler_params=_cparams(("arbitrary",)),
        name="merge",
    )(o_hg, o_nsa, proj, proj, proj, proj, w_hg, w_nsa)


def _out_kernel(x_ref, m_ref, w_ref, nw_ref, o_ref):
    h = x_ref[...] + _dot(m_ref[...], w_ref[...])
    ms = jnp.mean(h * h, axis=-1, keepdims=True)
    o_ref[...] = h * lax.rsqrt(ms + EPS) * nw_ref[...]


def _out(x2, merged, w_out, final_w, tm=512):
    S, D = x2.shape
    tm = min(tm, S)
    return pl.pallas_call(
        _out_kernel,
        grid=(S // tm,),
        in_specs=[
            pl.BlockSpec((tm, D), lambda i: (i, 0)),
            pl.BlockSpec((tm, D), lambda i: (i, 0)),
            pl.BlockSpec((D, D), lambda i: (0, 0)),
            pl.BlockSpec((1, D), lambda i: (0, 0)),
        ],
        out_specs=pl.BlockSpec((tm, D), lambda i: (i, 0)),
        out_shape=jax.ShapeDtypeStruct((S, D), F32),
        compiler_params=_cparams(("arbitrary",)),
        name="out_proj",
    )(x2, merged, w_out, final_w.reshape(1, D))


def _layer(x2, norm_w, w_in, lb_logits, hg_norm_w, cmp_k_pos, cmp_k_w1, cmp_k_b1, cmp_k_w2,
           cmp_v_pos, cmp_v_w1, cmp_v_b1, cmp_v_w2, w_branch_hg, w_branch_nsa, w_out, final_w):
    S = x2.shape[0]
    proj = _norm_proj(x2, norm_w, _w_prep(w_in.T))
    cosf, sinf = _rope_tables(S)
    segs, kvb = _kv_prep(proj, cosf, sinf)
    pe = jnp.stack([cmp_k_pos.reshape(1, -1), cmp_v_pos.reshape(1, -1)])
    w1 = jnp.stack([cmp_k_w1, cmp_v_w1]).astype(BF16)
    b1 = jnp.stack([cmp_k_b1.reshape(1, -1), cmp_v_b1.reshape(1, -1)])
    w2 = jnp.stack([cmp_k_w2, cmp_v_w2]).astype(BF16)
    kcvc = _compress(segs, pe, w1, b1, w2)
    o_hg = _hgrn(proj, lb_logits, hg_norm_w)
    selb, pcmp, qb = _nsa_select(proj, cosf, sinf, kcvc)
    o_nsa = _nsa(proj, qb, selb, pcmp, kvb)
    merged = _merge(o_hg, o_nsa, proj, w_branch_hg.astype(BF16), w_branch_nsa.astype(BF16))
    return _out(x2, merged, w_out.astype(BF16), final_w)


def kernel(x, norm_w, w_in, hg_lb_logits, hg_norm_w, cmp_k_pos, cmp_k_w1, cmp_k_b1, cmp_k_w2, cmp_v_pos, cmp_v_w1, cmp_v_b1, cmp_v_w2, w_branch_hg, w_branch_nsa, w_out, final_norm_w):
    B, S, D = x.shape
    assert B == 1 and D == D_MODEL and norm_w.shape[0] == 1
    y = _layer(x[0], norm_w[0], w_in[0], hg_lb_logits, hg_norm_w[0], cmp_k_pos[0], cmp_k_w1[0], cmp_k_b1[0],
               cmp_k_w2[0], cmp_v_pos[0], cmp_v_w1[0], cmp_v_b1[0], cmp_v_w2[0], w_branch_hg[0],
               w_branch_nsa[0], w_out[0], final_norm_w)
    return y[None]
```

```python
import functools

import numpy as np
import jax
import jax.numpy as jnp
from jax import lax
from jax.experimental import pallas as pl
from jax.experimental.pallas import tpu as pltpu

F32 = jnp.float32
BF16 = jnp.bfloat16

D_MODEL = 2048
HG_HEADS = 8
HG_DK = 128
HG_DV = 128
HG_FDIM = HG_HEADS * HG_DK
HG_WIDTH = HG_HEADS * HG_DV
HG_MM_LEVELS = 3
NSA_HEADS = 16
NSA_KV = 4
NSA_HG = NSA_HEADS // NSA_KV
HEAD_DIM = 128
NSA_WIDTH = NSA_HEADS * HEAD_DIM
NSA_KVW = NSA_KV * HEAD_DIM
CMP_LEN = 32
CMP_STRIDE = 16
CMP_HIDDEN = 512
SEL_LEN = 64
SEL_TOPK = 16
WINDOW = 512
Q_BLOCK = 256
ROPE_THETA = 500000.0
ROT_DIM = HEAD_DIM // 4
ROT_HALF = ROT_DIM // 2
EPS = 1e-6
LOG2E = float(np.log2(np.e))

LANES = 128
NEG_BIG = -1e30
VMEM_LIMIT = 56 * 1024 * 1024

OFF_Q = 4 * HG_FDIM
OFF_KV = OFF_Q + NSA_WIDTH
OFF_MG = OFF_KV + 6 * NSA_KVW
OFF_Z = OFF_MG + 2 * D_MODEL
OFF_G = OFF_Z + NSA_WIDTH
IN_GATE_OFF = OFF_MG
IN_Z_OFF = IN_GATE_OFF + 3 * NSA_HEADS
IN_MG_OFF = IN_Z_OFF + NSA_WIDTH


def _dot(a, b):
    return jnp.dot(a, b, preferred_element_type=F32)


def _dot_nt(a, b):
    return lax.dot_general(a, b, (((1,), (1,)), ((), ())), preferred_element_type=F32)


def _halves(dot, a, b):
    h = a.shape[0] // 2
    return jnp.concatenate([dot(a[:h], b), dot(a[h:], b)], axis=0)


def _split_bf16(a):
    hi = a.astype(BF16)
    lo = (a - hi.astype(F32)).astype(BF16)
    return hi, lo


def _cparams(sem):
    return pltpu.CompilerParams(dimension_semantics=sem, vmem_limit_bytes=VMEM_LIMIT)


def _w_prep_kernel(wt_ref, wg_ref, o_ref, *, n_direct):
    j = pl.program_id(0)

    @pl.when(j < n_direct)
    def _():
        o_ref[...] = wt_ref[...].astype(BF16)

    @pl.when(j >= n_direct)
    def _():
        o_ref[...] = wg_ref[...].astype(BF16)


def _w_prep(wt, tn=512):
    D = wt.shape[1]
    n_main, n_mg, n_z = OFF_MG // tn, 2 * D_MODEL // tn, NSA_WIDTH // tn
    n_direct = n_main + n_mg + n_z
    assert OFF_MG % tn == 0 and D_MODEL % tn == 0 and NSA_KV * LANES == tn
    gates = wt[IN_GATE_OFF:IN_Z_OFF].reshape(NSA_KV, 3 * NSA_HG, D)
    gates = jnp.pad(gates, ((0, 0), (0, LANES - 3 * NSA_HG), (0, 0))).reshape(NSA_KV * LANES, D)
    n_extra = 2
    gates = jnp.pad(gates, ((0, (n_extra - 1) * tn), (0, 0)))

    sub = 8
    assert IN_MG_OFF % sub == 0 and IN_Z_OFF % sub == 0 and tn % sub == 0

    def src_row(j):
        in_mg = IN_MG_OFF // sub + (j - n_main) * (tn // sub)
        in_z = IN_Z_OFF // sub + (jnp.minimum(j, n_direct - 1) - n_main - n_mg) * (tn // sub)
        return jnp.where(j < n_main, j * (tn // sub), jnp.where(j < n_main + n_mg, in_mg, in_z)) * sub

    return pl.pallas_call(
        functools.partial(_w_prep_kernel, n_direct=n_direct),
        grid=(n_direct + n_extra,),
        in_specs=[
            pl.BlockSpec((pl.Element(tn), pl.Element(D)), lambda j: (src_row(j), 0)),
            pl.BlockSpec((tn, D), lambda j: (jnp.maximum(j - n_direct, 0), 0)),
        ],
        out_specs=pl.BlockSpec((tn, D), lambda j: (j, 0)),
        out_shape=jax.ShapeDtypeStruct(((n_direct + n_extra) * tn, D), BF16),
        compiler_params=_cparams(("arbitrary",)),
        name="w_prep",
    )(wt, gates)


def _norm_proj_kernel(x_ref, nw_ref, w_ref, o_ref, xn_ref):
    @pl.when(pl.program_id(1) == 0)
    def _():
        x = x_ref[...]
        ms = jnp.mean(x * x, axis=-1, keepdims=True)
        xn_ref[...] = (x * lax.rsqrt(ms + EPS) * nw_ref[...]).astype(BF16)

    o_ref[...] = _dot_nt(xn_ref[...], w_ref[...])


def _norm_proj(x2, norm_w, wtb, tm=1024, tn=1024):
    S, D = x2.shape
    N = wtb.shape[0]
    tm = min(tm, S)
    return pl.pallas_call(
        _norm_proj_kernel,
        grid=(S // tm, N // tn),
        in_specs=[
            pl.BlockSpec((tm, D), lambda i, j: (i, 0)),
            pl.BlockSpec((1, D), lambda i, j: (0, 0)),
            pl.BlockSpec((tn, D), lambda i, j: (j, 0)),
        ],
        out_specs=pl.BlockSpec((tm, tn), lambda i, j: (i, j)),
        out_shape=jax.ShapeDtypeStruct((S, N), F32),
        scratch_shapes=[pltpu.VMEM((tm, D), BF16)],
        compiler_params=_cparams(("arbitrary", "arbitrary")),
        name="norm_proj",
    )(x2, norm_w.reshape(1, D), wtb)


def _rope(x, cosf, sinf):
    lane = lax.broadcasted_iota(jnp.int32, x.shape, 1)
    rot = jnp.where(lane < ROT_HALF, pltpu.roll(x, LANES - ROT_HALF, 1), pltpu.roll(x, ROT_HALF, 1))
    return x * cosf + rot * sinf


def _rope_tables(S):
    pos = np.arange(S, dtype=np.float64)
    inv = ROPE_THETA ** (-np.arange(0, ROT_DIM, 2, dtype=np.float64) / ROT_DIM)
    ang = pos[:, None] * inv[None, :]
    cos, sin = np.cos(ang), np.sin(ang)
    rest = LANES - ROT_DIM
    cosf = np.concatenate([cos, cos, np.ones((S, rest))], axis=1)
    sinf = np.concatenate([-sin, sin, np.zeros((S, rest))], axis=1)
    return jnp.asarray(cosf, F32), jnp.asarray(sinf, F32)


def _kv_prep_kernel(kv_ref, cos_ref, sin_ref, cmp_ref, kvb_ref, row_ref):
    cosf = cos_ref[...]
    sinf = sin_ref[...]
    W = NSA_KVW
    n_seg = row_ref.shape[0] // CMP_STRIDE

    def to_segments(a, g, rows):
        row_ref[...] = rows
        for l in range(CMP_STRIDE):
            cmp_ref[a, g, :, l * HEAD_DIM:(l + 1) * HEAD_DIM] = row_ref[pl.ds(l, n_seg, stride=CMP_STRIDE), :]

    for g in range(NSA_KV):
        kc = kv_ref[:, 0 * W + g * LANES:0 * W + (g + 1) * LANES]
        to_segments(0, g, _rope(kc, cosf, sinf))
        to_segments(1, g, kv_ref[:, 1 * W + g * LANES:1 * W + (g + 1) * LANES])
        ks = kv_ref[:, 2 * W + g * LANES:2 * W + (g + 1) * LANES]
        kvb_ref[:, 0 * W + g * LANES:0 * W + (g + 1) * LANES] = _rope(ks, cosf, sinf).astype(BF16)
        kvb_ref[:, 1 * W + g * LANES:1 * W + (g + 1) * LANES] = kv_ref[:, 3 * W + g * LANES:3 * W + (g + 1) * LANES].astype(BF16)
        kw = kv_ref[:, 4 * W + g * LANES:4 * W + (g + 1) * LANES]
        kvb_ref[:, 2 * W + g * LANES:2 * W + (g + 1) * LANES] = _rope(kw, cosf, sinf).astype(BF16)
        kvb_ref[:, 3 * W + g * LANES:3 * W + (g + 1) * LANES] = kv_ref[:, 5 * W + g * LANES:5 * W + (g + 1) * LANES].astype(BF16)


def _kv_prep(proj, cosf, sinf, tm=512):
    S = proj.shape[0]
    tm = min(tm, S)
    kvw = 6 * NSA_KVW
    return pl.pallas_call(
        _kv_prep_kernel,
        grid=(S // tm,),
        in_specs=[
            pl.BlockSpec((tm, kvw), lambda i: (i, OFF_KV // kvw)),
            pl.BlockSpec((tm, LANES), lambda i: (i, 0)),
            pl.BlockSpec((tm, LANES), lambda i: (i, 0)),
        ],
        out_specs=[
            pl.BlockSpec((2, NSA_KV, tm // CMP_STRIDE, CMP_STRIDE * HEAD_DIM), lambda i: (0, 0, i, 0)),
            pl.BlockSpec((tm, 4 * NSA_KVW), lambda i: (i, 0)),
        ],
        out_shape=[
            jax.ShapeDtypeStruct((2, NSA_KV, S // CMP_STRIDE, CMP_STRIDE * HEAD_DIM), F32),
            jax.ShapeDtypeStruct((S, 4 * NSA_KVW), BF16),
        ],
        scratch_shapes=[pltpu.VMEM((tm, HEAD_DIM), F32)],
        compiler_params=_cparams(("arbitrary",)),
        name="kv_prep",
    )(proj, cosf, sinf)


def _compress_kernel(seg_ref, pe_ref, w1_ref, b1_ref, w2_ref, o_ref):
    half = (CMP_LEN // 2) * HEAD_DIM
    seg = seg_ref[0, 0]
    n_seg = seg.shape[0]
    pe = pe_ref[0]
    a = (seg + pe[:, :half]).astype(BF16)
    b = (seg + pe[:, half:]).astype(BF16)
    u = _dot(a, w1_ref[0, :half, :])
    v = _dot(b, w1_ref[0, half:, :])
    v_next = pltpu.roll(v, n_seg - 1, 0)
    pre = u + v_next + b1_ref[0]
    h = 0.5 * pre * (1.0 + jnp.tanh(np.sqrt(2.0 / np.pi).astype(np.float32) * (pre + 0.044715 * (pre * pre * pre))))
    o_ref[0, 0] = _dot(h.astype(BF16), w2_ref[0])


def _compress(segs, pe, w1, b1, w2):
    _, G, n_seg, segw = segs.shape
    return pl.pallas_call(
        _compress_kernel,
        grid=(2, G),
        in_specs=[
            pl.BlockSpec((1, 1, n_seg, segw), lambda a, g: (a, g, 0, 0)),
            pl.BlockSpec((1, 1, 2 * segw), lambda a, g: (a, 0, 0)),
            pl.BlockSpec((1, 2 * segw, CMP_HIDDEN), lambda a, g: (a, 0, 0)),
            pl.BlockSpec((1, 1, CMP_HIDDEN), lambda a, g: (a, 0, 0)),
            pl.BlockSpec((1, CMP_HIDDEN, HEAD_DIM), lambda a, g: (a, 0, 0)),
        ],
        out_specs=pl.BlockSpec((1, 1, n_seg, HEAD_DIM), lambda a, g: (a, g, 0, 0)),
        out_shape=jax.ShapeDtypeStruct((2, G, n_seg, HEAD_DIM), F32),
        compiler_params=_cparams(("arbitrary", "arbitrary")),
        name="compress",
    )(segs, pe, w1, b1, w2)


def _hgrn_consts(ch):
    nl = int(np.log2(ch))
    assert 1 << nl == ch
    t = np.arange(ch)[:, None]
    r = np.arange(ch)[None, :]
    mats = [r <= t]
    masks = [np.eye(ch, dtype=bool)]
    for l in range(nl):
        half = 1 << l
        blk = t // (2 * half)
        ref = blk * 2 * half + half - 1
        up = ((t >> l) & 1) == 1
        mats.append(np.where(up, (r > ref) & (r <= t), (r > t) & (r <= ref)))
        masks.append(up & (((r >> l) & 1) == 0) & (blk == r // (2 * half)))
    mc = np.concatenate(mats[:1 + HG_MM_LEVELS], axis=0).astype(np.float32)
    lm = np.stack(masks, axis=0).astype(np.float32)
    return nl, mc, lm


def _hgrn_kernel(q_ref, f_ref, i_ref, z_ref, lbl_ref, nw_ref, mc_ref, lm_ref, o_ref, st_ref, e_ref,
                 *, ch, nl, nch):
    @pl.when(pl.program_id(0) == 0)
    def _():
        st_ref[...] = jnp.zeros_like(st_ref)

    lg = lbl_ref[...]
    ex = jnp.exp(lg - jnp.max(lg, axis=0, keepdims=True))
    lb = ex[0:1] / jnp.sum(ex, axis=0, keepdims=True)
    f = lb + (1.0 - lb) * jax.nn.sigmoid(f_ref[...])
    g_hi, g_lo = _split_bf16(jnp.log(f))
    mc = mc_ref[...]
    chunks = range(nch)
    crow = [slice(c * ch, (c + 1) * ch) for c in chunks]
    for c in chunks:
        e_ref[c] = _dot(mc, g_hi[crow[c]]) + _dot(mc, g_lo[crow[c]])
    row = lax.broadcasted_iota(jnp.int32, (ch, 1), 0)
    nw = nw_ref[...]

    heads = [slice(h * HG_DK, (h + 1) * HG_DK) for h in range(HG_HEADS)]
    qs = [[q_ref[crow[c], sl] for sl in heads] for c in chunks]
    ks = [[1.0 - f[crow[c], sl] for sl in heads] for c in chunks]
    bs = [[e_ref[c, 0:ch, sl] for sl in heads] for c in chunks]
    scs = [[_dot_nt(q.astype(BF16), k.astype(BF16)) * lm_ref[0] for q, k in zip(qs[c], ks[c])] for c in chunks]
    for l in range(nl):
        up = ((row >> l) & 1) == 1
        half = 1 << l
        for c in chunks:
            for h, sl in enumerate(heads):
                if l < HG_MM_LEVELS:
                    el = e_ref[c, (l + 1) * ch:(l + 2) * ch, sl]
                    xl = jnp.where(up, qs[c][h], ks[c][h]) * jnp.exp(el)
                else:
                    parts = []
                    for r0 in range(0, ch, 2 * half):
                        mid = r0 + half
                        edge = bs[c][h][mid - 1:mid, :]
                        parts.append(ks[c][h][r0:mid] * jnp.exp(edge - bs[c][h][r0:mid]))
                        parts.append(qs[c][h][mid:mid + half] * jnp.exp(bs[c][h][mid:mid + half] - edge))
                    xl = jnp.concatenate(parts, axis=0)
                xl = xl.astype(BF16)
                scs[c][h] = scs[c][h] + _dot_nt(xl, xl) * lm_ref[l + 1]

    sts = [st_ref[h] for h in range(HG_HEADS)]
    for c in chunks:
        vbs = [i_ref[crow[c], sl].astype(BF16) for sl in heads]
        inters = [_dot_nt((qs[c][h] * jnp.exp(bs[c][h])).astype(BF16), sts[h].astype(BF16))
                  for h in range(HG_HEADS)]
        for h in range(HG_HEADS):
            last = bs[c][h][ch - 1:ch, :]
            ke = (ks[c][h] * jnp.exp(last - bs[c][h])).astype(BF16)
            upd = lax.dot_general(vbs[h], ke, (((0,), (0,)), ((), ())), preferred_element_type=F32)
            sts[h] = sts[h] * jnp.exp(last) + upd
        outs = [inters[h] + _dot(scs[c][h].astype(BF16), vbs[h]) for h in range(HG_HEADS)]
        for h, sl in enumerate(heads):
            o = outs[h]
            ms = jnp.mean(o * o, axis=-1, keepdims=True)
            z = z_ref[crow[c], sl]
            o_ref[crow[c], sl] = (o * lax.rsqrt(ms + EPS) * nw * (z * jax.nn.sigmoid(z))).astype(o_ref.dtype)
    for h in range(HG_HEADS):
        st_ref[h] = sts[h]


def _hgrn(proj, lb_logits, hg_norm_w, ch=128, nch=4):
    S = proj.shape[0]
    nl, mc, lm = _hgrn_consts(ch)
    kern = functools.partial(_hgrn_kernel, ch=ch, nl=nl, nch=nch)
    nlb = lb_logits.shape[0]
    rows = ch * nch
    assert S % rows == 0
    return pl.pallas_call(
        kern,
        grid=(S // rows,),
        in_specs=[
            pl.BlockSpec((rows, HG_FDIM), lambda c: (c, 0)),
            pl.BlockSpec((rows, HG_FDIM), lambda c: (c, 1)),
            pl.BlockSpec((rows, HG_WIDTH), lambda c: (c, 2)),
            pl.BlockSpec((rows, HG_WIDTH), lambda c: (c, 3)),
            pl.BlockSpec((nlb, HG_FDIM), lambda c: (0, 0)),
            pl.BlockSpec((1, HG_DV), lambda c: (0, 0)),
            pl.BlockSpec(mc.shape, lambda c: (0, 0)),
            pl.BlockSpec(lm.shape, lambda c: (0, 0, 0)),
        ],
        out_specs=pl.BlockSpec((rows, HG_WIDTH), lambda c: (c, 0)),
        out_shape=jax.ShapeDtypeStruct((S, HG_WIDTH), BF16),
        scratch_shapes=[
            pltpu.VMEM((HG_HEADS, HG_DV, HG_DK), F32),
            pltpu.VMEM((nch, mc.shape[0], HG_FDIM), F32),
        ],
        compiler_params=_cparams(("arbitrary",)),
        name="hgrn",
    )(proj, proj, proj, proj, lb_logits, hg_norm_w.reshape(1, HG_DV), jnp.asarray(mc, BF16), jnp.asarray(lm, F32))


def _stack_heads(q_ref, rows, cosf, sinf):
    qscale = (HEAD_DIM ** -0.5) * LOG2E
    return jnp.concatenate(
        [(_rope(q_ref[rows, h * HEAD_DIM:(h + 1) * HEAD_DIM], cosf, sinf) * qscale).astype(BF16)
         for h in range(NSA_HG)], axis=0)


def _per_head(a):
    return jnp.concatenate([a] * NSA_HG, axis=0)


def _nsa_select_kernel(q_ref, cos_ref, sin_ref, gate_ref, kc_ref, vc_ref, ov_ref, ca_ref,
                       selb_ref, pc_ref, qb_ref, *, nsub):
    n = pl.program_id(1)
    R = NSA_HG * Q_BLOCK
    ncp = kc_ref.shape[2]
    nsel_pad = ov_ref.shape[1]
    nb = Q_BLOCK // CMP_STRIDE
    nsb = Q_BLOCK // SEL_LEN
    subs = range(nsub)
    rows = [slice(i * Q_BLOCK, (i + 1) * Q_BLOCK) for i in subs]
    q0s = [(n * nsub + i) * Q_BLOCK for i in subs]

    def body(ncols):
        nrows = ncols // nb * nsb
        crow = lax.broadcasted_iota(jnp.int32, (ncols, LANES), 0)
        clane = lax.broadcasted_iota(jnp.int32, (ncols, LANES), 1)
        kc = kc_ref[0, 0, :ncols, :].astype(BF16)
        vca = jnp.concatenate([vc_ref[0, 0, :ncols, :].astype(BF16), jnp.ones((ncols, LANES), BF16)], axis=1)
        ca4 = _per_head(ca_ref[...])
        ov = ov_ref[:ncols, :]

        ss = []
        for i in subs:
            q4 = _stack_heads(q_ref, rows[i], cos_ref[rows[i], :], sin_ref[rows[i], :])
            for h in range(NSA_HG):
                qb_ref[rows[i], h * HEAD_DIM:(h + 1) * HEAD_DIM] = q4[h * Q_BLOCK:(h + 1) * Q_BLOCK]
            c_first = (n * nsub + i) * nb - (CMP_LEN // CMP_STRIDE - 1)
            flags = jnp.where(clane == 0, jnp.where(crow >= c_first + nb, NEG_BIG, 0.0),
                              jnp.where(crow == c_first + clane - 1, 1.0, 0.0)).astype(BF16)
            ss.append(_halves(_dot_nt, jnp.concatenate([q4, ca4], axis=1), jnp.concatenate([kc, flags], axis=1)))
        ebs = [jnp.exp2(s - jnp.max(s, axis=-1, keepdims=True)).astype(BF16) for s in ss]
        pvs = [_halves(_dot, e, vca) for e in ebs]

        imps = []
        for i in subs:
            t_row = q0s[i] + (lax.broadcasted_iota(jnp.int32, (R, LANES), 0) & (Q_BLOCK - 1))
            inv = jnp.where(t_row >= CMP_LEN - 1, 1.0, 0.0) / pvs[i][:, LANES:]
            o_cmp = pvs[i][:, :LANES] * inv
            gate = jax.nn.sigmoid(gate_ref[rows[i], :])
            for h in range(NSA_HG):
                pc_ref[rows[i], h * HEAD_DIM:(h + 1) * HEAD_DIM] = (
                    gate[:, 3 * h:3 * h + 1] * o_cmp[h * Q_BLOCK:(h + 1) * Q_BLOCK])
            eo = _dot(ebs[i], ov) * inv
            imp = eo[0:Q_BLOCK]
            for h in range(1, NSA_HG):
                imp = imp + eo[h * Q_BLOCK:(h + 1) * Q_BLOCK]
            imps.append(imp)

        jj = lax.broadcasted_iota(jnp.int32, (nrows, Q_BLOCK), 0)
        jjf = jj.astype(F32)
        lane_q = lax.broadcasted_iota(jnp.int32, (nrows, Q_BLOCK), 1)
        TAKEN = -2.0
        scores = []
        for i in subs:
            tq = q0s[i] + lane_q
            jt = tq >> 6
            forced = (jj == 0) | (jj == jt) | (jj == jt - 1)
            scores.append(jnp.where(jj * SEL_LEN <= tq, jnp.where(forced, TAKEN, imps[i].T[:nrows]), -1.0))
        for _ in range(SEL_TOPK - 3):
            for i in subs:
                mx = jnp.max(scores[i], axis=0, keepdims=True)
                first = jnp.min(jnp.where(scores[i] == mx, jjf, float(nsel_pad)), axis=0, keepdims=True)
                first = jnp.where(mx >= 0.0, first, -1.0)
                scores[i] = jnp.where(jjf == first, TAKEN, scores[i])
        for i in subs:
            bias = jnp.where(scores[i] == TAKEN, 0.0, NEG_BIG)
            if nrows < nsel_pad:
                bias = jnp.concatenate([bias, jnp.full((nsel_pad - nrows, Q_BLOCK), NEG_BIG, F32)], axis=0)
            selb_ref[rows[i], :] = bias.T.astype(BF16)

    needed = (n + 1) * nsub * nb
    widths = list(range(LANES, ncp + 1, LANES))
    for w in widths:
        lo = w - LANES
        cond = needed > lo if w == widths[-1] else (needed > lo) & (needed <= w)
        pl.when(cond)(functools.partial(body, w))


def _nsa_select(proj, cosf, sinf, kcvc, nsub=4):
    S = proj.shape[0]
    ncp = kcvc.shape[2]
    nsel = S // SEL_LEN
    qs = min(nsub * Q_BLOCK, S)
    nsub = qs // Q_BLOCK
    assert nsel <= LANES and ncp % LANES == 0 and S % qs == 0
    assert CMP_LEN == 2 * CMP_STRIDE and Q_BLOCK % CMP_STRIDE == 0 and Q_BLOCK // CMP_STRIDE < LANES
    ov = jnp.asarray(_overlap_matrix(ncp, LANES, nsel), BF16)
    ca = jnp.asarray(_cmp_staircase(), BF16)
    qw = NSA_HG * HEAD_DIM
    return pl.pallas_call(
        functools.partial(_nsa_select_kernel, nsub=nsub),
        grid=(NSA_KV, S // qs),
        in_specs=[
            pl.BlockSpec((qs, qw), lambda g, n: (n, OFF_Q // qw + g)),
            pl.BlockSpec((qs, LANES), lambda g, n: (n, 0)),
            pl.BlockSpec((qs, LANES), lambda g, n: (n, 0)),
            pl.BlockSpec((qs, LANES), lambda g, n: (n, OFF_G // LANES + g)),
            pl.BlockSpec((1, 1, ncp, HEAD_DIM), lambda g, n: (0, g, 0, 0)),
            pl.BlockSpec((1, 1, ncp, HEAD_DIM), lambda g, n: (1, g, 0, 0)),
            pl.BlockSpec((ncp, LANES), lambda g, n: (0, 0)),
            pl.BlockSpec((Q_BLOCK, LANES), lambda g, n: (0, 0)),
        ],
        out_specs=[
            pl.BlockSpec((qs, LANES), lambda g, n: (n, g)),
            pl.BlockSpec((qs, qw), lambda g, n: (n, g)),
            pl.BlockSpec((qs, qw), lambda g, n: (n, g)),
        ],
        out_shape=[
            jax.ShapeDtypeStruct((S, NSA_KV * LANES), BF16),
            jax.ShapeDtypeStruct((S, NSA_WIDTH), F32),
            jax.ShapeDtypeStruct((S, NSA_WIDTH), BF16),
        ],
        compiler_params=_cparams(("arbitrary", "arbitrary")),
        name="nsa_select",
    )(proj, cosf, sinf, proj, kcvc, kcvc, ov, ca)


def _nsa_kernel(q_ref, gate_ref, z_ref, selb_ref, pc_ref, wb_ref, cb_ref,
                ks_ref, vs_ref, kw_ref, vw_ref, o_ref, m_ref, accl_ref, sa_ref, sb_ref, part_ref, *, tk):
    n = pl.program_id(1)
    q0 = n * Q_BLOCK
    R = NSA_HG * Q_BLOCK
    q4 = jnp.concatenate([q_ref[:, h * HEAD_DIM:(h + 1) * HEAD_DIM] for h in range(NSA_HG)], axis=0)
    per_head = _per_head

    wspan = WINDOW + Q_BLOCK
    w0 = pl.multiple_of(jnp.maximum(q0 - WINDOW, 0), Q_BLOCK)
    sw = (_halves(_dot_nt, q4, kw_ref[pl.ds(w0, wspan), :])
          + per_head(wb_ref[jnp.minimum(n, WINDOW // Q_BLOCK)]))
    ew = jnp.exp2(sw - jnp.max(sw, axis=-1, keepdims=True))
    vwa = jnp.concatenate([vw_ref[pl.ds(w0, wspan), :], jnp.ones((wspan, LANES), BF16)], axis=1)
    pvw = _halves(_dot, ew.astype(BF16), vwa)
    o_win = pvw[:, :LANES] / pvw[:, LANES:]

    gate = jax.nn.sigmoid(gate_ref[...])
    for h in range(NSA_HG):
        cs = slice(h * HEAD_DIM, (h + 1) * HEAD_DIM)
        part_ref[:, cs] = pc_ref[:, cs] + gate[:, 3 * h + 2:3 * h + 3] * o_win[h * Q_BLOCK:(h + 1) * Q_BLOCK]

    sel_bias = per_head(selb_ref[...])

    m_ref[...] = jnp.full(m_ref.shape, NEG_BIG, F32)
    accl_ref[...] = jnp.zeros(accl_ref.shape, F32)
    qa = jnp.concatenate([q4, sel_bias], axis=1)

    n_q = tk // Q_BLOCK
    last = lax.shift_right_logical(n, n_q.bit_length() - 1)
    diag = n - last * n_q

    def score_tile(kt, dst_ref, width=tk):
        k0 = pl.multiple_of(kt * tk, tk)
        key_blk = lax.broadcasted_iota(jnp.int32, (width, LANES), 0) >> 6
        blk_lane = lax.broadcasted_iota(jnp.int32, (width, LANES), 1)
        onehot = jnp.where(key_blk + kt * (tk // SEL_LEN) == blk_lane, 1.0, 0.0).astype(BF16)
        dst_ref[:, :width] = _dot_nt(qa, jnp.concatenate([ks_ref[pl.ds(k0, width), :], onehot], axis=1))

    def consume_tile(kt, src_ref, last_diag=None):
        width = tk if last_diag is None else (last_diag + 1) * Q_BLOCK
        k0 = pl.multiple_of(kt * tk, tk)
        sc = src_ref[:, :width]
        if last_diag is not None:
            sc = sc + per_head(cb_ref[last_diag, :, :width])
        m_prev = m_ref[...]
        m_next = jnp.maximum(m_prev, jnp.max(sc, axis=-1, keepdims=True))
        pr = jnp.exp2(sc - jnp.concatenate([m_next] * (width // LANES), axis=1))
        alpha = jnp.exp2(m_prev - m_next)
        va = jnp.concatenate([vs_ref[pl.ds(k0, width), :], jnp.ones((width, LANES), BF16)], axis=1)
        accl_ref[...] = jnp.concatenate([alpha, alpha], axis=1) * accl_ref[...] + _dot(pr.astype(BF16), va)
        m_ref[...] = m_next

    score_tile(0, sa_ref)

    def slc_pair(i, carry):
        score_tile(2 * i + 1, sb_ref)
        consume_tile(2 * i, sa_ref)
        score_tile(2 * i + 2, sa_ref)
        consume_tile(2 * i + 1, sb_ref)
        return carry

    lax.fori_loop(0, lax.shift_right_logical(last, 1), slc_pair, 0)

    def tail_even(d):
        consume_tile(last, sa_ref, d)

    def tail_odd(d):
        score_tile(last, sb_ref, (d + 1) * Q_BLOCK)
        consume_tile(last - 1, sa_ref)
        consume_tile(last, sb_ref, d)

    for d in range(n_q):
        pl.when(((last & 1) == 0) & (diag == d))(functools.partial(tail_even, d))
        pl.when(((last & 1) == 1) & (diag == d))(functools.partial(tail_odd, d))

    o_slc = accl_ref[:, :LANES] / accl_ref[:, LANES:]

    gsig = jax.nn.sigmoid(gate_ref[...])
    for h in range(NSA_HG):
        rs = slice(h * Q_BLOCK, (h + 1) * Q_BLOCK)
        cs = slice(h * HEAD_DIM, (h + 1) * HEAD_DIM)
        oh = part_ref[:, cs] + gsig[:, 3 * h + 1:3 * h + 2] * o_slc[rs]
        z = z_ref[:, cs]
        o_ref[:, cs] = (oh * (z * jax.nn.sigmoid(z))).astype(o_ref.dtype)


def _cmp_staircase():
    nb = Q_BLOCK // CMP_STRIDE
    a = np.zeros((Q_BLOCK, LANES), np.float32)
    a[:, 0] = 1.0
    r = np.arange(Q_BLOCK)[:, None]
    i = np.arange(nb)[None, :]
    a[:, 1:1 + nb] = np.where(r < CMP_STRIDE * i + (CMP_STRIDE - 1), NEG_BIG, 0.0)
    return a


def _window_masks():
    n_w = WINDOW // Q_BLOCK
    r = np.arange(Q_BLOCK)[None, :, None]
    c = np.arange(WINDOW + Q_BLOCK)[None, None, :]
    d = np.arange(n_w + 1)[:, None, None]
    ok = np.where(d < n_w, c <= Q_BLOCK * d + r, (c > r) & (c <= WINDOW + r))
    return np.where(ok, 0.0, NEG_BIG).astype(np.float32)


def _causal_staircases(tk):
    n_q = tk // Q_BLOCK
    r = np.arange(Q_BLOCK)[None, :, None]
    c = np.arange(tk)[None, None, :]
    d = np.arange(n_q)[:, None, None]
    return np.where(c <= Q_BLOCK * d + r, 0.0, NEG_BIG).astype(np.float32)


def _overlap_matrix(ncp, nsel_pad, nsel):
    ci = np.arange(ncp)[:, None] * CMP_STRIDE
    sj = np.arange(nsel_pad)[None, :] * SEL_LEN
    ov = (ci < sj + SEL_LEN) & (ci + CMP_LEN > sj) & (np.arange(nsel_pad)[None, :] < nsel) & (np.arange(ncp)[:, None] < ncp - 1)
    return ov.astype(np.float32)


def _nsa(proj, qb, selb, pcmp, kvb, tk=512):
    S = proj.shape[0]
    assert S % tk == 0 and S >= WINDOW + Q_BLOCK and WINDOW % Q_BLOCK == 0
    wb = jnp.asarray(_window_masks(), F32)
    n_q = tk // Q_BLOCK
    assert n_q & (n_q - 1) == 0 and tk % SEL_LEN == 0
    cb = jnp.asarray(_causal_staircases(tk), F32)
    qw = NSA_HG * HEAD_DIM
    R = NSA_HG * Q_BLOCK
    kern = functools.partial(_nsa_kernel, tk=tk)
    return pl.pallas_call(
        kern,
        grid=(NSA_KV, S // Q_BLOCK),
        in_specs=[
            pl.BlockSpec((Q_BLOCK, qw), lambda g, n: (n, g)),
            pl.BlockSpec((Q_BLOCK, LANES), lambda g, n: (n, OFF_G // LANES + g)),
            pl.BlockSpec((Q_BLOCK, qw), lambda g, n: (n, OFF_Z // qw + g)),
            pl.BlockSpec((Q_BLOCK, LANES), lambda g, n: (n, g)),
            pl.BlockSpec((Q_BLOCK, qw), lambda g, n: (n, g)),
            pl.BlockSpec((WINDOW // Q_BLOCK + 1, Q_BLOCK, WINDOW + Q_BLOCK), lambda g, n: (0, 0, 0)),
            pl.BlockSpec((n_q, Q_BLOCK, tk), lambda g, n: (0, 0, 0)),
            pl.BlockSpec((S, HEAD_DIM), lambda g, n: (0, 0 * NSA_KV + g)),
            pl.BlockSpec((S, HEAD_DIM), lambda g, n: (0, 1 * NSA_KV + g)),
            pl.BlockSpec((S, HEAD_DIM), lambda g, n: (0, 2 * NSA_KV + g)),
            pl.BlockSpec((S, HEAD_DIM), lambda g, n: (0, 3 * NSA_KV + g)),
        ],
        out_specs=pl.BlockSpec((Q_BLOCK, qw), lambda g, n: (n, g)),
        out_shape=jax.ShapeDtypeStruct((S, NSA_WIDTH), BF16),
        scratch_shapes=[
            pltpu.VMEM((R, LANES), F32),
            pltpu.VMEM((R, HEAD_DIM + LANES), F32),
            pltpu.VMEM((R, tk), F32),
            pltpu.VMEM((R, tk), F32),
            pltpu.VMEM((Q_BLOCK, qw), F32),
        ],
        compiler_params=_cparams(("arbitrary", "arbitrary")),
        name="nsa",
    )(qb, proj, proj, selb, pcmp, wb, cb, kvb, kvb, kvb, kvb)


def _merge_kernel(ohg_ref, onsa_ref, g0_ref, g1_ref, g2_ref, g3_ref, whg_ref, wnsa_ref, o_ref):
    y_hg = _dot(ohg_ref[...], whg_ref[...])
    y_nsa = _dot(onsa_ref[...], wnsa_ref[...])
    half = D_MODEL // 2
    gh = (g0_ref, g1_ref)
    gn = (g2_ref, g3_ref)
    for c in range(2):
        cs = slice(c * half, (c + 1) * half)
        m = jax.nn.sigmoid(gh[c][...]) * y_hg[:, cs] + jax.nn.sigmoid(gn[c][...]) * y_nsa[:, cs]
        o_ref[:, cs] = m.astype(o_ref.dtype)


def _merge(o_hg, o_nsa, proj, w_hg, w_nsa, tm=512):
    S = o_hg.shape[0]
    tm = min(tm, S)
    half = D_MODEL // 2
    gb = OFF_MG // half
    return pl.pallas_call(
        _merge_kernel,
        grid=(S // tm,),
        in_specs=[
            pl.BlockSpec((tm, HG_WIDTH), lambda i: (i, 0)),
            pl.BlockSpec((tm, NSA_WIDTH), lambda i: (i, 0)),
            pl.BlockSpec((tm, half), lambda i: (i, gb + 0)),
            pl.BlockSpec((tm, half), lambda i: (i, gb + 1)),
            pl.BlockSpec((tm, half), lambda i: (i, gb + 2)),
            pl.BlockSpec((tm, half), lambda i: (i, gb + 3)),
            pl.BlockSpec((HG_WIDTH, D_MODEL), lambda i: (0, 0)),
            pl.BlockSpec((NSA_WIDTH, D_MODEL), lambda i: (0, 0)),
        ],
        out_specs=pl.BlockSpec((tm, D_MODEL), lambda i: (i, 0)),
        out_shape=jax.ShapeDtypeStruct((S, D_MODEL), BF16),
        compiler_params=_cparams(("arbitrary",)),
        name="merge",
    )(o_hg, o_nsa, proj, proj, proj, proj, w_hg, w_nsa)


def _out_kernel(x_ref, m_ref, w_ref, nw_ref, o_ref):
    h = x_ref[...] + _dot(m_ref[...], w_ref[...])
    ms = jnp.mean(h * h, axis=-1, keepdims=True)
    o_ref[...] = h * lax.rsqrt(ms + EPS) * nw_ref[...]


def _out(x2, merged, w_out, final_w, tm=512):
    S, D = x2.shape
    tm = min(tm, S)
    return pl.pallas_call(
        _out_kernel,
        grid=(S // tm,),
        in_specs=[
            pl.BlockSpec((tm, D), lambda i: (i, 0)),
            pl.BlockSpec((tm, D), lambda i: (i, 0)),
            pl.BlockSpec((D, D), lambda i: (0, 0)),
            pl.BlockSpec((1, D), lambda i: (0, 0)),
        ],
        out_specs=pl.BlockSpec((tm, D), lambda i: (i, 0)),
        out_shape=jax.ShapeDtypeStruct((S, D), F32),
        compiler_params=_cparams(("arbitrary",)),
        name="out_proj",
    )(x2, merged, w_out, final_w.reshape(1, D))


def _layer(x2, norm_w, w_in, lb_logits, hg_norm_w, cmp_k_pos, cmp_k_w1, cmp_k_b1, cmp_k_w2,
           cmp_v_pos, cmp_v_w1, cmp_v_b1, cmp_v_w2, w_branch_hg, w_branch_nsa, w_out, final_w):
    S = x2.shape[0]
    proj = _norm_proj(x2, norm_w, _w_prep(w_in.T))
    cosf, sinf = _rope_tables(S)
    segs, kvb = _kv_prep(proj, cosf, sinf)
    pe = jnp.stack([cmp_k_pos.reshape(1, -1), cmp_v_pos.reshape(1, -1)])
    w1 = jnp.stack([cmp_k_w1, cmp_v_w1]).astype(BF16)
    b1 = jnp.stack([cmp_k_b1.reshape(1, -1), cmp_v_b1.reshape(1, -1)])
    w2 = jnp.stack([cmp_k_w2, cmp_v_w2]).astype(BF16)
    kcvc = _compress(segs, pe, w1, b1, w2)
    o_hg = _hgrn(proj, lb_logits, hg_norm_w)
    selb, pcmp, qb = _nsa_select(proj, cosf, sinf, kcvc)
    o_nsa = _nsa(proj, qb, selb, pcmp, kvb)
    merged = _merge(o_hg, o_nsa, proj, w_branch_hg.astype(BF16), w_branch_nsa.astype(BF16))
    return _out(x2, merged, w_out.astype(BF16), final_w)


def kernel(x, norm_w, w_in, hg_lb_logits, hg_norm_w, cmp_k_pos, cmp_k_w1, cmp_k_b1, cmp_k_w2, cmp_v_pos, cmp_v_w1, cmp_v_b1, cmp_v_w2, w_branch_hg, w_branch_nsa, w_out, final_norm_w):
    B, S, D = x.shape
    assert B == 1 and D == D_MODEL and norm_w.shape[0] == 1
    y = _layer(x[0], norm_w[0], w_in[0], hg_lb_logits, hg_norm_w[0], cmp_k_pos[0], cmp_k_w1[0], cmp_k_b1[0],
               cmp_k_w2[0], cmp_v_pos[0], cmp_v_w1[0], cmp_v_b1[0], cmp_v_w2[0], w_branch_hg[0],
               w_branch_nsa[0], w_out[0], final_norm_w)
    return y[None]
```

```python
import functools

import numpy as np
import jax
import jax.numpy as jnp
from jax import lax
from jax.experimental import pallas as pl
from jax.experimental.pallas import tpu as pltpu

F32 = jnp.float32
BF16 = jnp.bfloat16

D_MODEL = 2048
HG_HEADS = 8
HG_DK = 128
HG_DV = 128
HG_FDIM = HG_HEADS * HG_DK
HG_WIDTH = HG_HEADS * HG_DV
HG_MM_LEVELS = 3
NSA_HEADS = 16
NSA_KV = 4
NSA_HG = NSA_HEADS // NSA_KV
HEAD_DIM = 128
NSA_WIDTH = NSA_HEADS * HEAD_DIM
NSA_KVW = NSA_KV * HEAD_DIM
CMP_LEN = 32
CMP_STRIDE = 16
CMP_HIDDEN = 512
SEL_LEN = 64
SEL_TOPK = 16
WINDOW = 512
Q_BLOCK = 256
ROPE_THETA = 500000.0
ROT_DIM = HEAD_DIM // 4
ROT_HALF = ROT_DIM // 2
EPS = 1e-6
LOG2E = float(np.log2(np.e))

LANES = 128
NEG_BIG = -1e30
VMEM_LIMIT = 56 * 1024 * 1024

OFF_Q = 4 * HG_FDIM
OFF_KV = OFF_Q + NSA_WIDTH
OFF_MG = OFF_KV + 6 * NSA_KVW
OFF_Z = OFF_MG + 2 * D_MODEL
OFF_G = OFF_Z + NSA_WIDTH
IN_GATE_OFF = OFF_MG
IN_Z_OFF = IN_GATE_OFF + 3 * NSA_HEADS
IN_MG_OFF = IN_Z_OFF + NSA_WIDTH


def _dot(a, b):
    return jnp.dot(a, b, preferred_element_type=F32)


def _dot_nt(a, b):
    return lax.dot_general(a, b, (((1,), (1,)), ((), ())), preferred_element_type=F32)


def _halves(dot, a, b):
    h = a.shape[0] // 2
    return jnp.concatenate([dot(a[:h], b), dot(a[h:], b)], axis=0)


def _split_bf16(a):
    hi = a.astype(BF16)
    lo = (a - hi.astype(F32)).astype(BF16)
    return hi, lo


def _cparams(sem):
    return pltpu.CompilerParams(dimension_semantics=sem, vmem_limit_bytes=VMEM_LIMIT)


ROPE_PIECES = (0, 2, 3, 4, 1, 5, 6, 7)
ROPE_SRC = np.concatenate([np.arange(p * ROT_HALF, (p + 1) * ROT_HALF) for p in ROPE_PIECES])


def _w_prep_kernel(wt_ref, wg_ref, o_ref, *, n_direct, rope_blocks):
    j = pl.program_id(0)
    is_rope = functools.reduce(jnp.logical_or, [j == b for b in rope_blocks])

    @pl.when((j < n_direct) & jnp.logical_not(is_rope))
    def _():
        o_ref[...] = wt_ref[...].astype(BF16)

    @pl.when(is_rope)
    def _():
        x = wt_ref[...]
        pieces = [x[h * HEAD_DIM + p * ROT_HALF:h * HEAD_DIM + (p + 1) * ROT_HALF]
                  for h in range(x.shape[0] // HEAD_DIM) for p in ROPE_PIECES]
        o_ref[...] = jnp.concatenate(pieces, axis=0).astype(BF16)

    @pl.when(j >= n_direct)
    def _():
        o_ref[...] = wg_ref[...].astype(BF16)


def _w_prep(wt, tn=512):
    D = wt.shape[1]
    n_main, n_mg, n_z = OFF_MG // tn, 2 * D_MODEL // tn, NSA_WIDTH // tn
    n_direct = n_main + n_mg + n_z
    assert OFF_MG % tn == 0 and D_MODEL % tn == 0 and NSA_KV * LANES == tn
    gates = wt[IN_GATE_OFF:IN_Z_OFF].reshape(NSA_KV, 3 * NSA_HG, D)
    gates = jnp.pad(gates, ((0, 0), (0, LANES - 3 * NSA_HG), (0, 0))).reshape(NSA_KV * LANES, D)
    n_extra = 2
    gates = jnp.pad(gates, ((0, (n_extra - 1) * tn), (0, 0)))

    sub = 8
    assert IN_MG_OFF % sub == 0 and IN_Z_OFF % sub == 0 and tn % sub == 0

    def src_row(j):
        in_mg = IN_MG_OFF // sub + (j - n_main) * (tn // sub)
        in_z = IN_Z_OFF // sub + (jnp.minimum(j, n_direct - 1) - n_main - n_mg) * (tn // sub)
        return jnp.where(j < n_main, j * (tn // sub), jnp.where(j < n_main + n_mg, in_mg, in_z)) * sub

    assert tn == NSA_KVW and ROT_HALF * len(ROPE_PIECES) == HEAD_DIM
    rope_blocks = tuple(range(OFF_Q // tn, OFF_KV // tn)) + tuple(OFF_KV // tn + 2 * i for i in range(3))
    return pl.pallas_call(
        functools.partial(_w_prep_kernel, n_direct=n_direct, rope_blocks=rope_blocks),
        grid=(n_direct + n_extra,),
        in_specs=[
            pl.BlockSpec((pl.Element(tn), pl.Element(D)), lambda j: (src_row(j), 0)),
            pl.BlockSpec((tn, D), lambda j: (jnp.maximum(j - n_direct, 0), 0)),
        ],
        out_specs=pl.BlockSpec((tn, D), lambda j: (j, 0)),
        out_shape=jax.ShapeDtypeStruct(((n_direct + n_extra) * tn, D), BF16),
        compiler_params=_cparams(("arbitrary",)),
        name="w_prep",
    )(wt, gates)


def _norm_proj_kernel(x_ref, nw_ref, w_ref, o_ref, xn_ref):
    @pl.when(pl.program_id(1) == 0)
    def _():
        x = x_ref[...]
        ms = jnp.mean(x * x, axis=-1, keepdims=True)
        xn_ref[...] = (x * lax.rsqrt(ms + EPS) * nw_ref[...]).astype(BF16)

    o_ref[...] = _dot_nt(xn_ref[...], w_ref[...])


def _norm_proj(x2, norm_w, wtb, tm=1024, tn=1024):
    S, D = x2.shape
    N = wtb.shape[0]
    tm = min(tm, S)
    return pl.pallas_call(
        _norm_proj_kernel,
        grid=(S // tm, N // tn),
        in_specs=[
            pl.BlockSpec((tm, D), lambda i, j: (i, 0)),
            pl.BlockSpec((1, D), lambda i, j: (0, 0)),
            pl.BlockSpec((tn, D), lambda i, j: (j, 0)),
        ],
        out_specs=pl.BlockSpec((tm, tn), lambda i, j: (i, j)),
        out_shape=jax.ShapeDtypeStruct((S, N), F32),
        scratch_shapes=[pltpu.VMEM((tm, D), BF16)],
        compiler_params=_cparams(("arbitrary", "arbitrary")),
        name="norm_proj",
    )(x2, norm_w.reshape(1, D), wtb)


def _rope(x, cosf, sinf):
    return x * cosf + pltpu.roll(x, HEAD_DIM // 2, 1) * sinf


def _rope_tables(S):
    pos = np.arange(S, dtype=np.float64)
    inv = ROPE_THETA ** (-np.arange(0, ROT_DIM, 2, dtype=np.float64) / ROT_DIM)
    ang = pos[:, None] * inv[None, :]
    cos, sin = np.cos(ang), np.sin(ang)
    rest = LANES - ROT_DIM
    cosf = np.concatenate([cos, cos, np.ones((S, rest))], axis=1)[:, ROPE_SRC]
    sinf = np.concatenate([-sin, sin, np.zeros((S, rest))], axis=1)[:, ROPE_SRC]
    return jnp.asarray(cosf, F32), jnp.asarray(sinf, F32)


def _kv_prep_kernel(kv_ref, cos_ref, sin_ref, cmp_ref, kvb_ref, row_ref):
    cosf = cos_ref[...]
    sinf = sin_ref[...]
    W = NSA_KVW
    n_seg = row_ref.shape[0] // CMP_STRIDE

    def to_segments(a, g, rows):
        row_ref[...] = rows
        for l in range(CMP_STRIDE):
            cmp_ref[a, g, :, l * HEAD_DIM:(l + 1) * HEAD_DIM] = row_ref[pl.ds(l, n_seg, stride=CMP_STRIDE), :]

    for g in range(NSA_KV):
        kc = kv_ref[:, 0 * W + g * LANES:0 * W + (g + 1) * LANES]
        to_segments(0, g, _rope(kc, cosf, sinf))
        to_segments(1, g, kv_ref[:, 1 * W + g * LANES:1 * W + (g + 1) * LANES])
        ks = kv_ref[:, 2 * W + g * LANES:2 * W + (g + 1) * LANES]
        kvb_ref[:, 0 * W + g * LANES:0 * W + (g + 1) * LANES] = _rope(ks, cosf, sinf).astype(BF16)
        kvb_ref[:, 1 * W + g * LANES:1 * W + (g + 1) * LANES] = kv_ref[:, 3 * W + g * LANES:3 * W + (g + 1) * LANES].astype(BF16)
        kw = kv_ref[:, 4 * W + g * LANES:4 * W + (g + 1) * LANES]
        kvb_ref[:, 2 * W + g * LANES:2 * W + (g + 1) * LANES] = _rope(kw, cosf, sinf).astype(BF16)
        kvb_ref[:, 3 * W + g * LANES:3 * W + (g + 1) * LANES] = kv_ref[:, 5 * W + g * LANES:5 * W + (g + 1) * LANES].astype(BF16)


def _kv_prep(proj, cosf, sinf, tm=512):
    S = proj.shape[0]
    tm = min(tm, S)
    kvw = 6 * NSA_KVW
    return pl.pallas_call(
        _kv_prep_kernel,
        grid=(S // tm,),
        in_specs=[
            pl.BlockSpec((tm, kvw), lambda i: (i, OFF_KV // kvw)),
            pl.BlockSpec((tm, LANES), lambda i: (i, 0)),
            pl.BlockSpec((tm, LANES), lambda i: (i, 0)),
        ],
        out_specs=[
            pl.BlockSpec((2, NSA_KV, tm // CMP_STRIDE, CMP_STRIDE * HEAD_DIM), lambda i: (0, 0, i, 0)),
            pl.BlockSpec((tm, 4 * NSA_KVW), lambda i: (i, 0)),
        ],
        out_shape=[
            jax.ShapeDtypeStruct((2, NSA_KV, S // CMP_STRIDE, CMP_STRIDE * HEAD_DIM), F32),
            jax.ShapeDtypeStruct((S, 4 * NSA_KVW), BF16),
        ],
        scratch_shapes=[pltpu.VMEM((tm, HEAD_DIM), F32)],
        compiler_params=_cparams(("arbitrary",)),
        name="kv_prep",
    )(proj, cosf, sinf)


def _compress_kernel(seg_ref, pe_ref, w1_ref, b1_ref, w2_ref, o_ref):
    half = (CMP_LEN // 2) * HEAD_DIM
    seg = seg_ref[0, 0]
    n_seg = seg.shape[0]
    pe = pe_ref[0]
    a = (seg + pe[:, :half]).astype(BF16)
    b = (seg + pe[:, half:]).astype(BF16)
    u = _dot(a, w1_ref[0, :half, :])
    v = _dot(b, w1_ref[0, half:, :])
    v_next = pltpu.roll(v, n_seg - 1, 0)
    pre = u + v_next + b1_ref[0]
    h = 0.5 * pre * (1.0 + jnp.tanh(np.sqrt(2.0 / np.pi).astype(np.float32) * (pre + 0.044715 * (pre * pre * pre))))
    o_ref[0, 0] = _dot(h.astype(BF16), w2_ref[0])


def _compress(segs, pe, w1, b1, w2):
    _, G, n_seg, segw = segs.shape
    return pl.pallas_call(
        _compress_kernel,
        grid=(2, G),
        in_specs=[
            pl.BlockSpec((1, 1, n_seg, segw), lambda a, g: (a, g, 0, 0)),
            pl.BlockSpec((1, 1, 2 * segw), lambda a, g: (a, 0, 0)),
            pl.BlockSpec((1, 2 * segw, CMP_HIDDEN), lambda a, g: (a, 0, 0)),
            pl.BlockSpec((1, 1, CMP_HIDDEN), lambda a, g: (a, 0, 0)),
            pl.BlockSpec((1, CMP_HIDDEN, HEAD_DIM), lambda a, g: (a, 0, 0)),
        ],
        out_specs=pl.BlockSpec((1, 1, n_seg, HEAD_DIM), lambda a, g: (a, g, 0, 0)),
        out_shape=jax.ShapeDtypeStruct((2, G, n_seg, HEAD_DIM), F32),
        compiler_params=_cparams(("arbitrary", "arbitrary")),
        name="compress",
    )(segs, pe, w1, b1, w2)


def _hgrn_consts(ch):
    nl = int(np.log2(ch))
    assert 1 << nl == ch
    t = np.arange(ch)[:, None]
    r = np.arange(ch)[None, :]
    mats = [r <= t]
    masks = [np.eye(ch, dtype=bool)]
    for l in range(nl):
        half = 1 << l
        blk = t // (2 * half)
        ref = blk * 2 * half + half - 1
        up = ((t >> l) & 1) == 1
        mats.append(np.where(up, (r > ref) & (r <= t), (r > t) & (r <= ref)))
        masks.append(up & (((r >> l) & 1) == 0) & (blk == r // (2 * half)))
    mc = np.concatenate(mats[:1 + HG_MM_LEVELS], axis=0).astype(np.float32)
    lm = np.stack(masks, axis=0).astype(np.float32)
    return nl, mc, lm


def _hgrn_kernel(q_ref, f_ref, i_ref, z_ref, lbl_ref, nw_ref, mc_ref, lm_ref, o_ref, st_ref, e_ref,
                 *, ch, nl, nch):
    @pl.when(pl.program_id(0) == 0)
    def _():
        st_ref[...] = jnp.zeros_like(st_ref)

    lg = lbl_ref[...]
    ex = jnp.exp(lg - jnp.max(lg, axis=0, keepdims=True))
    lb = ex[0:1] / jnp.sum(ex, axis=0, keepdims=True)
    f = lb + (1.0 - lb) * jax.nn.sigmoid(f_ref[...])
    g_hi, g_lo = _split_bf16(jnp.log(f))
    mc = mc_ref[...]
    chunks = range(nch)
    crow = [slice(c * ch, (c + 1) * ch) for c in chunks]
    for c in chunks:
        e_ref[c] = _dot(mc, g_hi[crow[c]]) + _dot(mc, g_lo[crow[c]])
    row = lax.broadcasted_iota(jnp.int32, (ch, 1), 0)
    nw = nw_ref[...]

    heads = [slice(h * HG_DK, (h + 1) * HG_DK) for h in range(HG_HEADS)]
    qs = [[q_ref[crow[c], sl] for sl in heads] for c in chunks]
    ks = [[1.0 - f[crow[c], sl] for sl in heads] for c in chunks]
    bs = [[e_ref[c, 0:ch, sl] for sl in heads] for c in chunks]
    scs = [[_dot_nt(q.astype(BF16), k.astype(BF16)) * lm_ref[0] for q, k in zip(qs[c], ks[c])] for c in chunks]
    for l in range(nl):
        up = ((row >> l) & 1) == 1
        half = 1 << l
        for c in chunks:
            for h, sl in enumerate(heads):
                if l < HG_MM_LEVELS:
                    el = e_ref[c, (l + 1) * ch:(l + 2) * ch, sl]
                    xl = jnp.where(up, qs[c][h], ks[c][h]) * jnp.exp(el)
                else:
                    parts = []
                    for r0 in range(0, ch, 2 * half):
                        mid = r0 + half
                        edge = bs[c][h][mid - 1:mid, :]
                        parts.append(ks[c][h][r0:mid] * jnp.exp(edge - bs[c][h][r0:mid]))
                        parts.append(qs[c][h][mid:mid + half] * jnp.exp(bs[c][h][mid:mid + half] - edge))
                    xl = jnp.concatenate(parts, axis=0)
                xl = xl.astype(BF16)
                scs[c][h] = scs[c][h] + _dot_nt(xl, xl) * lm_ref[l + 1]

    sts = [st_ref[h] for h in range(HG_HEADS)]
    for c in chunks:
        vbs = [i_ref[crow[c], sl].astype(BF16) for sl in heads]
        inters = [_dot_nt((qs[c][h] * jnp.exp(bs[c][h])).astype(BF16), sts[h].astype(BF16))
                  for h in range(HG_HEADS)]
        for h in range(HG_HEADS):
            last = bs[c][h][ch - 1:ch, :]
            ke = (ks[c][h] * jnp.exp(last - bs[c][h])).astype(BF16)
            upd = lax.dot_general(vbs[h], ke, (((0,), (0,)), ((), ())), preferred_element_type=F32)
            sts[h] = sts[h] * jnp.exp(last) + upd
        outs = [inters[h] + _dot(scs[c][h].astype(BF16), vbs[h]) for h in range(HG_HEADS)]
        for h, sl in enumerate(heads):
            o = outs[h]
            ms = jnp.mean(o * o, axis=-1, keepdims=True)
            z = z_ref[crow[c], sl]
            o_ref[crow[c], sl] = (o * lax.rsqrt(ms + EPS) * nw * (z * jax.nn.sigmoid(z))).astype(o_ref.dtype)
    for h in range(HG_HEADS):
        st_ref[h] = sts[h]


def _hgrn(proj, lb_logits, hg_norm_w, ch=128, nch=4):
    S = proj.shape[0]
    nl, mc, lm = _hgrn_consts(ch)
    kern = functools.partial(_hgrn_kernel, ch=ch, nl=nl, nch=nch)
    nlb = lb_logits.shape[0]
    rows = ch * nch
    assert S % rows == 0
    return pl.pallas_call(
        kern,
        grid=(S // rows,),
        in_specs=[
            pl.BlockSpec((rows, HG_FDIM), lambda c: (c, 0)),
            pl.BlockSpec((rows, HG_FDIM), lambda c: (c, 1)),
            pl.BlockSpec((rows, HG_WIDTH), lambda c: (c, 2)),
            pl.BlockSpec((rows, HG_WIDTH), lambda c: (c, 3)),
            pl.BlockSpec((nlb, HG_FDIM), lambda c: (0, 0)),
            pl.BlockSpec((1, HG_DV), lambda c: (0, 0)),
            pl.BlockSpec(mc.shape, lambda c: (0, 0)),
            pl.BlockSpec(lm.shape, lambda c: (0, 0, 0)),
        ],
        out_specs=pl.BlockSpec((rows, HG_WIDTH), lambda c: (c, 0)),
        out_shape=jax.ShapeDtypeStruct((S, HG_WIDTH), BF16),
        scratch_shapes=[
            pltpu.VMEM((HG_HEADS, HG_DV, HG_DK), F32),
            pltpu.VMEM((nch, mc.shape[0], HG_FDIM), F32),
        ],
        compiler_params=_cparams(("arbitrary",)),
        name="hgrn",
    )(proj, proj, proj, proj, lb_logits, hg_norm_w.reshape(1, HG_DV), jnp.asarray(mc, BF16), jnp.asarray(lm, F32))


def _stack_heads(q_ref, rows, cosf, sinf):
    qscale = (HEAD_DIM ** -0.5) * LOG2E
    return jnp.concatenate(
        [(_rope(q_ref[rows, h * HEAD_DIM:(h + 1) * HEAD_DIM], cosf, sinf) * qscale).astype(BF16)
         for h in range(NSA_HG)], axis=0)


def _per_head(a):
    return jnp.concatenate([a] * NSA_HG, axis=0)


def _nsa_select_kernel(q_ref, cos_ref, sin_ref, gate_ref, kc_ref, vc_ref, ov_ref, ca_ref,
                       selb_ref, pc_ref, qb_ref, *, nsub):
    n = pl.program_id(1)
    R = NSA_HG * Q_BLOCK
    ncp = kc_ref.shape[2]
    nsel_pad = ov_ref.shape[1]
    nb = Q_BLOCK // CMP_STRIDE
    nsb = Q_BLOCK // SEL_LEN
    subs = range(nsub)
    rows = [slice(i * Q_BLOCK, (i + 1) * Q_BLOCK) for i in subs]
    q0s = [(n * nsub + i) * Q_BLOCK for i in subs]

    def body(ncols):
        nrows = ncols // nb * nsb
        crow = lax.broadcasted_iota(jnp.int32, (ncols, LANES), 0)
        clane = lax.broadcasted_iota(jnp.int32, (ncols, LANES), 1)
        kc = kc_ref[0, 0, :ncols, :].astype(BF16)
        vca = jnp.concatenate([vc_ref[0, 0, :ncols, :].astype(BF16), jnp.ones((ncols, LANES), BF16)], axis=1)
        ca4 = _per_head(ca_ref[...])
        ov = ov_ref[:ncols, :]

        ss = []
        for i in subs:
            q4 = _stack_heads(q_ref, rows[i], cos_ref[rows[i], :], sin_ref[rows[i], :])
            for h in range(NSA_HG):
                qb_ref[rows[i], h * HEAD_DIM:(h + 1) * HEAD_DIM] = q4[h * Q_BLOCK:(h + 1) * Q_BLOCK]
            c_first = (n * nsub + i) * nb - (CMP_LEN // CMP_STRIDE - 1)
            flags = jnp.where(clane == 0, jnp.where(crow >= c_first + nb, NEG_BIG, 0.0),
                              jnp.where(crow == c_first + clane - 1, 1.0, 0.0)).astype(BF16)
            ss.append(_halves(_dot_nt, jnp.concatenate([q4, ca4], axis=1), jnp.concatenate([kc, flags], axis=1)))
        ebs = [jnp.exp2(s - jnp.max(s, axis=-1, keepdims=True)).astype(BF16) for s in ss]
        pvs = [_halves(_dot, e, vca) for e in ebs]

        imps = []
        for i in subs:
            t_row = q0s[i] + (lax.broadcasted_iota(jnp.int32, (R, LANES), 0) & (Q_BLOCK - 1))
            inv = jnp.where(t_row >= CMP_LEN - 1, 1.0, 0.0) / pvs[i][:, LANES:]
            o_cmp = pvs[i][:, :LANES] * inv
            gate = jax.nn.sigmoid(gate_ref[rows[i], :])
            for h in range(NSA_HG):
                pc_ref[rows[i], h * HEAD_DIM:(h + 1) * HEAD_DIM] = (
                    gate[:, 3 * h:3 * h + 1] * o_cmp[h * Q_BLOCK:(h + 1) * Q_BLOCK])
            eo = _dot(ebs[i], ov) * inv
            imp = eo[0:Q_BLOCK]
            for h in range(1, NSA_HG):
                imp = imp + eo[h * Q_BLOCK:(h + 1) * Q_BLOCK]
            imps.append(imp)

        jj = lax.broadcasted_iota(jnp.int32, (nrows, Q_BLOCK), 0)
        jjf = jj.astype(F32)
        lane_q = lax.broadcasted_iota(jnp.int32, (nrows, Q_BLOCK), 1)
        TAKEN = -2.0
        scores = []
        for i in subs:
            tq = q0s[i] + lane_q
            jt = tq >> 6
            forced = (jj == 0) | (jj == jt) | (jj == jt - 1)
            scores.append(jnp.where(jj * SEL_LEN <= tq, jnp.where(forced, TAKEN, imps[i].T[:nrows]), -1.0))
        for _ in range(SEL_TOPK - 3):
            for i in subs:
                mx = jnp.max(scores[i], axis=0, keepdims=True)
                first = jnp.min(jnp.where(scores[i] == mx, jjf, float(nsel_pad)), axis=0, keepdims=True)
                first = jnp.where(mx >= 0.0, first, -1.0)
                scores[i] = jnp.where(jjf == first, TAKEN, scores[i])
        for i in subs:
            bias = jnp.where(scores[i] == TAKEN, 0.0, NEG_BIG)
            if nrows < nsel_pad:
                bias = jnp.concatenate([bias, jnp.full((nsel_pad - nrows, Q_BLOCK), NEG_BIG, F32)], axis=0)
            selb_ref[rows[i], :] = bias.T.astype(BF16)

    needed = (n + 1) * nsub * nb
    widths = list(range(LANES, ncp + 1, LANES))
    for w in widths:
        lo = w - LANES
        cond = needed > lo if w == widths[-1] else (needed > lo) & (needed <= w)
        pl.when(cond)(functools.partial(body, w))


def _nsa_select(proj, cosf, sinf, kcvc, nsub=4):
    S = proj.shape[0]
    ncp = kcvc.shape[2]
    nsel = S // SEL_LEN
    qs = min(nsub * Q_BLOCK, S)
    nsub = qs // Q_BLOCK
    assert nsel <= LANES and ncp % LANES == 0 and S % qs == 0
    assert CMP_LEN == 2 * CMP_STRIDE and Q_BLOCK % CMP_STRIDE == 0 and Q_BLOCK // CMP_STRIDE < LANES
    ov = jnp.asarray(_overlap_matrix(ncp, LANES, nsel), BF16)
    ca = jnp.asarray(_cmp_staircase(), BF16)
    qw = NSA_HG * HEAD_DIM
    return pl.pallas_call(
        functools.partial(_nsa_select_kernel, nsub=nsub),
        grid=(NSA_KV, S // qs),
        in_specs=[
            pl.BlockSpec((qs, qw), lambda g, n: (n, OFF_Q // qw + g)),
            pl.BlockSpec((qs, LANES), lambda g, n: (n, 0)),
            pl.BlockSpec((qs, LANES), lambda g, n: (n, 0)),
            pl.BlockSpec((qs, LANES), lambda g, n: (n, OFF_G // LANES + g)),
            pl.BlockSpec((1, 1, ncp, HEAD_DIM), lambda g, n: (0, g, 0, 0)),
            pl.BlockSpec((1, 1, ncp, HEAD_DIM), lambda g, n: (1, g, 0, 0)),
            pl.BlockSpec((ncp, LANES), lambda g, n: (0, 0)),
            pl.BlockSpec((Q_BLOCK, LANES), lambda g, n: (0, 0)),
        ],
        out_specs=[
            pl.BlockSpec((qs, LANES), lambda g, n: (n, g)),
            pl.BlockSpec((qs, qw), lambda g, n: (n, g)),
            pl.BlockSpec((qs, qw), lambda g, n: (n, g)),
        ],
        out_shape=[
            jax.ShapeDtypeStruct((S, NSA_KV * LANES), BF16),
            jax.ShapeDtypeStruct((S, NSA_WIDTH), F32),
            jax.ShapeDtypeStruct((S, NSA_WIDTH), BF16),
        ],
        compiler_params=_cparams(("arbitrary", "arbitrary")),
        name="nsa_select",
    )(proj, cosf, sinf, proj, kcvc, kcvc, ov, ca)


def _nsa_kernel(q_ref, gate_ref, z_ref, selb_ref, pc_ref, wb_ref, cb_ref,
                ks_ref, vs_ref, kw_ref, vw_ref, o_ref, m_ref, accl_ref, sa_ref, sb_ref, part_ref, *, tk):
    n = pl.program_id(1)
    q0 = n * Q_BLOCK
    R = NSA_HG * Q_BLOCK
    q4 = jnp.concatenate([q_ref[:, h * HEAD_DIM:(h + 1) * HEAD_DIM] for h in range(NSA_HG)], axis=0)
    per_head = _per_head

    wspan = WINDOW + Q_BLOCK
    w0 = pl.multiple_of(jnp.maximum(q0 - WINDOW, 0), Q_BLOCK)
    sw = (_halves(_dot_nt, q4, kw_ref[pl.ds(w0, wspan), :])
          + per_head(wb_ref[jnp.minimum(n, WINDOW // Q_BLOCK)]))
    ew = jnp.exp2(sw - jnp.max(sw, axis=-1, keepdims=True))
    vwa = jnp.concatenate([vw_ref[pl.ds(w0, wspan), :], jnp.ones((wspan, LANES), BF16)], axis=1)
    pvw = _halves(_dot, ew.astype(BF16), vwa)
    o_win = pvw[:, :LANES] / pvw[:, LANES:]

    gate = jax.nn.sigmoid(gate_ref[...])
    for h in range(NSA_HG):
        cs = slice(h * HEAD_DIM, (h + 1) * HEAD_DIM)
        part_ref[:, cs] = pc_ref[:, cs] + gate[:, 3 * h + 2:3 * h + 3] * o_win[h * Q_BLOCK:(h + 1) * Q_BLOCK]

    sel_bias = per_head(selb_ref[...])

    m_ref[...] = jnp.full(m_ref.shape, NEG_BIG, F32)
    accl_ref[...] = jnp.zeros(accl_ref.shape, F32)
    qa = jnp.concatenate([q4, sel_bias], axis=1)

    n_q = tk // Q_BLOCK
    last = lax.shift_right_logical(n, n_q.bit_length() - 1)
    diag = n - last * n_q

    def score_tile(kt, dst_ref, width=tk):
        k0 = pl.multiple_of(kt * tk, tk)
        key_blk = lax.broadcasted_iota(jnp.int32, (width, LANES), 0) >> 6
        blk_lane = lax.broadcasted_iota(jnp.int32, (width, LANES), 1)
        onehot = jnp.where(key_blk + kt * (tk // SEL_LEN) == blk_lane, 1.0, 0.0).astype(BF16)
        dst_ref[:, :width] = _dot_nt(qa, jnp.concatenate([ks_ref[pl.ds(k0, width), :], onehot], axis=1))

    def consume_tile(kt, src_ref, last_diag=None):
        width = tk if last_diag is None else (last_diag + 1) * Q_BLOCK
        k0 = pl.multiple_of(kt * tk, tk)
        sc = src_ref[:, :width]
        if last_diag is not None:
            sc = sc + per_head(cb_ref[last_diag, :, :width])
        m_prev = m_ref[...]
        m_next = jnp.maximum(m_prev, jnp.max(sc, axis=-1, keepdims=True))
        pr = jnp.exp2(sc - jnp.concatenate([m_next] * (width // LANES), axis=1))
        alpha = jnp.exp2(m_prev - m_next)
        va = jnp.concatenate([vs_ref[pl.ds(k0, width), :], jnp.ones((width, LANES), BF16)], axis=1)
        accl_ref[...] = jnp.concatenate([alpha, alpha], axis=1) * accl_ref[...] + _dot(pr.astype(BF16), va)
        m_ref[...] = m_next

    score_tile(0, sa_ref)

    def slc_pair(i, carry):
        score_tile(2 * i + 1, sb_ref)
        consume_tile(2 * i, sa_ref)
        score_tile(2 * i + 2, sa_ref)
        consume_tile(2 * i + 1, sb_ref)
        return carry

    lax.fori_loop(0, lax.shift_right_logical(last, 1), slc_pair, 0)

    def tail_even(d):
        consume_tile(last, sa_ref, d)

    def tail_odd(d):
        score_tile(last, sb_ref, (d + 1) * Q_BLOCK)
        consume_tile(last - 1, sa_ref)
        consume_tile(last, sb_ref, d)

    for d in range(n_q):
        pl.when(((last & 1) == 0) & (diag == d))(functools.partial(tail_even, d))
        pl.when(((last & 1) == 1) & (diag == d))(functools.partial(tail_odd, d))

    o_slc = accl_ref[:, :LANES] / accl_ref[:, LANES:]

    gsig = jax.nn.sigmoid(gate_ref[...])
    for h in range(NSA_HG):
        rs = slice(h * Q_BLOCK, (h + 1) * Q_BLOCK)
        cs = slice(h * HEAD_DIM, (h + 1) * HEAD_DIM)
        oh = part_ref[:, cs] + gsig[:, 3 * h + 1:3 * h + 2] * o_slc[rs]
        z = z_ref[:, cs]
        o_ref[:, cs] = (oh * (z * jax.nn.sigmoid(z))).astype(o_ref.dtype)


def _cmp_staircase():
    nb = Q_BLOCK // CMP_STRIDE
    a = np.zeros((Q_BLOCK, LANES), np.float32)
    a[:, 0] = 1.0
    r = np.arange(Q_BLOCK)[:, None]
    i = np.arange(nb)[None, :]
    a[:, 1:1 + nb] = np.where(r < CMP_STRIDE * i + (CMP_STRIDE - 1), NEG_BIG, 0.0)
    return a


def _window_masks():
    n_w = WINDOW // Q_BLOCK
    r = np.arange(Q_BLOCK)[None, :, None]
    c = np.arange(WINDOW + Q_BLOCK)[None, None, :]
    d = np.arange(n_w + 1)[:, None, None]
    ok = np.where(d < n_w, c <= Q_BLOCK * d + r, (c > r) & (c <= WINDOW + r))
    return np.where(ok, 0.0, NEG_BIG).astype(np.float32)


def _causal_staircases(tk):
    n_q = tk // Q_BLOCK
    r = np.arange(Q_BLOCK)[None, :, None]
    c = np.arange(tk)[None, None, :]
    d = np.arange(n_q)[:, None, None]
    return np.where(c <= Q_BLOCK * d + r, 0.0, NEG_BIG).astype(np.float32)


def _overlap_matrix(ncp, nsel_pad, nsel):
    ci = np.arange(ncp)[:, None] * CMP_STRIDE
    sj = np.arange(nsel_pad)[None, :] * SEL_LEN
    ov = (ci < sj + SEL_LEN) & (ci + CMP_LEN > sj) & (np.arange(nsel_pad)[None, :] < nsel) & (np.arange(ncp)[:, None] < ncp - 1)
    return ov.astype(np.float32)


def _nsa(proj, qb, selb, pcmp, kvb, tk=512):
    S = proj.shape[0]
    assert S % tk == 0 and S >= WINDOW + Q_BLOCK and WINDOW % Q_BLOCK == 0
    wb = jnp.asarray(_window_masks(), F32)
    n_q = tk // Q_BLOCK
    assert n_q & (n_q - 1) == 0 and tk % SEL_LEN == 0
    cb = jnp.asarray(_causal_staircases(tk), F32)
    qw = NSA_HG * HEAD_DIM
    R = NSA_HG * Q_BLOCK
    kern = functools.partial(_nsa_kernel, tk=tk)
    return pl.pallas_call(
        kern,
        grid=(NSA_KV, S // Q_BLOCK),
        in_specs=[
            pl.BlockSpec((Q_BLOCK, qw), lambda g, n: (n, g)),
            pl.BlockSpec((Q_BLOCK, LANES), lambda g, n: (n, OFF_G // LANES + g)),
            pl.BlockSpec((Q_BLOCK, qw), lambda g, n: (n, OFF_Z // qw + g)),
            pl.BlockSpec((Q_BLOCK, LANES), lambda g, n: (n, g)),
            pl.BlockSpec((Q_BLOCK, qw), lambda g, n: (n, g)),
            pl.BlockSpec((WINDOW // Q_BLOCK + 1, Q_BLOCK, WINDOW + Q_BLOCK), lambda g, n: (0, 0, 0)),
            pl.BlockSpec((n_q, Q_BLOCK, tk), lambda g, n: (0, 0, 0)),
            pl.BlockSpec((S, HEAD_DIM), lambda g, n: (0, 0 * NSA_KV + g)),
            pl.BlockSpec((S, HEAD_DIM), lambda g, n: (0, 1 * NSA_KV + g)),
            pl.BlockSpec((S, HEAD_DIM), lambda g, n: (0, 2 * NSA_KV + g)),
            pl.BlockSpec((S, HEAD_DIM), lambda g, n: (0, 3 * NSA_KV + g)),
        ],
        out_specs=pl.BlockSpec((Q_BLOCK, qw), lambda g, n: (n, g)),
        out_shape=jax.ShapeDtypeStruct((S, NSA_WIDTH), BF16),
        scratch_shapes=[
            pltpu.VMEM((R, LANES), F32),
            pltpu.VMEM((R, HEAD_DIM + LANES), F32),
            pltpu.VMEM((R, tk), F32),
            pltpu.VMEM((R, tk), F32),
            pltpu.VMEM((Q_BLOCK, qw), F32),
        ],
        compiler_params=_cparams(("arbitrary", "arbitrary")),
        name="nsa",
    )(qb, proj, proj, selb, pcmp, wb, cb, kvb, kvb, kvb, kvb)


def _merge_kernel(ohg_ref, onsa_ref, g0_ref, g1_ref, g2_ref, g3_ref, whg_ref, wnsa_ref, o_ref):
    y_hg = _dot(ohg_ref[...], whg_ref[...])
    y_nsa = _dot(onsa_ref[...], wnsa_ref[...])
    half = D_MODEL // 2
    gh = (g0_ref, g1_ref)
    gn = (g2_ref, g3_ref)
    for c in range(2):
        cs = slice(c * half, (c + 1) * half)
        m = jax.nn.sigmoid(gh[c][...]) * y_hg[:, cs] + jax.nn.sigmoid(gn[c][...]) * y_nsa[:, cs]
        o_ref[:, cs] = m.astype(o_ref.dtype)


def _merge(o_hg, o_nsa, proj, w_hg, w_nsa, tm=512):
    S = o_hg.shape[0]
    tm = min(tm, S)
    half = D_MODEL // 2
    gb = OFF_MG // half
    return pl.pallas_call(
        _merge_kernel,
        grid=(S // tm,),
        in_specs=[
            pl.BlockSpec((tm, HG_WIDTH), lambda i: (i, 0)),
            pl.BlockSpec((tm, NSA_WIDTH), lambda i: (i, 0)),
            pl.BlockSpec((tm, half), lambda i: (i, gb + 0)),
            pl.BlockSpec((tm, half), lambda i: (i, gb + 1)),
            pl.BlockSpec((tm, half), lambda i: (i, gb + 2)),
            pl.BlockSpec((tm, half), lambda i: (i, gb + 3)),
            pl.BlockSpec((HG_WIDTH, D_MODEL), lambda i: (0, 0)),
            pl.BlockSpec((NSA_WIDTH, D_MODEL), lambda i: (0, 0)),
        ],
        out_specs=pl.BlockSpec((tm, D_MODEL), lambda i: (i, 0)),
        out_shape=jax.ShapeDtypeStruct((S, D_MODEL), BF16),
        compiler_params=_cparams(("arbitrary",)),
        name="merge",
    )(o_hg, o_nsa, proj, proj, proj, proj, w_hg, w_nsa)


def _out_kernel(x_ref, m_ref, w_ref, nw_ref, o_ref):
    h = x_ref[...] + _dot(m_ref[...], w_ref[...])
    ms = jnp.mean(h * h, axis=-1, keepdims=True)
    o_ref[...] = h * lax.rsqrt(ms + EPS) * nw_ref[...]


def _out(x2, merged, w_out, final_w, tm=512):
    S, D = x2.shape
    tm = min(tm, S)
    return pl.pallas_call(
        _out_kernel,
        grid=(S // tm,),
        in_specs=[
            pl.BlockSpec((tm, D), lambda i: (i, 0)),
            pl.BlockSpec((tm, D), lambda i: (i, 0)),
            pl.BlockSpec((D, D), lambda i: (0, 0)),
            pl.BlockSpec((1, D), lambda i: (0, 0)),
        ],
        out_specs=pl.BlockSpec((tm, D), lambda i: (i, 0)),
        out_shape=jax.ShapeDtypeStruct((S, D), F32),
        compiler_params=_cparams(("arbitrary",)),
        name="out_proj",
    )(x2, merged, w_out, final_w.reshape(1, D))


def _layer(x2, norm_w, w_in, lb_logits, hg_norm_w, cmp_k_pos, cmp_k_w1, cmp_k_b1, cmp_k_w2,
           cmp_v_pos, cmp_v_w1, cmp_v_b1, cmp_v_w2, w_branch_hg, w_branch_nsa, w_out, final_w):
    S = x2.shape[0]
    proj = _norm_proj(x2, norm_w, _w_prep(w_in.T))
    cosf, sinf = _rope_tables(S)
    segs, kvb = _kv_prep(proj, cosf, sinf)
    pieces = np.asarray(ROPE_PIECES)
    k_pos = cmp_k_pos[:, ROPE_SRC]
    k_w1 = cmp_k_w1.reshape(CMP_LEN, len(ROPE_PIECES), ROT_HALF, CMP_HIDDEN)[:, pieces].reshape(cmp_k_w1.shape)
    k_w2 = cmp_k_w2[:, ROPE_SRC]
    pe = jnp.stack([k_pos.reshape(1, -1), cmp_v_pos.reshape(1, -1)])
    w1 = jnp.stack([k_w1, cmp_v_w1]).astype(BF16)
    b1 = jnp.stack([cmp_k_b1.reshape(1, -1), cmp_v_b1.reshape(1, -1)])
    w2 = jnp.stack([k_w2, cmp_v_w2]).astype(BF16)
    kcvc = _compress(segs, pe, w1, b1, w2)
    o_hg = _hgrn(proj, lb_logits, hg_norm_w)
    selb, pcmp, qb = _nsa_select(proj, cosf, sinf, kcvc)
    o_nsa = _nsa(proj, qb, selb, pcmp, kvb)
    merged = _merge(o_hg, o_nsa, proj, w_branch_hg.astype(BF16), w_branch_nsa.astype(BF16))
    return _out(x2, merged, w_out.astype(BF16), final_w)


def kernel(x, norm_w, w_in, hg_lb_logits, hg_norm_w, cmp_k_pos, cmp_k_w1, cmp_k_b1, cmp_k_w2, cmp_v_pos, cmp_v_w1, cmp_v_b1, cmp_v_w2, w_branch_hg, w_branch_nsa, w_out, final_norm_w):
    B, S, D = x.shape
    assert B == 1 and D == D_MODEL and norm_w.shape[0] == 1
    y = _layer(x[0], norm_w[0], w_in[0], hg_lb_logits, hg_norm_w[0], cmp_k_pos[0], cmp_k_w1[0], cmp_k_b1[0],
               cmp_k_w2[0], cmp_v_pos[0], cmp_v_w1[0], cmp_v_b1[0], cmp_v_w2[0], w_branch_hg[0],
               w_branch_nsa[0], w_out[0], final_norm_w)
    return y[None]
```

```python
import functools

import numpy as np
import jax
import jax.numpy as jnp
from jax import lax
from jax.experimental import pallas as pl
from jax.experimental.pallas import tpu as pltpu

F32 = jnp.float32
BF16 = jnp.bfloat16

D_MODEL = 2048
HG_HEADS = 8
HG_DK = 128
HG_DV = 128
HG_FDIM = HG_HEADS * HG_DK
HG_WIDTH = HG_HEADS * HG_DV
HG_MM_LEVELS = 3
NSA_HEADS = 16
NSA_KV = 4
NSA_HG = NSA_HEADS // NSA_KV
HEAD_DIM = 128
NSA_WIDTH = NSA_HEADS * HEAD_DIM
NSA_KVW = NSA_KV * HEAD_DIM
CMP_LEN = 32
CMP_STRIDE = 16
CMP_HIDDEN = 512
SEL_LEN = 64
SEL_TOPK = 16
WINDOW = 512
Q_BLOCK = 256
ROPE_THETA = 500000.0
ROT_DIM = HEAD_DIM // 4
ROT_HALF = ROT_DIM // 2
EPS = 1e-6
LOG2E = float(np.log2(np.e))

LANES = 128
NEG_BIG = -1e30
VMEM_LIMIT = 56 * 1024 * 1024

OFF_Q = 4 * HG_FDIM
OFF_KV = OFF_Q + NSA_WIDTH
OFF_MG = OFF_KV + 6 * NSA_KVW
OFF_Z = OFF_MG + 2 * D_MODEL
OFF_G = OFF_Z + NSA_WIDTH
IN_GATE_OFF = OFF_MG
IN_Z_OFF = IN_GATE_OFF + 3 * NSA_HEADS
IN_MG_OFF = IN_Z_OFF + NSA_WIDTH


def _dot(a, b):
    return jnp.dot(a, b, preferred_element_type=F32)


def _dot_nt(a, b):
    return lax.dot_general(a, b, (((1,), (1,)), ((), ())), preferred_element_type=F32)


def _halves(dot, a, b):
    h = a.shape[0] // 2
    return jnp.concatenate([dot(a[:h], b), dot(a[h:], b)], axis=0)


def _split_bf16(a):
    hi = a.astype(BF16)
    lo = (a - hi.astype(F32)).astype(BF16)
    return hi, lo


def _cparams(sem):
    return pltpu.CompilerParams(dimension_semantics=sem, vmem_limit_bytes=VMEM_LIMIT)


def _w_prep_kernel(wt_ref, wg_ref, o_ref, *, n_direct):
    j = pl.program_id(0)

    @pl.when(j < n_direct)
    def _():
        o_ref[...] = wt_ref[...].astype(BF16)

    @pl.when(j >= n_direct)
    def _():
        o_ref[...] = wg_ref[...].astype(BF16)


def _w_prep(wt, tn=512):
    D = wt.shape[1]
    n_main, n_mg, n_z = OFF_MG // tn, 2 * D_MODEL // tn, NSA_WIDTH // tn
    n_direct = n_main + n_mg + n_z
    assert OFF_MG % tn == 0 and D_MODEL % tn == 0 and NSA_KV * LANES == tn
    gates = wt[IN_GATE_OFF:IN_Z_OFF].reshape(NSA_KV, 3 * NSA_HG, D)
    gates = jnp.pad(gates, ((0, 0), (0, LANES - 3 * NSA_HG), (0, 0))).reshape(NSA_KV * LANES, D)
    n_extra = 2
    gates = jnp.pad(gates, ((0, (n_extra - 1) * tn), (0, 0)))

    sub = 8
    assert IN_MG_OFF % sub == 0 and IN_Z_OFF % sub == 0 and tn % sub == 0

    def src_row(j):
        in_mg = IN_MG_OFF // sub + (j - n_main) * (tn // sub)
        in_z = IN_Z_OFF // sub + (jnp.minimum(j, n_direct - 1) - n_main - n_mg) * (tn // sub)
        return jnp.where(j < n_main, j * (tn // sub), jnp.where(j < n_main + n_mg, in_mg, in_z)) * sub

    return pl.pallas_call(
        functools.partial(_w_prep_kernel, n_direct=n_direct),
        grid=(n_direct + n_extra,),
        in_specs=[
            pl.BlockSpec((pl.Element(tn), pl.Element(D)), lambda j: (src_row(j), 0)),
            pl.BlockSpec((tn, D), lambda j: (jnp.maximum(j - n_direct, 0), 0)),
        ],
        out_specs=pl.BlockSpec((tn, D), lambda j: (j, 0)),
        out_shape=jax.ShapeDtypeStruct(((n_direct + n_extra) * tn, D), BF16),
        compiler_params=_cparams(("arbitrary",)),
        name="w_prep",
    )(wt, gates)


def _norm_proj_kernel(x_ref, nw_ref, w_ref, o_ref, xn_ref):
    @pl.when(pl.program_id(1) == 0)
    def _():
        x = x_ref[...]
        ms = jnp.mean(x * x, axis=-1, keepdims=True)
        xn_ref[...] = (x * lax.rsqrt(ms + EPS) * nw_ref[...]).astype(BF16)

    o_ref[...] = _dot_nt(xn_ref[...], w_ref[...])


def _norm_proj(x2, norm_w, wtb, tm=1024, tn=2048):
    S, D = x2.shape
    N = wtb.shape[0]
    tm = min(tm, S)
    return pl.pallas_call(
        _norm_proj_kernel,
        grid=(S // tm, N // tn),
        in_specs=[
            pl.BlockSpec((tm, D), lambda i, j: (i, 0)),
            pl.BlockSpec((1, D), lambda i, j: (0, 0)),
            pl.BlockSpec((tn, D), lambda i, j: (j, 0)),
        ],
        out_specs=pl.BlockSpec((tm, tn), lambda i, j: (i, j)),
        out_shape=jax.ShapeDtypeStruct((S, N), F32),
        scratch_shapes=[pltpu.VMEM((tm, D), BF16)],
        compiler_params=_cparams(("arbitrary", "arbitrary")),
        name="norm_proj",
    )(x2, norm_w.reshape(1, D), wtb)


def _rope(x, cosf, sinf):
    lane = lax.broadcasted_iota(jnp.int32, x.shape, 1)
    rot = jnp.where(lane < ROT_HALF, pltpu.roll(x, LANES - ROT_HALF, 1), pltpu.roll(x, ROT_HALF, 1))
    return x * cosf + rot * sinf


def _rope_tables(S):
    pos = np.arange(S, dtype=np.float64)
    inv = ROPE_THETA ** (-np.arange(0, ROT_DIM, 2, dtype=np.float64) / ROT_DIM)
    ang = pos[:, None] * inv[None, :]
    cos, sin = np.cos(ang), np.sin(ang)
    rest = LANES - ROT_DIM
    cosf = np.concatenate([cos, cos, np.ones((S, rest))], axis=1)
    sinf = np.concatenate([-sin, sin, np.zeros((S, rest))], axis=1)
    return jnp.asarray(cosf, F32), jnp.asarray(sinf, F32)


def _kv_prep_kernel(kv_ref, cos_ref, sin_ref, cmp_ref, kvb_ref, row_ref):
    cosf = cos_ref[...]
    sinf = sin_ref[...]
    W = NSA_KVW
    n_seg = row_ref.shape[0] // CMP_STRIDE

    def to_segments(a, g, rows):
        row_ref[...] = rows
        for l in range(CMP_STRIDE):
            cmp_ref[a, g, :, l * HEAD_DIM:(l + 1) * HEAD_DIM] = row_ref[pl.ds(l, n_seg, stride=CMP_STRIDE), :]

    for g in range(NSA_KV):
        kc = kv_ref[:, 0 * W + g * LANES:0 * W + (g + 1) * LANES]
        to_segments(0, g, _rope(kc, cosf, sinf))
        to_segments(1, g, kv_ref[:, 1 * W + g * LANES:1 * W + (g + 1) * LANES])
        ks = kv_ref[:, 2 * W + g * LANES:2 * W + (g + 1) * LANES]
        kvb_ref[:, 0 * W + g * LANES:0 * W + (g + 1) * LANES] = _rope(ks, cosf, sinf).astype(BF16)
        kvb_ref[:, 1 * W + g * LANES:1 * W + (g + 1) * LANES] = kv_ref[:, 3 * W + g * LANES:3 * W + (g + 1) * LANES].astype(BF16)
        kw = kv_ref[:, 4 * W + g * LANES:4 * W + (g + 1) * LANES]
        kvb_ref[:, 2 * W + g * LANES:2 * W + (g + 1) * LANES] = _rope(kw, cosf, sinf).astype(BF16)
        kvb_ref[:, 3 * W + g * LANES:3 * W + (g + 1) * LANES] = kv_ref[:, 5 * W + g * LANES:5 * W + (g + 1) * LANES].astype(BF16)


def _kv_prep(proj, cosf, sinf, tm=512):
    S = proj.shape[0]
    tm = min(tm, S)
    kvw = 6 * NSA_KVW
    return pl.pallas_call(
        _kv_prep_kernel,
        grid=(S // tm,),
        in_specs=[
            pl.BlockSpec((tm, kvw), lambda i: (i, OFF_KV // kvw)),
            pl.BlockSpec((tm, LANES), lambda i: (i, 0)),
            pl.BlockSpec((tm, LANES), lambda i: (i, 0)),
        ],
        out_specs=[
            pl.BlockSpec((2, NSA_KV, tm // CMP_STRIDE, CMP_STRIDE * HEAD_DIM), lambda i: (0, 0, i, 0)),
            pl.BlockSpec((tm, 4 * NSA_KVW), lambda i: (i, 0)),
        ],
        out_shape=[
            jax.ShapeDtypeStruct((2, NSA_KV, S // CMP_STRIDE, CMP_STRIDE * HEAD_DIM), F32),
            jax.ShapeDtypeStruct((S, 4 * NSA_KVW), BF16),
        ],
        scratch_shapes=[pltpu.VMEM((tm, HEAD_DIM), F32)],
        compiler_params=_cparams(("arbitrary",)),
        name="kv_prep",
    )(proj, cosf, sinf)


def _compress_kernel(seg_ref, pe_ref, w1_ref, b1_ref, w2_ref, o_ref):
    half = (CMP_LEN // 2) * HEAD_DIM
    seg = seg_ref[0, 0]
    n_seg = seg.shape[0]
    pe = pe_ref[0]
    a = (seg + pe[:, :half]).astype(BF16)
    b = (seg + pe[:, half:]).astype(BF16)
    u = _dot(a, w1_ref[0, :half, :])
    v = _dot(b, w1_ref[0, half:, :])
    v_next = pltpu.roll(v, n_seg - 1, 0)
    pre = u + v_next + b1_ref[0]
    h = 0.5 * pre * (1.0 + jnp.tanh(np.sqrt(2.0 / np.pi).astype(np.float32) * (pre + 0.044715 * (pre * pre * pre))))
    o_ref[0, 0] = _dot(h.astype(BF16), w2_ref[0])


def _compress(segs, pe, w1, b1, w2):
    _, G, n_seg, segw = segs.shape
    return pl.pallas_call(
        _compress_kernel,
        grid=(2, G),
        in_specs=[
            pl.BlockSpec((1, 1, n_seg, segw), lambda a, g: (a, g, 0, 0)),
            pl.BlockSpec((1, 1, 2 * segw), lambda a, g: (a, 0, 0)),
            pl.BlockSpec((1, 2 * segw, CMP_HIDDEN), lambda a, g: (a, 0, 0)),
            pl.BlockSpec((1, 1, CMP_HIDDEN), lambda a, g: (a, 0, 0)),
            pl.BlockSpec((1, CMP_HIDDEN, HEAD_DIM), lambda a, g: (a, 0, 0)),
        ],
        out_specs=pl.BlockSpec((1, 1, n_seg, HEAD_DIM), lambda a, g: (a, g, 0, 0)),
        out_shape=jax.ShapeDtypeStruct((2, G, n_seg, HEAD_DIM), F32),
        compiler_params=_cparams(("arbitrary", "arbitrary")),
        name="compress",
    )(segs, pe, w1, b1, w2)


def _hgrn_consts(ch):
    nl = int(np.log2(ch))
    assert 1 << nl == ch
    t = np.arange(ch)[:, None]
    r = np.arange(ch)[None, :]
    mats = [r <= t]
    masks = [np.eye(ch, dtype=bool)]
    for l in range(nl):
        half = 1 << l
        blk = t // (2 * half)
        ref = blk * 2 * half + half - 1
        up = ((t >> l) & 1) == 1
        mats.append(np.where(up, (r > ref) & (r <= t), (r > t) & (r <= ref)))
        masks.append(up & (((r >> l) & 1) == 0) & (blk == r // (2 * half)))
    mc = np.concatenate(mats[:1 + HG_MM_LEVELS], axis=0).astype(np.float32)
    lm = np.stack(masks, axis=0).astype(np.float32)
    return nl, mc, lm


def _hgrn_kernel(q_ref, f_ref, i_ref, z_ref, lbl_ref, nw_ref, mc_ref, lm_ref, o_ref, st_ref, e_ref,
                 *, ch, nl, nch):
    @pl.when(pl.program_id(0) == 0)
    def _():
        st_ref[...] = jnp.zeros_like(st_ref)

    lg = lbl_ref[...]
    ex = jnp.exp(lg - jnp.max(lg, axis=0, keepdims=True))
    lb = ex[0:1] / jnp.sum(ex, axis=0, keepdims=True)
    f = lb + (1.0 - lb) * jax.nn.sigmoid(f_ref[...])
    g_hi, g_lo = _split_bf16(jnp.log(f))
    mc = mc_ref[...]
    chunks = range(nch)
    crow = [slice(c * ch, (c + 1) * ch) for c in chunks]
    for c in chunks:
        e_ref[c] = _dot(mc, g_hi[crow[c]]) + _dot(mc, g_lo[crow[c]])
    row = lax.broadcasted_iota(jnp.int32, (ch, 1), 0)
    nw = nw_ref[...]

    heads = [slice(h * HG_DK, (h + 1) * HG_DK) for h in range(HG_HEADS)]
    qs = [[q_ref[crow[c], sl] for sl in heads] for c in chunks]
    ks = [[1.0 - f[crow[c], sl] for sl in heads] for c in chunks]
    bs = [[e_ref[c, 0:ch, sl] for sl in heads] for c in chunks]
    scs = [[_dot_nt(q.astype(BF16), k.astype(BF16)) * lm_ref[0] for q, k in zip(qs[c], ks[c])] for c in chunks]
    for l in range(nl):
        up = ((row >> l) & 1) == 1
        half = 1 << l
        for c in chunks:
            for h, sl in enumerate(heads):
                if l < HG_MM_LEVELS:
                    el = e_ref[c, (l + 1) * ch:(l + 2) * ch, sl]
                    xl = jnp.where(up, qs[c][h], ks[c][h]) * jnp.exp(el)
                else:
                    parts = []
                    for r0 in range(0, ch, 2 * half):
                        mid = r0 + half
                        edge = bs[c][h][mid - 1:mid, :]
                        parts.append(ks[c][h][r0:mid] * jnp.exp(edge - bs[c][h][r0:mid]))
                        parts.append(qs[c][h][mid:mid + half] * jnp.exp(bs[c][h][mid:mid + half] - edge))
                    xl = jnp.concatenate(parts, axis=0)
                xl = xl.astype(BF16)
                scs[c][h] = scs[c][h] + _dot_nt(xl, xl) * lm_ref[l + 1]

    sts = [st_ref[h] for h in range(HG_HEADS)]
    for c in chunks:
        vbs = [i_ref[crow[c], sl].astype(BF16) for sl in heads]
        inters = [_dot_nt((qs[c][h] * jnp.exp(bs[c][h])).astype(BF16), sts[h].astype(BF16))
                  for h in range(HG_HEADS)]
        for h in range(HG_HEADS):
            last = bs[c][h][ch - 1:ch, :]
            ke = (ks[c][h] * jnp.exp(last - bs[c][h])).astype(BF16)
            upd = lax.dot_general(vbs[h], ke, (((0,), (0,)), ((), ())), preferred_element_type=F32)
            sts[h] = sts[h] * jnp.exp(last) + upd
        outs = [inters[h] + _dot(scs[c][h].astype(BF16), vbs[h]) for h in range(HG_HEADS)]
        for h, sl in enumerate(heads):
            o = outs[h]
            ms = jnp.mean(o * o, axis=-1, keepdims=True)
            z = z_ref[crow[c], sl]
            o_ref[crow[c], sl] = (o * lax.rsqrt(ms + EPS) * nw * (z * jax.nn.sigmoid(z))).astype(o_ref.dtype)
    for h in range(HG_HEADS):
        st_ref[h] = sts[h]


def _hgrn(proj, lb_logits, hg_norm_w, ch=128, nch=4):
    S = proj.shape[0]
    nl, mc, lm = _hgrn_consts(ch)
    kern = functools.partial(_hgrn_kernel, ch=ch, nl=nl, nch=nch)
    nlb = lb_logits.shape[0]
    rows = ch * nch
    assert S % rows == 0
    return pl.pallas_call(
        kern,
        grid=(S // rows,),
        in_specs=[
            pl.BlockSpec((rows, HG_FDIM), lambda c: (c, 0)),
            pl.BlockSpec((rows, HG_FDIM), lambda c: (c, 1)),
            pl.BlockSpec((rows, HG_WIDTH), lambda c: (c, 2)),
            pl.BlockSpec((rows, HG_WIDTH), lambda c: (c, 3)),
            pl.BlockSpec((nlb, HG_FDIM), lambda c: (0, 0)),
            pl.BlockSpec((1, HG_DV), lambda c: (0, 0)),
            pl.BlockSpec(mc.shape, lambda c: (0, 0)),
            pl.BlockSpec(lm.shape, lambda c: (0, 0, 0)),
        ],
        out_specs=pl.BlockSpec((rows, HG_WIDTH), lambda c: (c, 0)),
        out_shape=jax.ShapeDtypeStruct((S, HG_WIDTH), BF16),
        scratch_shapes=[
            pltpu.VMEM((HG_HEADS, HG_DV, HG_DK), F32),
            pltpu.VMEM((nch, mc.shape[0], HG_FDIM), F32),
        ],
        compiler_params=_cparams(("arbitrary",)),
        name="hgrn",
    )(proj, proj, proj, proj, lb_logits, hg_norm_w.reshape(1, HG_DV), jnp.asarray(mc, BF16), jnp.asarray(lm, F32))


def _stack_heads(q_ref, rows, cosf, sinf):
    qscale = (HEAD_DIM ** -0.5) * LOG2E
    return jnp.concatenate(
        [(_rope(q_ref[rows, h * HEAD_DIM:(h + 1) * HEAD_DIM], cosf, sinf) * qscale).astype(BF16)
         for h in range(NSA_HG)], axis=0)


def _per_head(a):
    return jnp.concatenate([a] * NSA_HG, axis=0)


def _nsa_select_kernel(q_ref, cos_ref, sin_ref, gate_ref, kc_ref, vc_ref, ov_ref, ca_ref,
                       selb_ref, pc_ref, qb_ref, *, nsub):
    n = pl.program_id(1)
    R = NSA_HG * Q_BLOCK
    ncp = kc_ref.shape[2]
    nsel_pad = ov_ref.shape[1]
    nb = Q_BLOCK // CMP_STRIDE
    nsb = Q_BLOCK // SEL_LEN
    subs = range(nsub)
    rows = [slice(i * Q_BLOCK, (i + 1) * Q_BLOCK) for i in subs]
    q0s = [(n * nsub + i) * Q_BLOCK for i in subs]

    def body(ncols):
        nrows = ncols // nb * nsb
        crow = lax.broadcasted_iota(jnp.int32, (ncols, LANES), 0)
        clane = lax.broadcasted_iota(jnp.int32, (ncols, LANES), 1)
        kc = kc_ref[0, 0, :ncols, :].astype(BF16)
        vca = jnp.concatenate([vc_ref[0, 0, :ncols, :].astype(BF16), jnp.ones((ncols, LANES), BF16)], axis=1)
        ca4 = _per_head(ca_ref[...])
        ov = ov_ref[:ncols, :]

        ss = []
        for i in subs:
            q4 = _stack_heads(q_ref, rows[i], cos_ref[rows[i], :], sin_ref[rows[i], :])
            for h in range(NSA_HG):
                qb_ref[rows[i], h * HEAD_DIM:(h + 1) * HEAD_DIM] = q4[h * Q_BLOCK:(h + 1) * Q_BLOCK]
            c_first = (n * nsub + i) * nb - (CMP_LEN // CMP_STRIDE - 1)
            flags = jnp.where(clane == 0, jnp.where(crow >= c_first + nb, NEG_BIG, 0.0),
                              jnp.where(crow == c_first + clane - 1, 1.0, 0.0)).astype(BF16)
            ss.append(_halves(_dot_nt, jnp.concatenate([q4, ca4], axis=1), jnp.concatenate([kc, flags], axis=1)))
        ebs = [jnp.exp2(s - jnp.max(s, axis=-1, keepdims=True)).astype(BF16) for s in ss]
        pvs = [_halves(_dot, e, vca) for e in ebs]

        imps = []
        for i in subs:
            t_row = q0s[i] + (lax.broadcasted_iota(jnp.int32, (R, LANES), 0) & (Q_BLOCK - 1))
            inv = jnp.where(t_row >= CMP_LEN - 1, 1.0, 0.0) / pvs[i][:, LANES:]
            o_cmp = pvs[i][:, :LANES] * inv
            gate = jax.nn.sigmoid(gate_ref[rows[i], :])
            for h in range(NSA_HG):
                pc_ref[rows[i], h * HEAD_DIM:(h + 1) * HEAD_DIM] = (
                    gate[:, 3 * h:3 * h + 1] * o_cmp[h * Q_BLOCK:(h + 1) * Q_BLOCK])
            eo = _dot(ebs[i], ov) * inv
            imp = eo[0:Q_BLOCK]
            for h in range(1, NSA_HG):
                imp = imp + eo[h * Q_BLOCK:(h + 1) * Q_BLOCK]
            imps.append(imp)

        jj = lax.broadcasted_iota(jnp.int32, (nrows, Q_BLOCK), 0)
        jjf = jj.astype(F32)
        lane_q = lax.broadcasted_iota(jnp.int32, (nrows, Q_BLOCK), 1)
        TAKEN = -2.0
        scores = []
        for i in subs:
            tq = q0s[i] + lane_q
            jt = tq >> 6
            forced = (jj == 0) | (jj == jt) | (jj == jt - 1)
            scores.append(jnp.where(jj * SEL_LEN <= tq, jnp.where(forced, TAKEN, imps[i].T[:nrows]), -1.0))
        for _ in range(SEL_TOPK - 3):
            for i in subs:
                mx = jnp.max(scores[i], axis=0, keepdims=True)
                first = jnp.min(jnp.where(scores[i] == mx, jjf, float(nsel_pad)), axis=0, keepdims=True)
                first = jnp.where(mx >= 0.0, first, -1.0)
                scores[i] = jnp.where(jjf == first, TAKEN, scores[i])
        for i in subs:
            bias = jnp.where(scores[i] == TAKEN, 0.0, NEG_BIG)
            if nrows < nsel_pad:
                bias = jnp.concatenate([bias, jnp.full((nsel_pad - nrows, Q_BLOCK), NEG_BIG, F32)], axis=0)
            selb_ref[rows[i], :] = bias.T.astype(BF16)

    needed = (n + 1) * nsub * nb
    widths = list(range(LANES, ncp + 1, LANES))
    for w in widths:
        lo = w - LANES
        cond = needed > lo if w == widths[-1] else (needed > lo) & (needed <= w)
        pl.when(cond)(functools.partial(body, w))


def _nsa_select(proj, cosf, sinf, kcvc, nsub=4):
    S = proj.shape[0]
    ncp = kcvc.shape[2]
    nsel = S // SEL_LEN
    qs = min(nsub * Q_BLOCK, S)
    nsub = qs // Q_BLOCK
    assert nsel <= LANES and ncp % LANES == 0 and S % qs == 0
    assert CMP_LEN == 2 * CMP_STRIDE and Q_BLOCK % CMP_STRIDE == 0 and Q_BLOCK // CMP_STRIDE < LANES
    ov = jnp.asarray(_overlap_matrix(ncp, LANES, nsel), BF16)
    ca = jnp.asarray(_cmp_staircase(), BF16)
    qw = NSA_HG * HEAD_DIM
    return pl.pallas_call(
        functools.partial(_nsa_select_kernel, nsub=nsub),
        grid=(NSA_KV, S // qs),
        in_specs=[
            pl.BlockSpec((qs, qw), lambda g, n: (n, OFF_Q // qw + g)),
            pl.BlockSpec((qs, LANES), lambda g, n: (n, 0)),
            pl.BlockSpec((qs, LANES), lambda g, n: (n, 0)),
            pl.BlockSpec((qs, LANES), lambda g, n: (n, OFF_G // LANES + g)),
            pl.BlockSpec((1, 1, ncp, HEAD_DIM), lambda g, n: (0, g, 0, 0)),
            pl.BlockSpec((1, 1, ncp, HEAD_DIM), lambda g, n: (1, g, 0, 0)),
            pl.BlockSpec((ncp, LANES), lambda g, n: (0, 0)),
            pl.BlockSpec((Q_BLOCK, LANES), lambda g, n: (0, 0)),
        ],
        out_specs=[
            pl.BlockSpec((qs, LANES), lambda g, n: (n, g)),
            pl.BlockSpec((qs, qw), lambda g, n: (n, g)),
            pl.BlockSpec((qs, qw), lambda g, n: (n, g)),
        ],
        out_shape=[
            jax.ShapeDtypeStruct((S, NSA_KV * LANES), BF16),
            jax.ShapeDtypeStruct((S, NSA_WIDTH), F32),
            jax.ShapeDtypeStruct((S, NSA_WIDTH), BF16),
        ],
        compiler_params=_cparams(("arbitrary", "arbitrary")),
        name="nsa_select",
    )(proj, cosf, sinf, proj, kcvc, kcvc, ov, ca)


def _nsa_kernel(q_ref, gate_ref, z_ref, selb_ref, pc_ref, wb_ref, cb_ref,
                ks_ref, vs_ref, kw_ref, vw_ref, o_ref, m_ref, accl_ref, sa_ref, sb_ref, part_ref, *, tk):
    n = pl.program_id(1)
    q0 = n * Q_BLOCK
    R = NSA_HG * Q_BLOCK
    q4 = jnp.concatenate([q_ref[:, h * HEAD_DIM:(h + 1) * HEAD_DIM] for h in range(NSA_HG)], axis=0)
    per_head = _per_head

    wspan = WINDOW + Q_BLOCK
    w0 = pl.multiple_of(jnp.maximum(q0 - WINDOW, 0), Q_BLOCK)
    sw = (_halves(_dot_nt, q4, kw_ref[pl.ds(w0, wspan), :])
          + per_head(wb_ref[jnp.minimum(n, WINDOW // Q_BLOCK)]))
    ew = jnp.exp2(sw - jnp.max(sw, axis=-1, keepdims=True))
    vwa = jnp.concatenate([vw_ref[pl.ds(w0, wspan), :], jnp.ones((wspan, LANES), BF16)], axis=1)
    pvw = _halves(_dot, ew.astype(BF16), vwa)
    o_win = pvw[:, :LANES] / pvw[:, LANES:]

    gate = jax.nn.sigmoid(gate_ref[...])
    for h in range(NSA_HG):
        cs = slice(h * HEAD_DIM, (h + 1) * HEAD_DIM)
        part_ref[:, cs] = pc_ref[:, cs] + gate[:, 3 * h + 2:3 * h + 3] * o_win[h * Q_BLOCK:(h + 1) * Q_BLOCK]

    sel_bias = per_head(selb_ref[...])

    m_ref[...] = jnp.full(m_ref.shape, NEG_BIG, F32)
    accl_ref[...] = jnp.zeros(accl_ref.shape, F32)
    qa = jnp.concatenate([q4, sel_bias], axis=1)

    n_q = tk // Q_BLOCK
    last = lax.shift_right_logical(n, n_q.bit_length() - 1)
    diag = n - last * n_q

    def score_tile(kt, dst_ref, width=tk):
        k0 = pl.multiple_of(kt * tk, tk)
        key_blk = lax.broadcasted_iota(jnp.int32, (width, LANES), 0) >> 6
        blk_lane = lax.broadcasted_iota(jnp.int32, (width, LANES), 1)
        onehot = jnp.where(key_blk + kt * (tk // SEL_LEN) == blk_lane, 1.0, 0.0).astype(BF16)
        dst_ref[:, :width] = _dot_nt(qa, jnp.concatenate([ks_ref[pl.ds(k0, width), :], onehot], axis=1))

    def consume_tile(kt, src_ref, last_diag=None):
        width = tk if last_diag is None else (last_diag + 1) * Q_BLOCK
        k0 = pl.multiple_of(kt * tk, tk)
        sc = src_ref[:, :width]
        if last_diag is not None:
            sc = sc + per_head(cb_ref[last_diag, :, :width])
        m_prev = m_ref[...]
        m_next = jnp.maximum(m_prev, jnp.max(sc, axis=-1, keepdims=True))
        pr = jnp.exp2(sc - jnp.concatenate([m_next] * (width // LANES), axis=1))
        alpha = jnp.exp2(m_prev - m_next)
        va = jnp.concatenate([vs_ref[pl.ds(k0, width), :], jnp.ones((width, LANES), BF16)], axis=1)
        accl_ref[...] = jnp.concatenate([alpha, alpha], axis=1) * accl_ref[...] + _dot(pr.astype(BF16), va)
        m_ref[...] = m_next

    score_tile(0, sa_ref)

    def slc_pair(i, carry):
        score_tile(2 * i + 1, sb_ref)
        consume_tile(2 * i, sa_ref)
        score_tile(2 * i + 2, sa_ref)
        consume_tile(2 * i + 1, sb_ref)
        return carry

    lax.fori_loop(0, lax.shift_right_logical(last, 1), slc_pair, 0)

    def tail_even(d):
        consume_tile(last, sa_ref, d)

    def tail_odd(d):
        score_tile(last, sb_ref, (d + 1) * Q_BLOCK)
        consume_tile(last - 1, sa_ref)
        consume_tile(last, sb_ref, d)

    for d in range(n_q):
        pl.when(((last & 1) == 0) & (diag == d))(functools.partial(tail_even, d))
        pl.when(((last & 1) == 1) & (diag == d))(functools.partial(tail_odd, d))

    o_slc = accl_ref[:, :LANES] / accl_ref[:, LANES:]

    gsig = jax.nn.sigmoid(gate_ref[...])
    for h in range(NSA_HG):
        rs = slice(h * Q_BLOCK, (h + 1) * Q_BLOCK)
        cs = slice(h * HEAD_DIM, (h + 1) * HEAD_DIM)
        oh = part_ref[:, cs] + gsig[:, 3 * h + 1:3 * h + 2] * o_slc[rs]
        z = z_ref[:, cs]
        o_ref[:, cs] = (oh * (z * jax.nn.sigmoid(z))).astype(o_ref.dtype)


def _cmp_staircase():
    nb = Q_BLOCK // CMP_STRIDE
    a = np.zeros((Q_BLOCK, LANES), np.float32)
    a[:, 0] = 1.0
    r = np.arange(Q_BLOCK)[:, None]
    i = np.arange(nb)[None, :]
    a[:, 1:1 + nb] = np.where(r < CMP_STRIDE * i + (CMP_STRIDE - 1), NEG_BIG, 0.0)
    return a


def _window_masks():
    n_w = WINDOW // Q_BLOCK
    r = np.arange(Q_BLOCK)[None, :, None]
    c = np.arange(WINDOW + Q_BLOCK)[None, None, :]
    d = np.arange(n_w + 1)[:, None, None]
    ok = np.where(d < n_w, c <= Q_BLOCK * d + r, (c > r) & (c <= WINDOW + r))
    return np.where(ok, 0.0, NEG_BIG).astype(np.float32)


def _causal_staircases(tk):
    n_q = tk // Q_BLOCK
    r = np.arange(Q_BLOCK)[None, :, None]
    c = np.arange(tk)[None, None, :]
    d = np.arange(n_q)[:, None, None]
    return np.where(c <= Q_BLOCK * d + r, 0.0, NEG_BIG).astype(np.float32)


def _overlap_matrix(ncp, nsel_pad, nsel):
    ci = np.arange(ncp)[:, None] * CMP_STRIDE
    sj = np.arange(nsel_pad)[None, :] * SEL_LEN
    ov = (ci < sj + SEL_LEN) & (ci + CMP_LEN > sj) & (np.arange(nsel_pad)[None, :] < nsel) & (np.arange(ncp)[:, None] < ncp - 1)
    return ov.astype(np.float32)


def _nsa(proj, qb, selb, pcmp, kvb, tk=512):
    S = proj.shape[0]
    assert S % tk == 0 and S >= WINDOW + Q_BLOCK and WINDOW % Q_BLOCK == 0
    wb = jnp.asarray(_window_masks(), F32)
    n_q = tk // Q_BLOCK
    assert n_q & (n_q - 1) == 0 and tk % SEL_LEN == 0
    cb = jnp.asarray(_causal_staircases(tk), F32)
    qw = NSA_HG * HEAD_DIM
    R = NSA_HG * Q_BLOCK
    kern = functools.partial(_nsa_kernel, tk=tk)
    return pl.pallas_call(
        kern,
        grid=(NSA_KV, S // Q_BLOCK),
        in_specs=[
            pl.BlockSpec((Q_BLOCK, qw), lambda g, n: (n, g)),
            pl.BlockSpec((Q_BLOCK, LANES), lambda g, n: (n, OFF_G // LANES + g)),
            pl.BlockSpec((Q_BLOCK, qw), lambda g, n: (n, OFF_Z // qw + g)),
            pl.BlockSpec((Q_BLOCK, LANES), lambda g, n: (n, g)),
            pl.BlockSpec((Q_BLOCK, qw), lambda g, n: (n, g)),
            pl.BlockSpec((WINDOW // Q_BLOCK + 1, Q_BLOCK, WINDOW + Q_BLOCK), lambda g, n: (0, 0, 0)),
            pl.BlockSpec((n_q, Q_BLOCK, tk), lambda g, n: (0, 0, 0)),
            pl.BlockSpec((S, HEAD_DIM), lambda g, n: (0, 0 * NSA_KV + g)),
            pl.BlockSpec((S, HEAD_DIM), lambda g, n: (0, 1 * NSA_KV + g)),
            pl.BlockSpec((S, HEAD_DIM), lambda g, n: (0, 2 * NSA_KV + g)),
            pl.BlockSpec((S, HEAD_DIM), lambda g, n: (0, 3 * NSA_KV + g)),
        ],
        out_specs=pl.BlockSpec((Q_BLOCK, qw), lambda g, n: (n, g)),
        out_shape=jax.ShapeDtypeStruct((S, NSA_WIDTH), BF16),
        scratch_shapes=[
            pltpu.VMEM((R, LANES), F32),
            pltpu.VMEM((R, HEAD_DIM + LANES), F32),
            pltpu.VMEM((R, tk), F32),
            pltpu.VMEM((R, tk), F32),
            pltpu.VMEM((Q_BLOCK, qw), F32),
        ],
        compiler_params=_cparams(("arbitrary", "arbitrary")),
        name="nsa",
    )(qb, proj, proj, selb, pcmp, wb, cb, kvb, kvb, kvb, kvb)


def _merge_kernel(ohg_ref, onsa_ref, g0_ref, g1_ref, g2_ref, g3_ref, whg_ref, wnsa_ref, o_ref):
    y_hg = _dot(ohg_ref[...], whg_ref[...])
    y_nsa = _dot(onsa_ref[...], wnsa_ref[...])
    half = D_MODEL // 2
    gh = (g0_ref, g1_ref)
    gn = (g2_ref, g3_ref)
    for c in range(2):
        cs = slice(c * half, (c + 1) * half)
        m = jax.nn.sigmoid(gh[c][...]) * y_hg[:, cs] + jax.nn.sigmoid(gn[c][...]) * y_nsa[:, cs]
        o_ref[:, cs] = m.astype(o_ref.dtype)


def _merge(o_hg, o_nsa, proj, w_hg, w_nsa, tm=512):
    S = o_hg.shape[0]
    tm = min(tm, S)
    half = D_MODEL // 2
    gb = OFF_MG // half
    return pl.pallas_call(
        _merge_kernel,
        grid=(S // tm,),
        in_specs=[
            pl.BlockSpec((tm, HG_WIDTH), lambda i: (i, 0)),
            pl.BlockSpec((tm, NSA_WIDTH), lambda i: (i, 0)),
            pl.BlockSpec((tm, half), lambda i: (i, gb + 0)),
            pl.BlockSpec((tm, half), lambda i: (i, gb + 1)),
            pl.BlockSpec((tm, half), lambda i: (i, gb + 2)),
            pl.BlockSpec((tm, half), lambda i: (i, gb + 3)),
            pl.BlockSpec((HG_WIDTH, D_MODEL), lambda i: (0, 0)),
            pl.BlockSpec((NSA_WIDTH, D_MODEL), lambda i: (0, 0)),
        ],
        out_specs=pl.BlockSpec((tm, D_MODEL), lambda i: (i, 0)),
        out_shape=jax.ShapeDtypeStruct((S, D_MODEL), BF16),
        compiler_params=_cparams(("arbitrary",)),
        name="merge",
    )(o_hg, o_nsa, proj, proj, proj, proj, w_hg, w_nsa)


def _out_kernel(x_ref, m_ref, w_ref, nw_ref, o_ref):
    h = x_ref[...] + _dot(m_ref[...], w_ref[...])
    ms = jnp.mean(h * h, axis=-1, keepdims=True)
    o_ref[...] = h * lax.rsqrt(ms + EPS) * nw_ref[...]


def _out(x2, merged, w_out, final_w, tm=512):
    S, D = x2.shape
    tm = min(tm, S)
    return pl.pallas_call(
        _out_kernel,
        grid=(S // tm,),
        in_specs=[
            pl.BlockSpec((tm, D), lambda i: (i, 0)),
            pl.BlockSpec((tm, D), lambda i: (i, 0)),
            pl.BlockSpec((D, D), lambda i: (0, 0)),
            pl.BlockSpec((1, D), lambda i: (0, 0)),
        ],
        out_specs=pl.BlockSpec((tm, D), lambda i: (i, 0)),
        out_shape=jax.ShapeDtypeStruct((S, D), F32),
        compiler_params=_cparams(("arbitrary",)),
        name="out_proj",
    )(x2, merged, w_out, final_w.reshape(1, D))


def _layer(x2, norm_w, w_in, lb_logits, hg_norm_w, cmp_k_pos, cmp_k_w1, cmp_k_b1, cmp_k_w2,
           cmp_v_pos, cmp_v_w1, cmp_v_b1, cmp_v_w2, w_branch_hg, w_branch_nsa, w_out, final_w):
    S = x2.shape[0]
    proj = _norm_proj(x2, norm_w, _w_prep(w_in.T))
    cosf, sinf = _rope_tables(S)
    segs, kvb = _kv_prep(proj, cosf, sinf)
    pe = jnp.stack([cmp_k_pos.reshape(1, -1), cmp_v_pos.reshape(1, -1)])
    w1 = jnp.stack([cmp_k_w1, cmp_v_w1]).astype(BF16)
    b1 = jnp.stack([cmp_k_b1.reshape(1, -1), cmp_v_b1.reshape(1, -1)])
    w2 = jnp.stack([cmp_k_w2, cmp_v_w2]).astype(BF16)
    kcvc = _compress(segs, pe, w1, b1, w2)
    o_hg = _hgrn(proj, lb_logits, hg_norm_w)
    selb, pcmp, qb = _nsa_select(proj, cosf, sinf, kcvc)
    o_nsa = _nsa(proj, qb, selb, pcmp, kvb)
    merged = _merge(o_hg, o_nsa, proj, w_branch_hg.astype(BF16), w_branch_nsa.astype(BF16))
    return _out(x2, merged, w_out.astype(BF16), final_w)


def kernel(x, norm_w, w_in, hg_lb_logits, hg_norm_w, cmp_k_pos, cmp_k_w1, cmp_k_b1, cmp_k_w2, cmp_v_pos, cmp_v_w1, cmp_v_b1, cmp_v_w2, w_branch_hg, w_branch_nsa, w_out, final_norm_w):
    B, S, D = x.shape
    assert B == 1 and D == D_MODEL and norm_w.shape[0] == 1
    y = _layer(x[0], norm_w[0], w_in[0], hg_lb_logits, hg_norm_w[0], cmp_k_pos[0], cmp_k_w1[0], cmp_k_b1[0],
               cmp_k_w2[0], cmp_v_pos[0], cmp_v_w1[0], cmp_v_b1[0], cmp_v_w2[0], w_branch_hg[0],
               w_branch_nsa[0], w_out[0], final_norm_w)
    return y[None]
```
